```python
import math
import jax
import jax.numpy as jnp
from jax import lax
import numpy as np

D_MODEL = 2048
BATCH = 4
SEQ = 2048
DEPTH = 2
DEC_BATCH = 32
DEC_SEQ = 8
PAST_LEN = 8192
PAGE_SIZE = 128

ATTN_HEADS = 8
ATTN_KV_HEADS = 4
ATTN_GQ = ATTN_HEADS // ATTN_KV_HEADS
ATTN_DH = 64
ATTN_VD = 2 * ATTN_DH
ATTN_WIDTH = ATTN_HEADS * ATTN_VD
ROPE_THETA = 10000.0
ATTN_QBLOCK = 128

SSD_WIDTH = D_MODEL // 4
SSD_HEAD_DIM = 64
SSD_HEADS = SSD_WIDTH // SSD_HEAD_DIM
SSD_GROUPS = 2
SSD_STATE = 128
SSD_CONV = 4
SSD_CHUNK = 128
SSD_CONV_CH = SSD_WIDTH + 2 * SSD_GROUPS * SSD_STATE

LRU_WIDTH = D_MODEL // 4
LRU_BLOCKS = 8
LRU_BLOCK_DIM = LRU_WIDTH // LRU_BLOCKS
LRU_CONV = 4
LRU_C = 8.0

MIX_WIDTH = ATTN_WIDTH + SSD_WIDTH + LRU_WIDTH
Q_COLS = ATTN_HEADS * 2 * ATTN_DH
K_COLS = ATTN_KV_HEADS * 2 * ATTN_DH
V_COLS = ATTN_KV_HEADS * ATTN_VD
IN_WIDTH = Q_COLS + K_COLS + V_COLS + SSD_WIDTH + SSD_CONV_CH + SSD_HEADS + 2 * LRU_WIDTH

MEM_TOKENS = 256
MEM_HEADS = 4
MEM_DH = 128
MEM_WIDTH = MEM_HEADS * MEM_DH

MOE_GROUPS = 4
MOE_EXPERTS_PER_GROUP = 8
MOE_EXPERTS = MOE_GROUPS * MOE_EXPERTS_PER_GROUP
MOE_TOPK = 2
MOE_D_FF = 512
MOE_BLOCK = 128

kernel_name = 'hybrid_parallel_heads_decode_step'


def rmsnorm(x, w, eps=1e-6):
    xf = x.astype(jnp.float32)
    y = xf * lax.rsqrt(jnp.mean(xf * xf, axis=-1, keepdims=True) + eps)
    return y.astype(x.dtype) * w


def rope(x, pos):
    half = x.shape[-1] // 2
    inv = ROPE_THETA ** (-jnp.arange(half, dtype=jnp.float32) / half)
    ang = pos.astype(jnp.float32)[:, None] * inv[None, :]
    shp = (1, pos.shape[0]) + (1,) * (x.ndim - 3) + (half,)
    cos = jnp.cos(ang).reshape(shp)
    sin = jnp.sin(ang).reshape(shp)
    x1 = x[..., :half].astype(jnp.float32)
    x2 = x[..., half:].astype(jnp.float32)
    return jnp.concatenate([x1 * cos - x2 * sin, x2 * cos + x1 * sin], axis=-1).astype(x.dtype)


def causal_dwconv(x_ext, w, b):
    width = w.shape[0]
    t = x_ext.shape[1] - width + 1
    out = b
    for j in range(width):
        out = out + x_ext[:, j:j + t] * w[j]
    return out


def split_projection(u):
    sizes = (Q_COLS, K_COLS, V_COLS, SSD_WIDTH, SSD_CONV_CH, SSD_HEADS, LRU_WIDTH)
    offs = []
    acc = 0
    for s in sizes:
        acc += s
        offs.append(acc)
    return jnp.split(u, offs, axis=-1)


def diff_attn_block(q, q_pos, k, v, k_pos, lam):
    s = jnp.einsum('bqkgcd,btkcd->bckgqt', q, k).astype(jnp.float32) * (ATTN_DH ** -0.5)
    causal = k_pos[None, :] <= q_pos[:, None]
    s = jnp.where(causal, s, -jnp.inf)
    p = jax.nn.softmax(s, axis=-1)
    a = p[:, 0] - lam * p[:, 1]
    return jnp.einsum('bkgqt,btkv->bqkgv', a.astype(v.dtype), v)


def diff_attention(q, k, v, q_pos, k_pos, lam):
    bsz, tq = q.shape[0], q.shape[1]
    if tq > ATTN_QBLOCK and tq % ATTN_QBLOCK == 0:
        nb = tq // ATTN_QBLOCK
        qb = q.reshape((bsz, nb, ATTN_QBLOCK) + q.shape[2:]).swapaxes(0, 1)
        pb = q_pos.reshape(nb, ATTN_QBLOCK)
        o = lax.map(lambda args: diff_attn_block(args[0], args[1], k, v, k_pos, lam), (qb, pb))
        return o.swapaxes(0, 1).reshape((bsz, tq) + o.shape[3:])
    return diff_attn_block(q, q_pos, k, v, k_pos, lam)


def ssd_scan(xh, dt, a, bh, ch, h0):
    bsz, t = xh.shape[0], xh.shape[1]
    blk = SSD_CHUNK if t % SSD_CHUNK == 0 else t
    nc = t // blk
    causal = jnp.tril(jnp.ones((blk, blk), dtype=bool))[None, :, :, None]

    def to_chunks(arr):
        return arr.reshape((bsz, nc, blk) + arr.shape[2:]).swapaxes(0, 1)

    def step(h, inp):
        xc, dtc, bc, cc = inp
        acum = jnp.cumsum(dtc * a, axis=1)
        seg = acum[:, :, None, :] - acum[:, None, :, :]
        decay = jnp.exp(jnp.where(causal, seg, -jnp.inf))
        w_ts = jnp.einsum('bthn,bshn->btsh', cc, bc) * decay * dtc[:, None, :, :]
        y = jnp.einsum('btsh,bshp->bthp', w_ts, xc)
        y = y + jnp.einsum('bthn,bhpn->bthp', cc, h) * jnp.exp(acum)[..., None]
        w_end = jnp.exp(acum[:, -1:, :] - acum) * dtc
        h_new = h * jnp.exp(acum[:, -1])[:, :, None, None] + jnp.einsum('bsh,bshp,bshn->bhpn', w_end, xc, bc)
        return h_new.astype(jnp.float32), y

    h, ys = lax.scan(step, h0.astype(jnp.float32), (to_chunks(xh), to_chunks(dt), to_chunks(bh), to_chunks(ch)))
    return ys.swapaxes(0, 1).reshape((bsz, t) + ys.shape[3:]), h


def lru_combine(left, right):
    a1, b1 = left
    a2, b2 = right
    return a1 * a2, a2 * b1 + b2


def hier_moe(h, p):
    bsz, t, d = h.shape
    n_tok = bsz * t
    xt = h.reshape(n_tok, d)
    pg = jax.nn.softmax((xt @ p['router_group_w'] + p['router_group_b']).astype(jnp.float32), axis=-1)
    p_grp, g_idx = lax.top_k(pg, 1)
    le = (xt @ p['router_expert_w'] + p['router_expert_b']).astype(jnp.float32)
    le = le.reshape(n_tok, MOE_GROUPS, MOE_EXPERTS_PER_GROUP)[jnp.arange(n_tok), g_idx[:, 0]]
    top_v, top_i = lax.top_k(le, MOE_TOPK)
    gates = p_grp * jax.nn.softmax(top_v, axis=-1)
    eids = g_idx * MOE_EXPERTS_PER_GROUP + top_i
    n_asg = n_tok * MOE_TOPK
    flat_e = eids.reshape(n_asg)
    flat_t = jnp.repeat(jnp.arange(n_tok, dtype=jnp.int32), MOE_TOPK)
    flat_g = gates.reshape(n_asg)
    order = jnp.argsort(flat_e)
    se = flat_e[order]
    counts = jnp.zeros((MOE_EXPERTS,), jnp.int32).at[flat_e].add(1)
    padded = (counts + MOE_BLOCK - 1) // MOE_BLOCK * MOE_BLOCK
    pad_end = jnp.cumsum(padded)
    pad_start = pad_end - padded
    start = jnp.cumsum(counts) - counts
    dest = pad_start[se] + (jnp.arange(n_asg, dtype=jnp.int32) - start[se])
    n_blocks = -(-(n_asg + MOE_EXPERTS * (MOE_BLOCK - 1)) // MOE_BLOCK)
    n_rows = n_blocks * MOE_BLOCK
    row_tok = jnp.full((n_rows,), n_tok, jnp.int32).at[dest].set(flat_t[order])
    row_gate = jnp.zeros((n_rows,), jnp.float32).at[dest].set(flat_g[order])
    blk_e = jnp.minimum(jnp.searchsorted(pad_end, jnp.arange(n_blocks, dtype=jnp.int32) * MOE_BLOCK, side='right'), MOE_EXPERTS - 1)
    xrows = jnp.concatenate([xt, jnp.zeros((1, d), xt.dtype)], axis=0)[row_tok].reshape(n_blocks, MOE_BLOCK, d)
    wg, wu, wd = p['moe_w_gate'], p['moe_w_up'], p['moe_w_down']

    def expert_block(args):
        xb, e = args
        return (jax.nn.silu(xb @ wg[e]) * (xb @ wu[e])) @ wd[e]

    out = lax.map(expert_block, (xrows, blk_e)).reshape(n_rows, d)
    y = jax.ops.segment_sum(out * row_gate[:, None], row_tok, num_segments=n_tok + 1)[:n_tok]
    return y.reshape(bsz, t, d).astype(h.dtype)


def hybrid_layer(x, pos, past_k, past_v, mem_k, mem_v, conv_ssd0, h_ssd0, conv_lru0, h_lru0, p, lam_init):
    bsz, t, _ = x.shape
    f32 = jnp.float32
    h = rmsnorm(x, p['norm_mix'])
    q, k, v, z, xbc, dt_raw, xr, gate = split_projection(h @ p['w_in'])

    q = rope(q.reshape(bsz, t, ATTN_HEADS, 2, ATTN_DH), pos).reshape(bsz, t, ATTN_KV_HEADS, ATTN_GQ, 2, ATTN_DH)
    k_new = rope(k.reshape(bsz, t, ATTN_KV_HEADS, 2, ATTN_DH), pos).reshape(bsz, t, ATTN_KV_HEADS, 2 * ATTN_DH)
    v_new = v.reshape(bsz, t, ATTN_KV_HEADS, ATTN_VD)
    if past_k is None:
        k_all, v_all = k_new, v_new
    else:
        k_all = jnp.concatenate([past_k, k_new.astype(past_k.dtype)], axis=1)
        v_all = jnp.concatenate([past_v, v_new.astype(past_v.dtype)], axis=1)
    k_pos = jnp.arange(k_all.shape[1], dtype=jnp.int32)
    lam = (jnp.exp(jnp.sum(p['lq1'].astype(f32) * p['lk1']))
           - jnp.exp(jnp.sum(p['lq2'].astype(f32) * p['lk2'])) + lam_init)
    o = diff_attention(q, k_all.reshape(bsz, -1, ATTN_KV_HEADS, 2, ATTN_DH), v_all, pos, k_pos, lam)
    attn_out = (rmsnorm(o, p['attn_subln']) * (1.0 - lam_init)).reshape(bsz, t, ATTN_WIDTH)

    conv_in = jnp.concatenate([conv_ssd0.astype(xbc.dtype), xbc], axis=1)
    xbc_c = jax.nn.silu(causal_dwconv(conv_in, p['ssd_conv_w'], p['ssd_conv_b']))
    new_conv_ssd = conv_in[:, -(SSD_CONV - 1):]
    xs, bs, cs = jnp.split(xbc_c, [SSD_WIDTH, SSD_WIDTH + SSD_GROUPS * SSD_STATE], axis=-1)
    xh = xs.reshape(bsz, t, SSD_HEADS, SSD_HEAD_DIM)
    rep = SSD_HEADS // SSD_GROUPS
    bh = jnp.repeat(bs.reshape(bsz, t, SSD_GROUPS, SSD_STATE), rep, axis=2)
    ch = jnp.repeat(cs.reshape(bsz, t, SSD_GROUPS, SSD_STATE), rep, axis=2)
    dt = jax.nn.softplus(dt_raw.astype(f32) + p['ssd_dt_bias'])
    a = -jnp.exp(p['ssd_a_log'].astype(f32))
    y, h_ssd = ssd_scan(xh, dt, a, bh, ch, h_ssd0)
    y = y + p['ssd_d'][:, None] * xh
    ssd_out = rmsnorm(y.reshape(bsz, t, SSD_WIDTH) * jax.nn.silu(z), p['ssd_norm'])

    conv_in = jnp.concatenate([conv_lru0.astype(xr.dtype), xr], axis=1)
    xc = causal_dwconv(conv_in, p['lru_conv_w'], p['lru_conv_b'])
    new_conv_lru = conv_in[:, -(LRU_CONV - 1):]
    xb = xc.reshape(bsz, t, LRU_BLOCKS, LRU_BLOCK_DIM)
    r = jax.nn.sigmoid(jnp.einsum('btgi,gij->btgj', xb, p['lru_wa']) + p['lru_ba'])
    i_g = jax.nn.sigmoid(jnp.einsum('btgi,gij->btgj', xb, p['lru_wx']) + p['lru_bx'])
    log_a = -LRU_C * r.astype(f32) * jax.nn.softplus(-p['lru_lambda'].astype(f32)).reshape(LRU_BLOCKS, LRU_BLOCK_DIM)
    a_t = jnp.exp(log_a).reshape(bsz, t, LRU_WIDTH)
    b_t = (jnp.sqrt(-jnp.expm1(2.0 * log_a)) * (i_g * xb)).reshape(bsz, t, LRU_WIDTH)
    b_t = b_t.at[:, 0].add(a_t[:, 0] * h_lru0)
    _, hs = lax.associative_scan(lru_combine, (a_t, b_t), axis=1)
    h_lru = hs[:, -1]
    lru_out = hs * jax.nn.gelu(gate)

    x = x + jnp.concatenate([attn_out, ssd_out, lru_out], axis=-1) @ p['w_out']

    hm = rmsnorm(x, p['norm_mem'])
    qm = (hm @ p['wq_mem']).reshape(bsz, t, MEM_HEADS, MEM_DH)
    sm = jnp.einsum('bqhd,bmhd->bhqm', qm, mem_k).astype(f32) * (MEM_DH ** -0.5)
    pm = jax.nn.softmax(sm, axis=-1)
    om = jnp.einsum('bhqm,bmhd->bqhd', pm.astype(mem_v.dtype), mem_v).reshape(bsz, t, MEM_WIDTH)
    x = x + om @ p['wo_mem']

    x = x + hier_moe(rmsnorm(x, p['norm_ffn']), p)
    return x, (k_new, v_new, new_conv_ssd, h_ssd, new_conv_lru, h_lru)


def setup_inputs(seed: int = 0) -> dict:
    key = jax.random.key(seed)
    ks = jax.random.split(key, 64)
    f32 = jnp.float32

    def nrm(i, shape, scale=1.0):
        return scale * jax.random.normal(ks[i], shape, f32)

    def gain(i, shape):
        return 1.0 + nrm(i, shape, 0.02)

    n_pages = PAST_LEN // PAGE_SIZE
    n_pool = (DEC_BATCH * n_pages * 5) // 4
    page_table = jax.random.permutation(ks[0], n_pool)[:DEC_BATCH * n_pages].reshape(DEC_BATCH, n_pages).astype(jnp.int32)
    dt0 = jnp.exp(jax.random.uniform(ks[1], (DEPTH, SSD_HEADS), f32, math.log(1e-3), math.log(1e-1)))
    ssd_dt_bias = dt0 + jnp.log(-jnp.expm1(-dt0))
    ssd_a_log = jnp.log(jax.random.uniform(ks[2], (DEPTH, SSD_HEADS), f32, 1.0, 16.0))
    a0 = jax.random.uniform(ks[3], (DEPTH, LRU_WIDTH), f32, 0.9, 0.999) ** (1.0 / LRU_C)
    lru_lambda = jnp.log(a0) - jnp.log1p(-a0)
    dinv = D_MODEL ** -0.5
    return {
        'x_prompt': nrm(4, (BATCH, SEQ, D_MODEL)),
        'x_sample': nrm(5, (DEC_BATCH, DEC_SEQ, D_MODEL)),
        'cache_k': nrm(6, (DEPTH, n_pool, PAGE_SIZE, ATTN_KV_HEADS, 2 * ATTN_DH)),
        'cache_v': nrm(7, (DEPTH, n_pool, PAGE_SIZE, ATTN_KV_HEADS, ATTN_VD)),
        'cache_mem_k': nrm(8, (DEPTH, DEC_BATCH, MEM_TOKENS, MEM_HEADS, MEM_DH)),
        'cache_mem_v': nrm(9, (DEPTH, DEC_BATCH, MEM_TOKENS, MEM_HEADS, MEM_DH)),
        'state_ssd_conv': nrm(10, (DEPTH, DEC_BATCH, SSD_CONV - 1, SSD_CONV_CH)),
        'state_ssd': nrm(11, (DEPTH, DEC_BATCH, SSD_HEADS, SSD_HEAD_DIM, SSD_STATE), 0.5),
        'state_lru_conv': nrm(12, (DEPTH, DEC_BATCH, LRU_CONV - 1, LRU_WIDTH)),
        'state_lru': nrm(13, (DEPTH, DEC_BATCH, LRU_WIDTH), 0.5),
        'page_table': page_table,
        'mem_prompt': nrm(14, (BATCH, MEM_TOKENS, D_MODEL)),
        'norm_mix': gain(15, (DEPTH, D_MODEL)),
        'w_in': nrm(16, (DEPTH, D_MODEL, IN_WIDTH), dinv),
        'attn_lambda_q1': nrm(17, (DEPTH, ATTN_DH), 0.1),
        'attn_lambda_k1': nrm(18, (DEPTH, ATTN_DH), 0.1),
        'attn_lambda_q2': nrm(19, (DEPTH, ATTN_DH), 0.1),
        'attn_lambda_k2': nrm(20, (DEPTH, ATTN_DH), 0.1),
        'attn_subln': gain(21, (DEPTH, ATTN_VD)),
        'ssd_conv_w': nrm(22, (DEPTH, SSD_CONV, SSD_CONV_CH), SSD_CONV ** -0.5),
        'ssd_conv_b': nrm(23, (DEPTH, SSD_CONV_CH), 0.02),
        'ssd_dt_bias': ssd_dt_bias,
        'ssd_a_log': ssd_a_log,
        'ssd_d': gain(24, (DEPTH, SSD_HEADS)),
        'ssd_norm': gain(25, (DEPTH, SSD_WIDTH)),
        'lru_conv_w': nrm(26, (DEPTH, LRU_CONV, LRU_WIDTH), LRU_CONV ** -0.5),
        'lru_conv_b': nrm(27, (DEPTH, LRU_WIDTH), 0.02),
        'lru_wa': nrm(28, (DEPTH, LRU_BLOCKS, LRU_BLOCK_DIM, LRU_BLOCK_DIM), LRU_BLOCK_DIM ** -0.5),
        'lru_ba': nrm(29, (DEPTH, LRU_BLOCKS, LRU_BLOCK_DIM), 0.02),
        'lru_wx': nrm(30, (DEPTH, LRU_BLOCKS, LRU_BLOCK_DIM, LRU_BLOCK_DIM), LRU_BLOCK_DIM ** -0.5),
        'lru_bx': nrm(31, (DEPTH, LRU_BLOCKS, LRU_BLOCK_DIM), 0.02),
        'lru_lambda': lru_lambda,
        'w_out': nrm(32, (DEPTH, MIX_WIDTH, D_MODEL), MIX_WIDTH ** -0.5),
        'norm_mem': gain(33, (DEPTH, D_MODEL)),
        'wq_mem': nrm(34, (DEPTH, D_MODEL, MEM_WIDTH), dinv),
        'wk_mem': nrm(35, (DEPTH, D_MODEL, MEM_WIDTH), dinv),
        'wv_mem': nrm(36, (DEPTH, D_MODEL, MEM_WIDTH), dinv),
        'wo_mem': nrm(37, (DEPTH, MEM_WIDTH, D_MODEL), MEM_WIDTH ** -0.5),
        'norm_ffn': gain(38, (DEPTH, D_MODEL)),
        'router_group_w': nrm(39, (DEPTH, D_MODEL, MOE_GROUPS), dinv),
        'router_group_b': nrm(40, (DEPTH, MOE_GROUPS), 0.01),
        'router_expert_w': nrm(41, (DEPTH, D_MODEL, MOE_EXPERTS), dinv),
        'router_expert_b': nrm(42, (DEPTH, MOE_EXPERTS), 0.01),
        'moe_w_gate': nrm(43, (DEPTH, MOE_EXPERTS, D_MODEL, MOE_D_FF), dinv),
        'moe_w_up': nrm(44, (DEPTH, MOE_EXPERTS, D_MODEL, MOE_D_FF), dinv),
        'moe_w_down': nrm(45, (DEPTH, MOE_EXPERTS, MOE_D_FF, D_MODEL), MOE_D_FF ** -0.5),
        'final_norm': gain(46, (D_MODEL,)),
    }


def reference(x_prompt, x_sample, cache_k, cache_v, cache_mem_k, cache_mem_v, state_ssd_conv, state_ssd,
              state_lru_conv, state_lru, page_table, mem_prompt, norm_mix, w_in, attn_lambda_q1, attn_lambda_k1,
              attn_lambda_q2, attn_lambda_k2, attn_subln, ssd_conv_w, ssd_conv_b, ssd_dt_bias, ssd_a_log, ssd_d,
              ssd_norm, lru_conv_w, lru_conv_b, lru_wa, lru_ba, lru_wx, lru_bx, lru_lambda, w_out, norm_mem,
              wq_mem, wk_mem, wv_mem, wo_mem, norm_ffn, router_group_w, router_group_b, router_expert_w,
              router_expert_b, moe_w_gate, moe_w_up, moe_w_down, final_norm):
    bp, tp = x_prompt.shape[0], x_prompt.shape[1]
    bsm, tsm = x_sample.shape[0], x_sample.shape[1]
    n_mem = mem_prompt.shape[1]
    n_pages = page_table.shape[1]
    past_len = n_pages * PAGE_SIZE
    pos_p = jnp.arange(tp, dtype=jnp.int32)
    pos_s = past_len + jnp.arange(tsm, dtype=jnp.int32)
    xp, xs = x_prompt, x_sample
    kp, vp, mkp, mvp, scp, ssp, lcp, lsp = [], [], [], [], [], [], [], []
    ksm, vsm, scs, sss, lcs, lss = [], [], [], [], [], []
    for l in range(DEPTH):
        p = {
            'norm_mix': norm_mix[l], 'w_in': w_in[l],
            'lq1': attn_lambda_q1[l], 'lk1': attn_lambda_k1[l], 'lq2': attn_lambda_q2[l], 'lk2': attn_lambda_k2[l],
            'attn_subln': attn_subln[l],
            'ssd_conv_w': ssd_conv_w[l], 'ssd_conv_b': ssd_conv_b[l], 'ssd_dt_bias': ssd_dt_bias[l],
            'ssd_a_log': ssd_a_log[l], 'ssd_d': ssd_d[l], 'ssd_norm': ssd_norm[l],
            'lru_conv_w': lru_conv_w[l], 'lru_conv_b': lru_conv_b[l], 'lru_wa': lru_wa[l], 'lru_ba': lru_ba[l],
            'lru_wx': lru_wx[l], 'lru_bx': lru_bx[l], 'lru_lambda': lru_lambda[l],
            'w_out': w_out[l], 'norm_mem': norm_mem[l], 'wq_mem': wq_mem[l], 'wo_mem': wo_mem[l],
            'norm_ffn': norm_ffn[l], 'router_group_w': router_group_w[l], 'router_group_b': router_group_b[l],
            'router_expert_w': router_expert_w[l], 'router_expert_b': router_expert_b[l],
            'moe_w_gate': moe_w_gate[l], 'moe_w_up': moe_w_up[l], 'moe_w_down': moe_w_down[l],
        }
        lam_init = 0.8 - 0.6 * math.exp(-0.3 * l)

        mem_k = (mem_prompt @ wk_mem[l]).reshape(bp, n_mem, MEM_HEADS, MEM_DH)
        mem_v = (mem_prompt @ wv_mem[l]).reshape(bp, n_mem, MEM_HEADS, MEM_DH)
        xp, st = hybrid_layer(
            xp, pos_p, None, None, mem_k, mem_v,
            jnp.zeros((bp, SSD_CONV - 1, SSD_CONV_CH), x_prompt.dtype),
            jnp.zeros((bp, SSD_HEADS, SSD_HEAD_DIM, SSD_STATE), jnp.float32),
            jnp.zeros((bp, LRU_CONV - 1, LRU_WIDTH), x_prompt.dtype),
            jnp.zeros((bp, LRU_WIDTH), jnp.float32),
            p, lam_init)
        kp.append(st[0]); vp.append(st[1]); mkp.append(mem_k); mvp.append(mem_v)
        scp.append(st[2]); ssp.append(st[3]); lcp.append(st[4]); lsp.append(st[5])

        past_k = cache_k[l, page_table].reshape(bsm, past_len, ATTN_KV_HEADS, 2 * ATTN_DH)
        past_v = cache_v[l, page_table].reshape(bsm, past_len, ATTN_KV_HEADS, ATTN_VD)
        xs, st = hybrid_layer(
            xs, pos_s, past_k, past_v, cache_mem_k[l], cache_mem_v[l],
            state_ssd_conv[l], state_ssd[l], state_lru_conv[l], state_lru[l], p, lam_init)
        ksm.append(st[0]); vsm.append(st[1]); scs.append(st[2]); sss.append(st[3]); lcs.append(st[4]); lss.append(st[5])

    y_prompt = rmsnorm(xp, final_norm)
    y_sample = rmsnorm(xs, final_norm)
    return (y_prompt, y_sample,
            jnp.stack(kp), jnp.stack(vp), jnp.stack(mkp), jnp.stack(mvp),
            jnp.stack(scp), jnp.stack(ssp), jnp.stack(lcp), jnp.stack(lsp),
            jnp.stack(ksm), jnp.stack(vsm), jnp.stack(scs), jnp.stack(sss), jnp.stack(lcs), jnp.stack(lss))
```

```python
import functools
import math

import jax
import jax.numpy as jnp
from jax import lax
from jax.experimental import pallas as pl
from jax.experimental.pallas import tpu as pltpu

F32, BF16, I32 = jnp.float32, jnp.bfloat16, jnp.int32
EPS = 1e-6
LANES = 128
SUBLANES = 8
MIB = 1024 * 1024

ATTN_HEADS, ATTN_KV_HEADS, ATTN_DH = 8, 4, 64
ATTN_VD = 2 * ATTN_DH
ROPE_THETA = 10000.0
SSD_HEADS, SSD_HEAD_DIM, SSD_GROUPS, SSD_STATE, SSD_CONV = 8, 64, 2, 128, 4
SSD_CHUNK = 128
LRU_BLOCKS, LRU_CONV, LRU_C = 8, 4, 8.0
MEM_HEADS, MEM_DH = 4, 128
MOE_GROUPS, MOE_EXPERTS_PER_GROUP, MOE_TOPK = 4, 8, 2
MOE_EXPERTS = MOE_GROUPS * MOE_EXPERTS_PER_GROUP

Q0, K0, V0, XBC0, Z0, XR0, GATE0, DT0, UW = 0, 1024, 1536, 2048, 3072, 3584, 4096, 4608, 5120
PROJ_TN = 1024


def _cparams(sem, vmem_mib):
    return pltpu.CompilerParams(dimension_semantics=sem, vmem_limit_bytes=vmem_mib * MIB)


def _tile(n, prefs):
    for p in prefs:
        if n % p == 0:
            return p
    return n


def _rms(x, w):
    return (x * lax.rsqrt(jnp.mean(x * x, axis=-1, keepdims=True) + EPS)) * w


def _nt(a, b):
    return lax.dot_general(a, b, (((1,), (1,)), ((), ())), preferred_element_type=F32)


def _mm(a, b):
    return jnp.dot(a, b, preferred_element_type=F32)


def _rope128(yc, cos, sin):
    lane = lax.broadcasted_iota(I32, yc.shape, 1)
    sw = jnp.where((lane % 64) < 32, pltpu.roll(yc, 96, 1), pltpu.roll(yc, 32, 1))
    return yc * cos + sw * sin


def _in_proj_body(x_ref, nw_ref, w_ref, cos_ref, sin_ref, o_ref):
    j = pl.program_id(0)
    h = _rms(x_ref[...], nw_ref[...])
    y = _mm(h.astype(BF16), w_ref[...])
    n_chunks = y.shape[1] // LANES

    def store_rope(n_rope):
        cos, sin = cos_ref[...], sin_ref[...]
        for c in range(n_rope):
            o_ref[:, c * LANES:(c + 1) * LANES] = _rope128(y[:, c * LANES:(c + 1) * LANES], cos, sin)
        if n_rope < n_chunks:
            o_ref[:, n_rope * LANES:] = y[:, n_rope * LANES:]

    @pl.when(j == 0)
    def _():
        store_rope(n_chunks)

    @pl.when(j == 1)
    def _():
        store_rope((V0 - K0) // LANES)

    @pl.when(j >= 2)
    def _():
        o_ref[...] = y


def _in_proj(x, nw, w_pad, cos, sin):
    t, d = x.shape
    tm = _tile(t, (512, 256, 128))
    return pl.pallas_call(
        _in_proj_body,
        grid=(UW // PROJ_TN, t // tm),
        in_specs=[pl.BlockSpec((tm, d), lambda j, i: (i, 0)),
                  pl.BlockSpec((1, d), lambda j, i: (0, 0)),
                  pl.BlockSpec((d, PROJ_TN), lambda j, i: (0, j)),
                  pl.BlockSpec((tm, LANES), lambda j, i: (i, 0)),
                  pl.BlockSpec((tm, LANES), lambda j, i: (i, 0))],
        out_specs=pl.BlockSpec((tm, PROJ_TN), lambda j, i: (i, j)),
        out_shape=jax.ShapeDtypeStruct((t, UW), F32),
        compiler_params=_cparams(("arbitrary", "arbitrary"), 48),
        name="in_proj",
    )(x, nw, w_pad, cos, sin)


def _lambda(lq1, lk1, lq2, lk2, lam_init):
    s1 = jnp.sum(lq1[...] * lk1[...], axis=-1, keepdims=True)
    s2 = jnp.sum(lq2[...] * lk2[...], axis=-1, keepdims=True)
    return jnp.exp(s1) - jnp.exp(s2) + lam_init


def _attn_prompt_body(lq1, lk1, lq2, lk2, sub_ref, q_ref, k_ref, v_ref, o_ref, m_sc, l_sc, acc_sc, *, tq, lam_init):
    i = pl.program_id(2)
    lam = _lambda(lq1, lk1, lq2, lk2, lam_init)
    q = q_ref[...] * (ATTN_DH ** -0.5)
    lane = lax.broadcasted_iota(I32, (tq, LANES), 1)
    parts = []
    for g in range(2):
        qh = q[:, g * LANES:(g + 1) * LANES]
        parts.append(jnp.where(lane < ATTN_DH, qh, 0.0))
        parts.append(jnp.where(lane >= ATTN_DH, qh, 0.0))
    q4 = jnp.concatenate(parts, axis=0).astype(BF16)
    n4 = 4 * tq

    m_sc[...] = jnp.full_like(m_sc, -jnp.inf)
    l_sc[...] = jnp.zeros_like(l_sc)
    acc_sc[...] = jnp.zeros_like(acc_sc)

    def block(kb, masked):
        start = pl.multiple_of(kb * tq, tq)
        k = k_ref[pl.ds(start, tq), :].astype(BF16)
        v = v_ref[pl.ds(start, tq), :].astype(BF16)
        s = _nt(q4, k)
        if masked:
            r = jnp.bitwise_and(lax.broadcasted_iota(I32, (n4, tq), 0), tq - 1)
            c = lax.broadcasted_iota(I32, (n4, tq), 1)
            s = jnp.where(c <= r, s, -jnp.inf)
        m_old = m_sc[...]
        m_new = jnp.maximum(m_old, jnp.max(s, axis=1, keepdims=True))
        alpha = jnp.exp(m_old - m_new)
        p = jnp.exp(s - m_new)
        m_sc[...] = m_new
        l_sc[...] = alpha * l_sc[...] + jnp.sum(p, axis=1, keepdims=True)
        acc_sc[...] = alpha * acc_sc[...] + _mm(p.astype(BF16), v)

    def body(kb, carry):
        block(kb, False)
        return carry

    lax.fori_loop(0, i, body, 0)
    block(i, True)
    o = acc_sc[...] / l_sc[...]
    for g in range(2):
        og = o[(2 * g) * tq:(2 * g + 1) * tq] - lam * o[(2 * g + 1) * tq:(2 * g + 2) * tq]
        o_ref[:, g * LANES:(g + 1) * LANES] = _rms(og, sub_ref[...]) * (1.0 - lam_init)


def _attn_prompt(u, lams, subln, bsz, seq, lam_init):
    tq = _tile(seq, (256, 128))
    nq = seq // tq
    vec = pl.BlockSpec((1, ATTN_DH), lambda b, h, i: (0, 0))
    return pl.pallas_call(
        functools.partial(_attn_prompt_body, tq=tq, lam_init=lam_init),
        grid=(bsz, ATTN_KV_HEADS, nq),
        in_specs=[vec, vec, vec, vec,
                  pl.BlockSpec((1, ATTN_VD), lambda b, h, i: (0, 0)),
                  pl.BlockSpec((tq, 2 * LANES), lambda b, h, i: (b * nq + i, h)),
                  pl.BlockSpec((seq, LANES), lambda b, h, i: (b, K0 // LANES + h)),
                  pl.BlockSpec((seq, LANES), lambda b, h, i: (b, V0 // LANES + h))],
        out_specs=pl.BlockSpec((tq, 2 * LANES), lambda b, h, i: (b * nq + i, h)),
        out_shape=jax.ShapeDtypeStruct((bsz * seq, ATTN_HEADS * ATTN_VD), F32),
        scratch_shapes=[pltpu.VMEM((4 * tq, 1), F32), pltpu.VMEM((4 * tq, 1), F32), pltpu.VMEM((4 * tq, LANES), F32)],
        compiler_params=_cparams(("arbitrary", "arbitrary", "arbitrary"), 40),
        name="attn_prompt",
    )(*lams, subln, u, u, u)


def _attn_sample_body(pt_ref, lq1, lk1, lq2, lk2, sub_ref, q_ref, kn_ref, vn_ref, *rest, n_pg, lam_init, tdec):
    del pt_ref
    k_refs, v_refs = rest[:n_pg], rest[n_pg:2 * n_pg]
    o_ref, m_sc, l_sc, acc_sc, qr_sc = rest[2 * n_pg:]
    j = pl.program_id(1)
    nrow = ATTN_KV_HEADS * 2 * 2 * tdec
    width = ATTN_KV_HEADS * LANES

    @pl.when(j == 0)
    def _():
        q = q_ref[...] * (ATTN_DH ** -0.5)
        lane = lax.broadcasted_iota(I32, (tdec, LANES), 1)
        qr_sc[...] = jnp.zeros_like(qr_sc)
        for kv in range(ATTN_KV_HEADS):
            for c in range(2):
                for g in range(2):
                    hd = kv * 2 + g
                    qh = q[:, hd * LANES:(hd + 1) * LANES]
                    piece = jnp.where((lane >= c * ATTN_DH) & (lane < (c + 1) * ATTN_DH), qh, 0.0)
                    n0 = ((kv * 2 + c) * 2 + g) * tdec
                    qr_sc[n0:n0 + tdec, kv * LANES:(kv + 1) * LANES] = piece
        m_sc[...] = jnp.full_like(m_sc, -jnp.inf)
        l_sc[...] = jnp.zeros_like(l_sc)
        acc_sc[...] = jnp.zeros_like(acc_sc)

    qr = qr_sc[...].astype(BF16)

    def update(s_list, v_list):
        m_old = m_sc[...]
        m_new = m_old
        for s in s_list:
            m_new = jnp.maximum(m_new, jnp.max(s, axis=1, keepdims=True))
        alpha = jnp.exp(m_old - m_new)
        l = alpha * l_sc[...]
        pv = None
        for s, v in zip(s_list, v_list):
            p = jnp.exp(s - m_new)
            l = l + jnp.sum(p, axis=1, keepdims=True)
            d = _mm(p.astype(BF16), v)
            pv = d if pv is None else pv + d
        m_sc[...] = m_new
        l_sc[...] = l
        acc_sc[...] = alpha * acc_sc[...] + pv

    update([_nt(qr, k[...].astype(BF16)) for k in k_refs], [v[...].astype(BF16) for v in v_refs])

    @pl.when(j == pl.num_programs(1) - 1)
    def _():
        lam = _lambda(lq1, lk1, lq2, lk2, lam_init)
        pad = jnp.zeros((LANES - tdec, width), F32)
        kn = jnp.concatenate([kn_ref[...], pad], axis=0).astype(BF16)
        vn = jnp.concatenate([vn_ref[...], pad], axis=0).astype(BF16)
        s = _nt(qr, kn)
        r = jnp.bitwise_and(lax.broadcasted_iota(I32, (nrow, LANES), 0), tdec - 1)
        c = lax.broadcasted_iota(I32, (nrow, LANES), 1)
        update([jnp.where(c <= r, s, -jnp.inf)], [vn])
        o = acc_sc[...] / l_sc[...]
        for kv in range(ATTN_KV_HEADS):
            for g in range(2):
                n0 = ((kv * 2 + 0) * 2 + g) * tdec
                n1 = ((kv * 2 + 1) * 2 + g) * tdec
                og = (o[n0:n0 + tdec, kv * LANES:(kv + 1) * LANES]
                      - lam * o[n1:n1 + tdec, kv * LANES:(kv + 1) * LANES])
                hd = kv * 2 + g
                o_ref[:, hd * LANES:(hd + 1) * LANES] = _rms(og, sub_ref[...]) * (1.0 - lam_init)


def _attn_sample(u, cache_k, cache_v, page_flat, layer, n_pool, lams, subln, bsz, tdec, n_pages, lam_init):
    n_pg = _tile(n_pages, (8, 4, 2, 1))
    nchunks = n_pages // n_pg
    page = cache_k.shape[1]
    width = ATTN_KV_HEADS * LANES
    nrow = ATTN_KV_HEADS * 2 * 2 * tdec
    assert nrow == LANES and tdec == SUBLANES
    vec = pl.BlockSpec((1, ATTN_DH), lambda b, j, pt: (0, 0))

    def page_spec(i):
        return pl.BlockSpec((None, page, width),
                            lambda b, j, pt: (layer * n_pool + pt[b * n_pages + j * n_pg + i], 0, 0))

    in_specs = [vec, vec, vec, vec,
                pl.BlockSpec((1, ATTN_VD), lambda b, j, pt: (0, 0)),
                pl.BlockSpec((tdec, ATTN_HEADS * ATTN_VD), lambda b, j, pt: (b, 0)),
                pl.BlockSpec((tdec, width), lambda b, j, pt: (b, K0 // width)),
                pl.BlockSpec((tdec, width), lambda b, j, pt: (b, V0 // width))]
    in_specs += [page_spec(i) for i in range(n_pg)] * 2
    return pl.pallas_call(
        functools.partial(_attn_sample_body, n_pg=n_pg, lam_init=lam_init, tdec=tdec),
        grid_spec=pltpu.PrefetchScalarGridSpec(
            num_scalar_prefetch=1,
            grid=(bsz, nchunks),
            in_specs=in_specs,
            out_specs=pl.BlockSpec((tdec, ATTN_HEADS * ATTN_VD), lambda b, j, pt: (b, 0)),
            scratch_shapes=[pltpu.VMEM((nrow, 1), F32), pltpu.VMEM((nrow, 1), F32),
                            pltpu.VMEM((nrow, width), F32), pltpu.VMEM((nrow, width), F32)]),
        out_shape=jax.ShapeDtypeStruct((bsz * tdec, ATTN_HEADS * ATTN_VD), F32),
        compiler_params=_cparams(("arbitrary", "arbitrary"), 40),
        name="attn_sample",
    )(page_flat, *lams, subln, u, u, u, *([cache_k] * n_pg), *([cache_v] * n_pg))


def _causal_conv(x, prev, cw_ref, cb_ref):
    taps = cw_ref.shape[0]
    rowi = lax.broadcasted_iota(I32, x.shape, 0)
    acc = cb_ref[...] + x * cw_ref[taps - 1:taps, :]
    for s in range(1, taps):
        sh = jnp.where(rowi < s, pltpu.roll(prev, s, 0), pltpu.roll(x, s, 0))
        acc = acc + sh * cw_ref[taps - 1 - s:taps - s, :]
    return acc


def _pad_rows(x, rows):
    if x.shape[0] == rows:
        return x
    return jnp.concatenate([x, jnp.zeros((rows - x.shape[0], x.shape[1]), x.dtype)], axis=0)


def _ssd_body(xbc_ref, z_ref, dt_ref, st_ref, h0_ref, cw_ref, cb_ref, dtb_ref, alog_ref, dvec_ref, nw_ref,
              y_ref, hout_ref, prev_sc, h_sc, *, rows):
    c = pl.program_id(1)
    L = SSD_CHUNK
    width = SSD_HEADS * SSD_HEAD_DIM
    gw = SSD_GROUPS * SSD_STATE

    @pl.when(c == 0)
    def _():
        prev_sc[...] = jnp.concatenate([jnp.zeros((L - SUBLANES, prev_sc.shape[1]), F32), st_ref[...]], axis=0)
        h_sc[...] = h0_ref[...]

    x = _pad_rows(xbc_ref[...], L)
    conv = _causal_conv(x, prev_sc[...], cw_ref, cb_ref)
    prev_sc[...] = x
    xc = conv * jax.nn.sigmoid(conv)
    xs, bm, cm = xc[:, :width], xc[:, width:width + gw], xc[:, width + gw:]

    dt = jax.nn.softplus(_pad_rows(dt_ref[...], L) + dtb_ref[...])
    if rows < L:
        dt = jnp.where(lax.broadcasted_iota(I32, dt.shape, 0) < rows, dt, 0.0)
    da = dt * (-jnp.exp(alog_ref[...]))
    r0 = lax.broadcasted_iota(I32, (L, L), 0)
    c0 = lax.broadcasted_iota(I32, (L, L), 1)
    causal = r0 >= c0
    acum = jnp.dot(causal.astype(F32), da, precision=lax.Precision.HIGHEST, preferred_element_type=F32)
    acum_t, dt_t = acum.T, dt.T
    last = acum[L - 1:L, :]
    wend = jnp.exp(last - acum) * dt
    eac = jnp.exp(acum)
    elast = jnp.exp(last)
    lane = lax.broadcasted_iota(I32, (L, LANES), 1)
    first = lane < SSD_HEAD_DIM
    top = lax.broadcasted_iota(I32, (2 * SSD_HEAD_DIM, SSD_STATE), 0) < SSD_HEAD_DIM

    ys = []
    for pr in range(SSD_HEADS // 2):
        g = (2 * pr) // (SSD_HEADS // SSD_GROUPS)
        bg = bm[:, g * SSD_STATE:(g + 1) * SSD_STATE].astype(BF16)
        cg = cm[:, g * SSD_STATE:(g + 1) * SSD_STATE].astype(BF16)
        gmat = _nt(cg, bg)
        xp = xs[:, pr * LANES:(pr + 1) * LANES]
        xpb = xp.astype(BF16)
        hp = h_sc[pr * LANES:(pr + 1) * LANES, :]
        outs = []
        for hh in (2 * pr, 2 * pr + 1):
            seg = acum[:, hh:hh + 1] - acum_t[hh:hh + 1, :]
            dec = jnp.exp(jnp.where(causal, seg, -jnp.inf))
            w = gmat * dec * dt_t[hh:hh + 1, :]
            outs.append(_mm(w.astype(BF16), xpb))
        h_a, h_b = 2 * pr, 2 * pr + 1
        y_intra = jnp.where(first, outs[0], outs[1])
        e_pair = jnp.where(first, eac[:, h_a:h_a + 1], eac[:, h_b:h_b + 1])
        ys.append(y_intra + _nt(cg, hp.astype(BF16)) * e_pair)
        w_pair = jnp.where(first, wend[:, h_a:h_a + 1], wend[:, h_b:h_b + 1])
        upd = _mm((xp * w_pair).T.astype(BF16), bg)
        keep = jnp.where(top, elast[:, h_a:h_a + 1], elast[:, h_b:h_b + 1])
        h_sc[pr * LANES:(pr + 1) * LANES, :] = hp * keep + upd

    y = jnp.concatenate(ys, axis=1) + dvec_ref[...] * xs
    zz = _pad_rows(z_ref[...], L)
    gated = y * (zz * jax.nn.sigmoid(zz))
    y_ref[...] = _rms(gated, nw_ref[...])[:rows]

    @pl.when(c == pl.num_programs(1) - 1)
    def _():
        hout_ref[...] = h_sc[...]


def _ssd(u, st8, h0, cw, cb, dtb, alog, dvec, nw, bsz, seq):
    rows = SSD_CHUNK if seq % SSD_CHUNK == 0 else seq
    assert rows == SSD_CHUNK or (rows == seq and rows % SUBLANES == 0 and rows <= SSD_CHUNK)
    nch = seq // rows
    cc = cw.shape[1]
    width = SSD_HEADS * SSD_HEAD_DIM
    hp = SSD_HEADS * SSD_HEAD_DIM
    const = lambda shape: pl.BlockSpec(shape, lambda b, c: (0,) * len(shape))
    return pl.pallas_call(
        functools.partial(_ssd_body, rows=rows),
        grid=(bsz, nch),
        in_specs=[pl.BlockSpec((rows, cc), lambda b, c: (b * nch + c, XBC0 // cc)),
                  pl.BlockSpec((rows, width), lambda b, c: (b * nch + c, Z0 // width)),
                  pl.BlockSpec((rows, LANES), lambda b, c: (b * nch + c, DT0 // LANES)),
                  pl.BlockSpec((None, SUBLANES, cc), lambda b, c: (b, 0, 0)),
                  pl.BlockSpec((None, hp, SSD_STATE), lambda b, c: (b, 0, 0)),
                  const((SSD_CONV, cc)), const((1, cc)), const((1, LANES)), const((1, LANES)),
                  const((1, width)), const((1, width))],
        out_specs=[pl.BlockSpec((rows, width), lambda b, c: (b * nch + c, 0)),
                   pl.BlockSpec((None, hp, SSD_STATE), lambda b, c: (b, 0, 0))],
        out_shape=[jax.ShapeDtypeStruct((bsz * seq, width), F32),
                   jax.ShapeDtypeStruct((bsz, hp, SSD_STATE), F32)],
        scratch_shapes=[pltpu.VMEM((SSD_CHUNK, cc), F32), pltpu.VMEM((hp, SSD_STATE), F32)],
        compiler_params=_cparams(("arbitrary", "arbitrary"), 32),
        name="ssd",
    )(u, u, u, st8, h0, cw, cb, dtb, alog, dvec, nw)


def _expm1(t):
    u = jnp.exp(t)
    small = (u - 1.0) * t / jnp.log(u)
    return jnp.where(t < -1.0, u - 1.0, jnp.where(u == 1.0, t, small))


def _lru_body(xr_ref, gate_ref, st_ref, h0_ref, cw_ref, cb_ref, wa_ref, ba_ref, wx_ref, bx_ref, lam_ref,
              y_ref, hout_ref, prev_sc, h_sc, *, rows):
    c = pl.program_id(1)

    @pl.when(c == 0)
    def _():
        st = st_ref[...]
        if rows > SUBLANES:
            st = jnp.concatenate([jnp.zeros((rows - SUBLANES, st.shape[1]), F32), st], axis=0)
        prev_sc[...] = st
        h_sc[...] = h0_ref[...]

    x = xr_ref[...]
    xc = _causal_conv(x, prev_sc[...], cw_ref, cb_ref)
    prev_sc[...] = x
    xcb = xc.astype(BF16)
    r = jax.nn.sigmoid(_mm(xcb, wa_ref[...]) + ba_ref[...])
    ig = jax.nn.sigmoid(_mm(xcb, wx_ref[...]) + bx_ref[...])
    log_a = (-LRU_C) * r * jax.nn.softplus(-lam_ref[...])
    a = jnp.exp(log_a)
    b = jnp.sqrt(-_expm1(2.0 * log_a)) * (ig * xc)
    rowi = lax.broadcasted_iota(I32, a.shape, 0)
    d = 1
    while d < rows:
        ok = rowi >= d
        b = jnp.where(ok, a * pltpu.roll(b, d, 0) + b, b)
        a = jnp.where(ok, a * pltpu.roll(a, d, 0), a)
        d *= 2
    h = b + a * h_sc[...]
    h_sc[...] = h[rows - 1:rows, :]
    y_ref[...] = h * jax.nn.gelu(gate_ref[...])

    @pl.when(c == pl.num_programs(1) - 1)
    def _():
        hout_ref[...] = h[rows - 1:rows, :]


def _lru(u, st8, h0, cw, cb, wa, ba, wx, bx, lam, bsz, seq):
    rows = _tile(seq, (128, 64, 32, 16, 8))
    nch = seq // rows
    w = cw.shape[1]
    const = lambda shape: pl.BlockSpec(shape, lambda b, c: (0,) * len(shape))
    return pl.pallas_call(
        functools.partial(_lru_body, rows=rows),
        grid=(bsz, nch),
        in_specs=[pl.BlockSpec((rows, w), lambda b, c: (b * nch + c, XR0 // w)),
                  pl.BlockSpec((rows, w), lambda b, c: (b * nch + c, GATE0 // w)),
                  pl.BlockSpec((None, SUBLANES, w), lambda b, c: (b, 0, 0)),
                  pl.BlockSpec((None, 1, w), lambda b, c: (b, 0, 0)),
                  const((LRU_CONV, w)), const((1, w)), const((w, w)), const((1, w)), const((w, w)), const((1, w)),
                  const((1, w))],
        out_specs=[pl.BlockSpec((rows, w), lambda b, c: (b * nch + c, 0)),
                   pl.BlockSpec((None, 1, w), lambda b, c: (b, 0, 0))],
        out_shape=[jax.ShapeDtypeStruct((bsz * seq, w), F32), jax.ShapeDtypeStruct((bsz, 1, w), F32)],
        scratch_shapes=[pltpu.VMEM((rows, w), F32), pltpu.VMEM((1, w), F32)],
        compiler_params=_cparams(("arbitrary", "arbitrary"), 32),
        name="lru",
    )(u, u, st8, h0, cw, cb, wa, ba, wx, bx, lam)


def _out_proj_body(x_ref, a_ref, s_ref, l_ref, wo_ref, nw_ref, wq_ref, x1_ref, qm_ref):
    wa, ws = a_ref.shape[1], s_ref.shape[1]
    acc = x_ref[...] + _mm(a_ref[...].astype(BF16), wo_ref[0:wa, :])
    acc = acc + _mm(s_ref[...].astype(BF16), wo_ref[wa:wa + ws, :])
    acc = acc + _mm(l_ref[...].astype(BF16), wo_ref[wa + ws:, :])
    x1_ref[...] = acc
    qm_ref[...] = _mm(_rms(acc, nw_ref[...]).astype(BF16), wq_ref[...])


def _out_proj(x, attn, ssd, lru, wo, nw, wq):
    t, d = x.shape
    tm = _tile(t, (256, 128))
    row = lambda w: pl.BlockSpec((tm, w), lambda i: (i, 0))
    const = lambda shape: pl.BlockSpec(shape, lambda i: (0, 0))
    return pl.pallas_call(
        _out_proj_body,
        grid=(t // tm,),
        in_specs=[row(d), row(attn.shape[1]), row(ssd.shape[1]), row(lru.shape[1]),
                  const(wo.shape), const((1, d)), const(wq.shape)],
        out_specs=[row(d), row(wq.shape[1])],
        out_shape=[jax.ShapeDtypeStruct((t, d), F32), jax.ShapeDtypeStruct((t, wq.shape[1]), F32)],
        compiler_params=_cparams(("arbitrary",), 48),
        name="out_proj",
    )(x, attn, ssd, lru, wo, nw, wq)


def _mem_kv_body(m_ref, wk_ref, wv_ref, k_ref, v_ref):
    mb = m_ref[...].astype(BF16)
    k_ref[...] = _mm(mb, wk_ref[...])
    v_ref[...] = _mm(mb, wv_ref[...])


def _mem_kv(mem, wk, wv):
    t, d = mem.shape
    tm = _tile(t, (256, 128))
    w = wk.shape[1]
    return pl.pallas_call(
        _mem_kv_body,
        grid=(t // tm,),
        in_specs=[pl.BlockSpec((tm, d), lambda i: (i, 0)),
                  pl.BlockSpec((d, w), lambda i: (0, 0)), pl.BlockSpec((d, w), lambda i: (0, 0))],
        out_specs=[pl.BlockSpec((tm, w), lambda i: (i, 0)), pl.BlockSpec((tm, w), lambda i: (i, 0))],
        out_shape=[jax.ShapeDtypeStruct((t, w), F32), jax.ShapeDtypeStruct((t, w), F32)],
        compiler_params=_cparams(("arbitrary",), 32),
        name="mem_kv",
    )(mem, wk, wv)


def _xattn_body(q_ref, k_ref, v_ref, o_ref):
    for h in range(MEM_HEADS):
        sl = slice(h * MEM_DH, (h + 1) * MEM_DH)
        s = _nt(q_ref[:, sl].astype(BF16), k_ref[:, sl].astype(BF16)) * (MEM_DH ** -0.5)
        e = jnp.exp(s - jnp.max(s, axis=1, keepdims=True))
        o = _mm(e.astype(BF16), v_ref[:, sl].astype(BF16))
        o_ref[:, sl] = o / jnp.sum(e, axis=1, keepdims=True)


def _xattn(qm, mem_k, mem_v, bsz, seq):
    tq = _tile(seq, (256, 128))
    nq = seq // tq
    w = qm.shape[1]
    m = mem_k.shape[1]
    return pl.pallas_call(
        _xattn_body,
        grid=(bsz, nq),
        in_specs=[pl.BlockSpec((tq, w), lambda b, i: (b * nq + i, 0)),
                  pl.BlockSpec((None, m, w), lambda b, i: (b, 0, 0)),
                  pl.BlockSpec((None, m, w), lambda b, i: (b, 0, 0))],
        out_specs=pl.BlockSpec((tq, w), lambda b, i: (b * nq + i, 0)),
        out_shape=jax.ShapeDtypeStruct(qm.shape, F32),
        compiler_params=_cparams(("arbitrary", "arbitrary"), 32),
        name="xattn",
    )(qm, mem_k, mem_v)


def _route_body(x1_ref, om_ref, wo_ref, nw_ref, rw_ref, rb_ref, x2_ref, hf_ref, eid_ref, gate_ref):
    x2 = x1_ref[...] + _mm(om_ref[...].astype(BF16), wo_ref[...])
    x2_ref[...] = x2
    hf = _rms(x2, nw_ref[...])
    hf_ref[...] = hf
    logits = jnp.dot(hf, rw_ref[...], precision=lax.Precision.HIGHEST, preferred_element_type=F32) + rb_ref[...]
    lane = lax.broadcasted_iota(I32, logits.shape, 1).astype(F32)
    ninf = -jnp.inf
    big = float(LANES)

    def first_argmax(v, mx):
        return jnp.min(jnp.where(v == mx, lane, big), axis=1, keepdims=True)

    lg = jnp.where(lane < MOE_GROUPS, logits, ninf)
    mg = jnp.max(lg, axis=1, keepdims=True)
    p_grp = 1.0 / jnp.sum(jnp.exp(lg - mg), axis=1, keepdims=True)
    lo = MOE_GROUPS + MOE_EXPERTS_PER_GROUP * first_argmax(lg, mg)
    le = jnp.where((lane >= lo) & (lane < lo + MOE_EXPERTS_PER_GROUP), logits, ninf)
    v1 = jnp.max(le, axis=1, keepdims=True)
    i1 = first_argmax(le, v1)
    le2 = jnp.where(lane == i1, ninf, le)
    v2 = jnp.max(le2, axis=1, keepdims=True)
    i2 = first_argmax(le2, v2)
    e2 = jnp.exp(v2 - v1)
    den = 1.0 + e2
    g1 = p_grp * (1.0 / den)
    g2 = p_grp * (e2 / den)
    gate_ref[...] = jnp.where(lane == 0.0, g1, jnp.where(lane == 1.0, g2, 0.0))
    eid_ref[...] = jnp.where(lane == 0.0, i1 - MOE_GROUPS, jnp.where(lane == 1.0, i2 - MOE_GROUPS, 0.0)).astype(I32)


def _route(x1, om, wo, nw, rw, rb):
    t, d = x1.shape
    tm = _tile(t, (256, 128))
    row = lambda w: pl.BlockSpec((tm, w), lambda i: (i, 0))
    const = lambda shape: pl.BlockSpec(shape, lambda i: (0, 0))
    return pl.pallas_call(
        _route_body,
        grid=(t // tm,),
        in_specs=[row(d), row(om.shape[1]), const(wo.shape), const((1, d)), const(rw.shape), const((1, LANES))],
        out_specs=[row(d), row(d), row(LANES), row(LANES)],
        out_shape=[jax.ShapeDtypeStruct((t, d), F32), jax.ShapeDtypeStruct((t, d), F32),
                   jax.ShapeDtypeStruct((t, LANES), I32), jax.ShapeDtypeStruct((t, LANES), F32)],
        compiler_params=_cparams(("arbitrary",), 40),
        name="route",
    )(x1, om, wo, nw, rw, rb)


def _moe_body(order_ref, starts_ref, hf_hbm, wg_ref, wu_ref, wd_ref, o_hbm,
              xbuf, ybuf, wgb, wub, wdb, gsem, ssem, *, rb):
    e = pl.program_id(0)
    s0 = starts_ref[e]
    n = starts_ref[e + 1] - s0

    @pl.when(e == 0)
    def _():
        xbuf[...] = jnp.zeros_like(xbuf)

    @pl.when(n > 0)
    def _():
        wgb[...] = wg_ref[...].astype(BF16)
        wub[...] = wu_ref[...].astype(BF16)
        wdb[...] = wd_ref[...].astype(BF16)

    def gather_copy(i, tok):
        return pltpu.make_async_copy(hf_hbm.at[pl.ds(tok, 1), :], xbuf.at[pl.ds(i, 1), :], gsem)

    def scatter_copy(i, dst):
        return pltpu.make_async_copy(ybuf.at[pl.ds(i, 1), :], o_hbm.at[pl.ds(dst, 1), :], ssem)

    def block(r, carry):
        base = s0 + r * rb
        cnt = jnp.minimum(rb, n - r * rb)

        def g_start(i, c):
            gather_copy(i, lax.div(order_ref[base + i], MOE_TOPK)).start()
            return c

        def g_wait(i, c):
            gather_copy(i, 0).wait()
            return c

        lax.fori_loop(0, cnt, g_start, 0)
        lax.fori_loop(0, cnt, g_wait, 0)
        xb = xbuf[...].astype(BF16)
        hg = _mm(xb, wgb[...])
        act = (hg * jax.nn.sigmoid(hg)) * _mm(xb, wub[...])
        ybuf[...] = _mm(act.astype(BF16), wdb[...])

        def s_start(i, c):
            scatter_copy(i, order_ref[base + i]).start()
            return c

        def s_wait(i, c):
            scatter_copy(i, 0).wait()
            return c

        lax.fori_loop(0, cnt, s_start, 0)
        lax.fori_loop(0, cnt, s_wait, 0)
        return carry

    lax.fori_loop(0, (n + rb - 1) // rb, block, 0)


def _moe_ffn(hf, order, starts, wg, wu, wd, layer):
    t, d = hf.shape
    n_asg = order.shape[0]
    ff = wg.shape[-1]
    rb = 256 if n_asg >= 256 * MOE_EXPERTS else 128
    return pl.pallas_call(
        functools.partial(_moe_body, rb=rb),
        grid_spec=pltpu.PrefetchScalarGridSpec(
            num_scalar_prefetch=2,
            grid=(MOE_EXPERTS,),
            in_specs=[pl.BlockSpec(memory_space=pl.ANY),
                      pl.BlockSpec((None, None, d, ff), lambda e, o, s: (layer, e, 0, 0)),
                      pl.BlockSpec((None, None, d, ff), lambda e, o, s: (layer, e, 0, 0)),
                      pl.BlockSpec((None, None, ff, d), lambda e, o, s: (layer, e, 0, 0))],
            out_specs=pl.BlockSpec(memory_space=pl.ANY),
            scratch_shapes=[pltpu.VMEM((rb, d), F32), pltpu.VMEM((rb, d), F32),
                            pltpu.VMEM((d, ff), BF16), pltpu.VMEM((d, ff), BF16), pltpu.VMEM((ff, d), BF16),
                            pltpu.SemaphoreType.DMA(()), pltpu.SemaphoreType.DMA(())]),
        out_shape=jax.ShapeDtypeStruct((n_asg, d), F32),
        compiler_params=_cparams(("arbitrary",), 56),
        name="moe_ffn",
    )(order, starts, hf, wg, wu, wd)


def _combine_body(x2_ref, o_ref, g_ref, fw_ref, x3_ref, *, final):
    d = x2_ref.shape[1]
    g = g_ref[...]
    x3 = x2_ref[...] + (g[:, 0:1] * o_ref[:, :d] + g[:, 1:2] * o_ref[:, d:])
    x3_ref[...] = _rms(x3, fw_ref[...]) if final else x3


def _combine(x2, o2, gates, fw, final):
    t, d = x2.shape
    tm = _tile(t, (256, 128))
    return pl.pallas_call(
        functools.partial(_combine_body, final=final),
        grid=(t // tm,),
        in_specs=[pl.BlockSpec((tm, d), lambda i: (i, 0)),
                  pl.BlockSpec((tm, MOE_TOPK * d), lambda i: (i, 0)),
                  pl.BlockSpec((tm, LANES), lambda i: (i, 0)),
                  pl.BlockSpec((1, d), lambda i: (0, 0))],
        out_specs=pl.BlockSpec((tm, d), lambda i: (i, 0)),
        out_shape=jax.ShapeDtypeStruct((t, d), F32),
        compiler_params=_cparams(("arbitrary",), 32),
        name="combine",
    )(x2, o2, gates, fw)


def _rope_tables(pos):
    half = ATTN_DH // 2
    inv = ROPE_THETA ** (-jnp.arange(half, dtype=F32) / half)
    ang = pos.astype(F32)[:, None] * inv[None, :]
    cos, sin = jnp.cos(ang), jnp.sin(ang)
    reps = LANES // ATTN_DH
    return jnp.tile(jnp.concatenate([cos, cos], axis=1), (1, reps)), jnp.tile(jnp.concatenate([-sin, sin], axis=1), (1, reps))


def _block_diag(w):
    g, a, b = w.shape
    eye = jnp.eye(g, dtype=w.dtype)
    return (w[:, :, None, :] * eye[:, None, :, None]).reshape(g * a, g * b)


def _pad_lanes(v, n):
    return jnp.pad(v.reshape(1, -1), ((0, 0), (0, n - v.size)))


def _layer(x, grp, lw, layer, lam_init, final, final_norm):
    bsz, seq = grp["bsz"], grp["seq"]
    u = _in_proj(x, lw["norm_mix"], lw["w_in"], grp["cos"], grp["sin"])
    lams = lw["lams"]
    if grp["paged"]:
        attn = _attn_sample(u, grp["cache_k"], grp["cache_v"], grp["page_flat"], layer, grp["n_pool"], lams,
                            lw["subln"], bsz, seq, grp["n_pages"], lam_init)
    else:
        attn = _attn_prompt(u, lams, lw["subln"], bsz, seq, lam_init)
    ssd, h_ssd = _ssd(u, grp["ssd_conv"][layer], grp["ssd_state"][layer], lw["ssd_conv_w"], lw["ssd_conv_b"],
                      lw["ssd_dt_bias"], lw["ssd_a_log"], lw["ssd_d"], lw["ssd_norm"], bsz, seq)
    lru, h_lru = _lru(u, grp["lru_conv"][layer], grp["lru_state"][layer], lw["lru_conv_w"], lw["lru_conv_b"],
                      lw["lru_wa"], lw["lru_ba"], lw["lru_wx"], lw["lru_bx"], lw["lru_lambda"], bsz, seq)
    x1, qm = _out_proj(x, attn, ssd, lru, lw["w_out"], lw["norm_mem"], lw["wq_mem"])
    om = _xattn(qm, grp["mem_k"][layer], grp["mem_v"][layer], bsz, seq)
    x2, hf, eid, gate = _route(x1, om, lw["wo_mem"], lw["norm_ffn"], lw["router_w"], lw["router_b"])
    flat_e = eid[:, :MOE_TOPK].reshape(-1)
    order = jnp.argsort(flat_e, stable=True).astype(I32)
    counts = jnp.sum((flat_e[:, None] == jnp.arange(MOE_EXPERTS, dtype=I32)[None, :]).astype(I32), axis=0)
    starts = jnp.concatenate([jnp.zeros((1,), I32), jnp.cumsum(counts).astype(I32)])
    o2 = _moe_ffn(hf, order, starts, lw["moe_w_gate"], lw["moe_w_up"], lw["moe_w_down"], layer)
    x3 = _combine(x2, o2.reshape(x.shape[0], MOE_TOPK * x.shape[1]), gate, final_norm, final)
    ur = u.reshape(bsz, seq, UW)
    width_kv = ATTN_KV_HEADS * ATTN_VD
    outs = dict(
        k=ur[:, :, K0:K0 + width_kv].reshape(bsz, seq, ATTN_KV_HEADS, ATTN_VD),
        v=ur[:, :, V0:V0 + width_kv].reshape(bsz, seq, ATTN_KV_HEADS, ATTN_VD),
        ssd_conv=ur[:, seq - (SSD_CONV - 1):, XBC0:Z0],
        ssd_state=h_ssd.reshape(bsz, SSD_HEADS, SSD_HEAD_DIM, SSD_STATE),
        lru_conv=ur[:, seq - (LRU_CONV - 1):, XR0:GATE0],
        lru_state=h_lru.reshape(bsz, -1),
    )
    return x3, outs


def _conv_state8(st):
    return jnp.pad(st, ((0, 0), (0, 0), (SUBLANES - st.shape[2], 0), (0, 0)))


def kernel(x_prompt, x_sample, cache_k, cache_v, cache_mem_k, cache_mem_v, state_ssd_conv, state_ssd, state_lru_conv, state_lru, page_table, mem_prompt, norm_mix, w_in, attn_lambda_q1, attn_lambda_k1, attn_lambda_q2, attn_lambda_k2, attn_subln, ssd_conv_w, ssd_conv_b, ssd_dt_bias, ssd_a_log, ssd_d, ssd_norm, lru_conv_w, lru_conv_b, lru_wa, lru_ba, lru_wx, lru_bx, lru_lambda, w_out, norm_mem, wq_mem, wk_mem, wv_mem, wo_mem, norm_ffn, router_group_w, router_group_b, router_expert_w, router_expert_b, moe_w_gate, moe_w_up, moe_w_down, final_norm):
    depth = w_in.shape[0]
    bp, tp, d = x_prompt.shape
    bs, ts, _ = x_sample.shape
    n_pool, page = cache_k.shape[1], cache_k.shape[2]
    n_pages = page_table.shape[1]
    past_len = n_pages * page
    n_mem = mem_prompt.shape[1]
    ssd_cc = state_ssd_conv.shape[-1]
    lru_w = state_lru.shape[-1]
    kvw = ATTN_KV_HEADS * ATTN_VD
    memw = MEM_HEADS * MEM_DH

    cos_p, sin_p = _rope_tables(jnp.tile(jnp.arange(tp, dtype=I32), bp))
    cos_s, sin_s = _rope_tables(jnp.tile(past_len + jnp.arange(ts, dtype=I32), bs))

    prompt = dict(bsz=bp, seq=tp, paged=False, cos=cos_p, sin=sin_p,
                  ssd_conv=jnp.zeros((depth, bp, SUBLANES, ssd_cc), F32),
                  ssd_state=jnp.zeros((depth, bp, SSD_HEADS * SSD_HEAD_DIM, SSD_STATE), F32),
                  lru_conv=jnp.zeros((depth, bp, SUBLANES, lru_w), F32),
                  lru_state=jnp.zeros((depth, bp, 1, lru_w), F32))
    sample = dict(bsz=bs, seq=ts, paged=True, cos=cos_s, sin=sin_s,
                  cache_k=cache_k.reshape(depth * n_pool, page, kvw), cache_v=cache_v.reshape(depth * n_pool, page, kvw),
                  page_flat=page_table.reshape(-1), n_pool=n_pool, n_pages=n_pages,
                  ssd_conv=_conv_state8(state_ssd_conv),
                  ssd_state=state_ssd.reshape(depth, bs, SSD_HEADS * SSD_HEAD_DIM, SSD_STATE),
                  lru_conv=_conv_state8(state_lru_conv),
                  lru_state=state_lru.reshape(depth, bs, 1, lru_w),
                  mem_k=cache_mem_k.reshape(depth, bs, n_mem, memw), mem_v=cache_mem_v.reshape(depth, bs, n_mem, memw))

    xp = x_prompt.reshape(bp * tp, d)
    xs = x_sample.reshape(bs * ts, d)
    po, so, mks, mvs = [], [], [], []
    for l in range(depth):
        w = w_in[l]
        zc = V0 + kvw
        xc0 = zc + SSD_HEADS * SSD_HEAD_DIM
        dc = xc0 + ssd_cc
        rc = dc + SSD_HEADS
        w_pad = jnp.concatenate([w[:, :zc], w[:, xc0:dc], w[:, zc:xc0], w[:, rc:rc + 2 * lru_w], w[:, dc:rc],
                                 jnp.zeros((d, UW - DT0 - SSD_HEADS), F32)], axis=1).astype(BF16)
        router_w = jnp.pad(jnp.concatenate([router_group_w[l], router_expert_w[l]], axis=1),
                           ((0, 0), (0, LANES - MOE_GROUPS - MOE_EXPERTS)))
        router_b = _pad_lanes(jnp.concatenate([router_group_b[l], router_expert_b[l]]), LANES)
        lw = dict(
            norm_mix=norm_mix[l].reshape(1, d), w_in=w_pad,
            lams=[v[l].reshape(1, ATTN_DH) for v in (attn_lambda_q1, attn_lambda_k1, attn_lambda_q2, attn_lambda_k2)],
            subln=attn_subln[l].reshape(1, ATTN_VD),
            ssd_conv_w=ssd_conv_w[l], ssd_conv_b=ssd_conv_b[l].reshape(1, -1),
            ssd_dt_bias=_pad_lanes(ssd_dt_bias[l], LANES), ssd_a_log=_pad_lanes(ssd_a_log[l], LANES),
            ssd_d=jnp.repeat(ssd_d[l], SSD_HEAD_DIM).reshape(1, -1), ssd_norm=ssd_norm[l].reshape(1, -1),
            lru_conv_w=lru_conv_w[l], lru_conv_b=lru_conv_b[l].reshape(1, -1),
            lru_wa=_block_diag(lru_wa[l]).astype(BF16), lru_ba=lru_ba[l].reshape(1, -1),
            lru_wx=_block_diag(lru_wx[l]).astype(BF16), lru_bx=lru_bx[l].reshape(1, -1),
            lru_lambda=lru_lambda[l].reshape(1, -1),
            w_out=w_out[l].astype(BF16), norm_mem=norm_mem[l].reshape(1, d), wq_mem=wq_mem[l].astype(BF16),
            wo_mem=wo_mem[l].astype(BF16), norm_ffn=norm_ffn[l].reshape(1, d),
            router_w=router_w, router_b=router_b,
            moe_w_gate=moe_w_gate, moe_w_up=moe_w_up, moe_w_down=moe_w_down,
        )
        lam_init = 0.8 - 0.6 * math.exp(-0.3 * l)
        mk, mv = _mem_kv(mem_prompt.reshape(bp * n_mem, d), wk_mem[l].astype(BF16), wv_mem[l].astype(BF16))
        mks.append(mk.reshape(bp, n_mem, MEM_HEADS, MEM_DH))
        mvs.append(mv.reshape(bp, n_mem, MEM_HEADS, MEM_DH))
        pg = dict(prompt, mem_k=[None] * l + [mk.reshape(bp, n_mem, memw)], mem_v=[None] * l + [mv.reshape(bp, n_mem, memw)])
        final = l == depth - 1
        xp, o = _layer(xp, pg, lw, l, lam_init, final, final_norm.reshape(1, d))
        po.append(o)
        xs, o = _layer(xs, sample, lw, l, lam_init, final, final_norm.reshape(1, d))
        so.append(o)

    st = lambda outs, key: jnp.stack([o[key] for o in outs])
    return (xp.reshape(bp, tp, d), xs.reshape(bs, ts, d),
            st(po, "k"), st(po, "v"), jnp.stack(mks), jnp.stack(mvs),
            st(po, "ssd_conv"), st(po, "ssd_state"), st(po, "lru_conv"), st(po, "lru_state"),
            st(so, "k"), st(so, "v"), st(so, "ssd_conv"), st(so, "ssd_state"), st(so, "lru_conv"), st(so, "lru_state"))
```

```python
import functools
import math

import jax
import jax.numpy as jnp
from jax import lax
from jax.experimental import pallas as pl
from jax.experimental.pallas import tpu as pltpu

F32, BF16, I32 = jnp.float32, jnp.bfloat16, jnp.int32
EPS = 1e-6
LANES = 128
SUBLANES = 8
MIB = 1024 * 1024

ATTN_HEADS, ATTN_KV_HEADS, ATTN_DH = 8, 4, 64
ATTN_VD = 2 * ATTN_DH
ROPE_THETA = 10000.0
SSD_HEADS, SSD_HEAD_DIM, SSD_GROUPS, SSD_STATE, SSD_CONV = 8, 64, 2, 128, 4
SSD_CHUNK = 128
LRU_BLOCKS, LRU_CONV, LRU_C = 8, 4, 8.0
MEM_HEADS, MEM_DH = 4, 128
MOE_GROUPS, MOE_EXPERTS_PER_GROUP, MOE_TOPK = 4, 8, 2
MOE_EXPERTS = MOE_GROUPS * MOE_EXPERTS_PER_GROUP

Q0, K0, V0, XBC0, Z0, XR0, GATE0, DT0, UW = 0, 1024, 1536, 2048, 3072, 3584, 4096, 4608, 5120
PROJ_TN = 1024
PAGE_GROUP = 4


def _cparams(sem, vmem_mib):
    return pltpu.CompilerParams(dimension_semantics=sem, vmem_limit_bytes=vmem_mib * MIB)


def _tile(n, prefs):
    for p in prefs:
        if n % p == 0:
            return p
    return n


def _rms(x, w):
    return (x * lax.rsqrt(jnp.mean(x * x, axis=-1, keepdims=True) + EPS)) * w


def _nt(a, b):
    return lax.dot_general(a, b, (((1,), (1,)), ((), ())), preferred_element_type=F32)


def _mm(a, b):
    return jnp.dot(a, b, preferred_element_type=F32)


def _rope128(yc, cos, sin):
    lane = lax.broadcasted_iota(I32, yc.shape, 1)
    sw = jnp.where((lane % 64) < 32, pltpu.roll(yc, 96, 1), pltpu.roll(yc, 32, 1))
    return yc * cos + sw * sin


def _in_proj_body(x_ref, nw_ref, w_ref, cos_ref, sin_ref, o_ref):
    j = pl.program_id(0)
    h = _rms(x_ref[...], nw_ref[...])
    y = _mm(h.astype(BF16), w_ref[...])
    n_chunks = y.shape[1] // LANES

    def store_rope(n_rope):
        cos, sin = cos_ref[...], sin_ref[...]
        for c in range(n_rope):
            o_ref[:, c * LANES:(c + 1) * LANES] = _rope128(y[:, c * LANES:(c + 1) * LANES], cos, sin)
        if n_rope < n_chunks:
            o_ref[:, n_rope * LANES:] = y[:, n_rope * LANES:]

    @pl.when(j == 0)
    def _():
        store_rope(n_chunks)

    @pl.when(j == 1)
    def _():
        store_rope((V0 - K0) // LANES)

    @pl.when(j >= 2)
    def _():
        o_ref[...] = y


def _in_proj(x, nw, w_pad, cos, sin):
    t, d = x.shape
    tm = _tile(t, (512, 256, 128))
    return pl.pallas_call(
        _in_proj_body,
        grid=(UW // PROJ_TN, t // tm),
        in_specs=[pl.BlockSpec((tm, d), lambda j, i: (i, 0)),
                  pl.BlockSpec((1, d), lambda j, i: (0, 0)),
                  pl.BlockSpec((d, PROJ_TN), lambda j, i: (0, j)),
                  pl.BlockSpec((tm, LANES), lambda j, i: (i, 0)),
                  pl.BlockSpec((tm, LANES), lambda j, i: (i, 0))],
        out_specs=pl.BlockSpec((tm, PROJ_TN), lambda j, i: (i, j)),
        out_shape=jax.ShapeDtypeStruct((t, UW), F32),
        compiler_params=_cparams(("arbitrary", "arbitrary"), 48),
        name="in_proj",
    )(x, nw, w_pad, cos, sin)


def _lambda(lq1, lk1, lq2, lk2, lam_init):
    s1 = jnp.sum(lq1[...] * lk1[...], axis=-1, keepdims=True)
    s2 = jnp.sum(lq2[...] * lk2[...], axis=-1, keepdims=True)
    return jnp.exp(s1) - jnp.exp(s2) + lam_init


def _attn_prompt_body(lq1, lk1, lq2, lk2, sub_ref, q_ref, k_ref, v_ref, o_ref, m_sc, l_sc, acc_sc, *, tq, lam_init):
    i = pl.program_id(2)
    lam = _lambda(lq1, lk1, lq2, lk2, lam_init)
    q = q_ref[...] * (ATTN_DH ** -0.5)
    lane = lax.broadcasted_iota(I32, (tq, LANES), 1)
    parts = []
    for g in range(2):
        qh = q[:, g * LANES:(g + 1) * LANES]
        parts.append(jnp.where(lane < ATTN_DH, qh, 0.0))
        parts.append(jnp.where(lane >= ATTN_DH, qh, 0.0))
    q4 = jnp.concatenate(parts, axis=0).astype(BF16)
    n4 = 4 * tq

    m_sc[...] = jnp.full_like(m_sc, -jnp.inf)
    l_sc[...] = jnp.zeros_like(l_sc)
    acc_sc[...] = jnp.zeros_like(acc_sc)

    def block(kb, masked):
        start = pl.multiple_of(kb * tq, tq)
        k = k_ref[pl.ds(start, tq), :].astype(BF16)
        v = v_ref[pl.ds(start, tq), :].astype(BF16)
        st = _nt(k, q4)
        if masked:
            key = lax.broadcasted_iota(I32, (tq, n4), 0)
            t = jnp.bitwise_and(lax.broadcasted_iota(I32, (tq, n4), 1), tq - 1)
            st = jnp.where(key <= t, st, -jnp.inf)
        m_old = m_sc[...]
        m_new = jnp.maximum(m_old, jnp.max(st, axis=0, keepdims=True))
        alpha = jnp.exp(m_old - m_new)
        p = jnp.exp(st - m_new)
        m_sc[...] = m_new
        l_sc[...] = alpha * l_sc[...] + jnp.sum(p, axis=0, keepdims=True)
        pv = lax.dot_general(v, p.astype(BF16), (((0,), (0,)), ((), ())), preferred_element_type=F32)
        acc_sc[...] = alpha * acc_sc[...] + pv

    def body(kb, carry):
        block(kb, False)
        return carry

    lax.fori_loop(0, i, body, 0)
    block(i, True)
    ot = acc_sc[...] / l_sc[...]
    for g in range(2):
        og = ot[:, (2 * g) * tq:(2 * g + 1) * tq] - lam * ot[:, (2 * g + 1) * tq:(2 * g + 2) * tq]
        ms = jnp.mean(og * og, axis=0, keepdims=True)
        nrm = (og * lax.rsqrt(ms + EPS)) * sub_ref[...] * (1.0 - lam_init)
        o_ref[:, g * LANES:(g + 1) * LANES] = nrm.T


def _attn_prompt(u, lams, subln, bsz, seq, lam_init):
    tq = _tile(seq, (256, 128))
    nq = seq // tq
    vec = pl.BlockSpec((1, ATTN_DH), lambda b, h, i: (0, 0))
    return pl.pallas_call(
        functools.partial(_attn_prompt_body, tq=tq, lam_init=lam_init),
        grid=(bsz, ATTN_KV_HEADS, nq),
        in_specs=[vec, vec, vec, vec,
                  pl.BlockSpec((ATTN_VD, 1), lambda b, h, i: (0, 0)),
                  pl.BlockSpec((tq, 2 * LANES), lambda b, h, i: (b * nq + i, h)),
                  pl.BlockSpec((seq, LANES), lambda b, h, i: (b, K0 // LANES + h)),
                  pl.BlockSpec((seq, LANES), lambda b, h, i: (b, V0 // LANES + h))],
        out_specs=pl.BlockSpec((tq, 2 * LANES), lambda b, h, i: (b * nq + i, h)),
        out_shape=jax.ShapeDtypeStruct((bsz * seq, ATTN_HEADS * ATTN_VD), F32),
        scratch_shapes=[pltpu.VMEM((1, 4 * tq), F32), pltpu.VMEM((1, 4 * tq), F32), pltpu.VMEM((ATTN_VD, 4 * tq), F32)],
        compiler_params=_cparams(("arbitrary", "arbitrary", "arbitrary"), 40),
        name="attn_prompt",
    )(*lams, subln.reshape(ATTN_VD, 1), u, u, u)


def _attn_sample_body(pt_ref, lq1, lk1, lq2, lk2, sub_ref, q_ref, kn_ref, vn_ref, *rest, n_pg, lam_init, tdec):
    del pt_ref
    k_refs, v_refs = rest[:n_pg], rest[n_pg:2 * n_pg]
    o_ref, m_sc, l_sc, acc_sc, qr_sc = rest[2 * n_pg:]
    j = pl.program_id(1)
    nrow = ATTN_KV_HEADS * 2 * 2 * tdec
    width = ATTN_KV_HEADS * LANES

    @pl.when(j == 0)
    def _():
        q = q_ref[...] * (ATTN_DH ** -0.5)
        lane = lax.broadcasted_iota(I32, (tdec, LANES), 1)
        qr_sc[...] = jnp.zeros_like(qr_sc)
        for kv in range(ATTN_KV_HEADS):
            for c in range(2):
                for g in range(2):
                    hd = kv * 2 + g
                    qh = q[:, hd * LANES:(hd + 1) * LANES]
                    piece = jnp.where((lane >= c * ATTN_DH) & (lane < (c + 1) * ATTN_DH), qh, 0.0)
                    n0 = ((kv * 2 + c) * 2 + g) * tdec
                    qr_sc[n0:n0 + tdec, kv * LANES:(kv + 1) * LANES] = piece
        m_sc[...] = jnp.full_like(m_sc, -jnp.inf)
        l_sc[...] = jnp.zeros_like(l_sc)
        acc_sc[...] = jnp.zeros_like(acc_sc)

    qr = qr_sc[...].astype(BF16)

    def update(s_list, v_list):
        m_old = m_sc[...]
        m_new = m_old
        for s in s_list:
            m_new = jnp.maximum(m_new, jnp.max(s, axis=1, keepdims=True))
        alpha = jnp.exp(m_old - m_new)
        l = alpha * l_sc[...]
        pv = None
        for s, v in zip(s_list, v_list):
            p = jnp.exp(s - m_new)
            l = l + jnp.sum(p, axis=1, keepdims=True)
            d = _mm(p.astype(BF16), v)
            pv = d if pv is None else pv + d
        m_sc[...] = m_new
        l_sc[...] = l
        acc_sc[...] = alpha * acc_sc[...] + pv

    def page(ref):
        return jnp.concatenate([ref[:, h, :] for h in range(ATTN_KV_HEADS)], axis=1).astype(BF16)

    for g0 in range(0, n_pg, PAGE_GROUP):
        grp = range(g0, min(g0 + PAGE_GROUP, n_pg))
        update([_nt(qr, page(k_refs[i])) for i in grp], [page(v_refs[i]) for i in grp])

    @pl.when(j == pl.num_programs(1) - 1)
    def _():
        lam = _lambda(lq1, lk1, lq2, lk2, lam_init)
        pad = jnp.zeros((LANES - tdec, width), F32)
        kn = jnp.concatenate([kn_ref[...], pad], axis=0).astype(BF16)
        vn = jnp.concatenate([vn_ref[...], pad], axis=0).astype(BF16)
        s = _nt(qr, kn)
        r = jnp.bitwise_and(lax.broadcasted_iota(I32, (nrow, LANES), 0), tdec - 1)
        c = lax.broadcasted_iota(I32, (nrow, LANES), 1)
        update([jnp.where(c <= r, s, -jnp.inf)], [vn])
        o = acc_sc[...] / l_sc[...]
        for kv in range(ATTN_KV_HEADS):
            for g in range(2):
                n0 = ((kv * 2 + 0) * 2 + g) * tdec
                n1 = ((kv * 2 + 1) * 2 + g) * tdec
                og = (o[n0:n0 + tdec, kv * LANES:(kv + 1) * LANES]
                      - lam * o[n1:n1 + tdec, kv * LANES:(kv + 1) * LANES])
                hd = kv * 2 + g
                o_ref[:, hd * LANES:(hd + 1) * LANES] = _rms(og, sub_ref[...]) * (1.0 - lam_init)


def _attn_sample(u, cache_k, cache_v, page_flat, layer, n_pool, lams, subln, bsz, tdec, n_pages, lam_init):
    del n_pool
    n_pg = _tile(n_pages, (16, 8, 4, 2, 1))
    nchunks = n_pages // n_pg
    page = cache_k.shape[2]
    width = ATTN_KV_HEADS * LANES
    nrow = ATTN_KV_HEADS * 2 * 2 * tdec
    assert nrow == LANES and tdec == SUBLANES
    vec = pl.BlockSpec((1, ATTN_DH), lambda b, j, pt: (0, 0))

    def page_spec(i):
        return pl.BlockSpec((None, None, page, ATTN_KV_HEADS, ATTN_VD),
                            lambda b, j, pt: (layer, pt[b * n_pages + j * n_pg + i], 0, 0, 0))

    in_specs = [vec, vec, vec, vec,
                pl.BlockSpec((1, ATTN_VD), lambda b, j, pt: (0, 0)),
                pl.BlockSpec((tdec, ATTN_HEADS * ATTN_VD), lambda b, j, pt: (b, 0)),
                pl.BlockSpec((tdec, width), lambda b, j, pt: (b, K0 // width)),
                pl.BlockSpec((tdec, width), lambda b, j, pt: (b, V0 // width))]
    in_specs += [page_spec(i) for i in range(n_pg)] * 2
    return pl.pallas_call(
        functools.partial(_attn_sample_body, n_pg=n_pg, lam_init=lam_init, tdec=tdec),
        grid_spec=pltpu.PrefetchScalarGridSpec(
            num_scalar_prefetch=1,
            grid=(bsz, nchunks),
            in_specs=in_specs,
            out_specs=pl.BlockSpec((tdec, ATTN_HEADS * ATTN_VD), lambda b, j, pt: (b, 0)),
            scratch_shapes=[pltpu.VMEM((nrow, 1), F32), pltpu.VMEM((nrow, 1), F32),
                            pltpu.VMEM((nrow, width), F32), pltpu.VMEM((nrow, width), F32)]),
        out_shape=jax.ShapeDtypeStruct((bsz * tdec, ATTN_HEADS * ATTN_VD), F32),
        compiler_params=_cparams(("arbitrary", "arbitrary"), 48),
        name="attn_sample",
    )(page_flat, *lams, subln, u, u, u, *([cache_k] * n_pg), *([cache_v] * n_pg))


def _causal_conv(x, prev, cw_ref, cb_ref):
    taps = cw_ref.shape[0]
    rowi = lax.broadcasted_iota(I32, x.shape, 0)
    acc = cb_ref[...] + x * cw_ref[taps - 1:taps, :]
    for s in range(1, taps):
        sh = jnp.where(rowi < s, pltpu.roll(prev, s, 0), pltpu.roll(x, s, 0))
        acc = acc + sh * cw_ref[taps - 1 - s:taps - s, :]
    return acc


def _pad_rows(x, rows):
    if x.shape[0] == rows:
        return x
    return jnp.concatenate([x, jnp.zeros((rows - x.shape[0], x.shape[1]), x.dtype)], axis=0)


def _ssd_body(xbc_ref, z_ref, dt_ref, st_ref, h0_ref, cw_ref, cb_ref, dtb_ref, alog_ref, dvec_ref, nw_ref,
              y_ref, hout_ref, prev_sc, h_sc, *, rows):
    c = pl.program_id(1)
    L = SSD_CHUNK
    width = SSD_HEADS * SSD_HEAD_DIM
    gw = SSD_GROUPS * SSD_STATE

    @pl.when(c == 0)
    def _():
        prev_sc[...] = jnp.concatenate([jnp.zeros((L - SUBLANES, prev_sc.shape[1]), F32), st_ref[...]], axis=0)
        h_sc[...] = h0_ref[...]

    x = _pad_rows(xbc_ref[...], L)
    conv = _causal_conv(x, prev_sc[...], cw_ref, cb_ref)
    prev_sc[...] = x
    xc = conv * jax.nn.sigmoid(conv)
    xs, bm, cm = xc[:, :width], xc[:, width:width + gw], xc[:, width + gw:]

    dt = jax.nn.softplus(_pad_rows(dt_ref[...], L) + dtb_ref[...])
    if rows < L:
        dt = jnp.where(lax.broadcasted_iota(I32, dt.shape, 0) < rows, dt, 0.0)
    da = dt * (-jnp.exp(alog_ref[...]))
    r0 = lax.broadcasted_iota(I32, (L, L), 0)
    c0 = lax.broadcasted_iota(I32, (L, L), 1)
    causal = r0 >= c0
    acum = jnp.dot(causal.astype(F32), da, precision=lax.Precision.HIGHEST, preferred_element_type=F32)
    acum_t, dt_t = acum.T, dt.T
    last = acum[L - 1:L, :]
    wend = jnp.exp(last - acum) * dt
    eac = jnp.exp(acum)
    elast = jnp.exp(last)
    lane = lax.broadcasted_iota(I32, (L, LANES), 1)
    first = lane < SSD_HEAD_DIM
    top = lax.broadcasted_iota(I32, (2 * SSD_HEAD_DIM, SSD_STATE), 0) < SSD_HEAD_DIM

    ys = []
    for pr in range(SSD_HEADS // 2):
        g = (2 * pr) // (SSD_HEADS // SSD_GROUPS)
        bg = bm[:, g * SSD_STATE:(g + 1) * SSD_STATE].astype(BF16)
        cg = cm[:, g * SSD_STATE:(g + 1) * SSD_STATE].astype(BF16)
        gmat = _nt(cg, bg)
        xp = xs[:, pr * LANES:(pr + 1) * LANES]
        xpb = xp.astype(BF16)
        hp = h_sc[pr * LANES:(pr + 1) * LANES, :]
        outs = []
        for hh in (2 * pr, 2 * pr + 1):
            seg = acum[:, hh:hh + 1] - acum_t[hh:hh + 1, :]
            dec = jnp.exp(jnp.where(causal, seg, -jnp.inf))
            w = gmat * dec * dt_t[hh:hh + 1, :]
            outs.append(_mm(w.astype(BF16), xpb))
        h_a, h_b = 2 * pr, 2 * pr + 1
        y_intra = jnp.where(first, outs[0], outs[1])
        e_pair = jnp.where(first, eac[:, h_a:h_a + 1], eac[:, h_b:h_b + 1])
        ys.append(y_intra + _nt(cg, hp.astype(BF16)) * e_pair)
        w_pair = jnp.where(first, wend[:, h_a:h_a + 1], wend[:, h_b:h_b + 1])
        upd = _mm((xp * w_pair).T.astype(BF16), bg)
        keep = jnp.where(top, elast[:, h_a:h_a + 1], elast[:, h_b:h_b + 1])
        h_sc[pr * LANES:(pr + 1) * LANES, :] = hp * keep + upd

    y = jnp.concatenate(ys, axis=1) + dvec_ref[...] * xs
    zz = _pad_rows(z_ref[...], L)
    gated = y * (zz * jax.nn.sigmoid(zz))
    y_ref[...] = _rms(gated, nw_ref[...])[:rows]

    @pl.when(c == pl.num_programs(1) - 1)
    def _():
        hout_ref[...] = h_sc[...]


def _ssd(u, st8, h0, cw, cb, dtb, alog, dvec, nw, bsz, seq):
    rows = SSD_CHUNK if seq % SSD_CHUNK == 0 else seq
    assert rows == SSD_CHUNK or (rows == seq and rows % SUBLANES == 0 and rows <= SSD_CHUNK)
    nch = seq // rows
    cc = cw.shape[1]
    width = SSD_HEADS * SSD_HEAD_DIM
    hp = SSD_HEADS * SSD_HEAD_DIM
    const = lambda shape: pl.BlockSpec(shape, lambda b, c: (0,) * len(shape))
    return pl.pallas_call(
        functools.partial(_ssd_body, rows=rows),
        grid=(bsz, nch),
        in_specs=[pl.BlockSpec((rows, cc), lambda b, c: (b * nch + c, XBC0 // cc)),
                  pl.BlockSpec((rows, width), lambda b, c: (b * nch + c, Z0 // width)),
                  pl.BlockSpec((rows, LANES), lambda b, c: (b * nch + c, DT0 // LANES)),
                  pl.BlockSpec((None, SUBLANES, cc), lambda b, c: (b, 0, 0)),
                  pl.BlockSpec((None, hp, SSD_STATE), lambda b, c: (b, 0, 0)),
                  const((SSD_CONV, cc)), const((1, cc)), const((1, LANES)), const((1, LANES)),
                  const((1, width)), const((1, width))],
        out_specs=[pl.BlockSpec((rows, width), lambda b, c: (b * nch + c, 0)),
                   pl.BlockSpec((None, hp, SSD_STATE), lambda b, c: (b, 0, 0))],
        out_shape=[jax.ShapeDtypeStruct((bsz * seq, width), F32),
                   jax.ShapeDtypeStruct((bsz, hp, SSD_STATE), F32)],
        scratch_shapes=[pltpu.VMEM((SSD_CHUNK, cc), F32), pltpu.VMEM((hp, SSD_STATE), F32)],
        compiler_params=_cparams(("arbitrary", "arbitrary"), 32),
        name="ssd",
    )(u, u, u, st8, h0, cw, cb, dtb, alog, dvec, nw)


def _expm1(t):
    u = jnp.exp(t)
    small = (u - 1.0) * t / jnp.log(u)
    return jnp.where(t < -1.0, u - 1.0, jnp.where(u == 1.0, t, small))


def _lru_body(xr_ref, gate_ref, st_ref, h0_ref, cw_ref, cb_ref, wa_ref, ba_ref, wx_ref, bx_ref, lam_ref,
              y_ref, hout_ref, prev_sc, h_sc, *, rows):
    c = pl.program_id(1)

    @pl.when(c == 0)
    def _():
        st = st_ref[...]
        if rows > SUBLANES:
            st = jnp.concatenate([jnp.zeros((rows - SUBLANES, st.shape[1]), F32), st], axis=0)
        prev_sc[...] = st
        h_sc[...] = h0_ref[...]

    x = xr_ref[...]
    xc = _causal_conv(x, prev_sc[...], cw_ref, cb_ref)
    prev_sc[...] = x
    xcb = xc.astype(BF16)
    r = jax.nn.sigmoid(_mm(xcb, wa_ref[...]) + ba_ref[...])
    ig = jax.nn.sigmoid(_mm(xcb, wx_ref[...]) + bx_ref[...])
    log_a = (-LRU_C) * r * jax.nn.softplus(-lam_ref[...])
    a = jnp.exp(log_a)
    b = jnp.sqrt(-_expm1(2.0 * log_a)) * (ig * xc)
    rowi = lax.broadcasted_iota(I32, a.shape, 0)
    d = 1
    while d < rows:
        ok = rowi >= d
        b = jnp.where(ok, a * pltpu.roll(b, d, 0) + b, b)
        a = jnp.where(ok, a * pltpu.roll(a, d, 0), a)
        d *= 2
    h = b + a * h_sc[...]
    h_sc[...] = h[rows - 1:rows, :]
    y_ref[...] = h * jax.nn.gelu(gate_ref[...])

    @pl.when(c == pl.num_programs(1) - 1)
    def _():
        hout_ref[...] = h[rows - 1:rows, :]


def _lru(u, st8, h0, cw, cb, wa, ba, wx, bx, lam, bsz, seq):
    rows = _tile(seq, (128, 64, 32, 16, 8))
    nch = seq // rows
    w = cw.shape[1]
    const = lambda shape: pl.BlockSpec(shape, lambda b, c: (0,) * len(shape))
    return pl.pallas_call(
        functools.partial(_lru_body, rows=rows),
        grid=(bsz, nch),
        in_specs=[pl.BlockSpec((rows, w), lambda b, c: (b * nch + c, XR0 // w)),
                  pl.BlockSpec((rows, w), lambda b, c: (b * nch + c, GATE0 // w)),
                  pl.BlockSpec((None, SUBLANES, w), lambda b, c: (b, 0, 0)),
                  pl.BlockSpec((None, 1, w), lambda b, c: (b, 0, 0)),
                  const((LRU_CONV, w)), const((1, w)), const((w, w)), const((1, w)), const((w, w)), const((1, w)),
                  const((1, w))],
        out_specs=[pl.BlockSpec((rows, w), lambda b, c: (b * nch + c, 0)),
                   pl.BlockSpec((None, 1, w), lambda b, c: (b, 0, 0))],
        out_shape=[jax.ShapeDtypeStruct((bsz * seq, w), F32), jax.ShapeDtypeStruct((bsz, 1, w), F32)],
        scratch_shapes=[pltpu.VMEM((rows, w), F32), pltpu.VMEM((1, w), F32)],
        compiler_params=_cparams(("arbitrary", "arbitrary"), 32),
        name="lru",
    )(u, u, st8, h0, cw, cb, wa, ba, wx, bx, lam)


def _out_proj_body(x_ref, a_ref, s_ref, l_ref, wo_ref, nw_ref, wq_ref, x1_ref, qm_ref):
    wa, ws = a_ref.shape[1], s_ref.shape[1]
    acc = x_ref[...] + _mm(a_ref[...].astype(BF16), wo_ref[0:wa, :])
    acc = acc + _mm(s_ref[...].astype(BF16), wo_ref[wa:wa + ws, :])
    acc = acc + _mm(l_ref[...].astype(BF16), wo_ref[wa + ws:, :])
    x1_ref[...] = acc
    qm_ref[...] = _mm(_rms(acc, nw_ref[...]).astype(BF16), wq_ref[...])


def _out_proj(x, attn, ssd, lru, wo, nw, wq):
    t, d = x.shape
    tm = _tile(t, (256, 128))
    row = lambda w: pl.BlockSpec((tm, w), lambda i: (i, 0))
    const = lambda shape: pl.BlockSpec(shape, lambda i: (0, 0))
    return pl.pallas_call(
        _out_proj_body,
        grid=(t // tm,),
        in_specs=[row(d), row(attn.shape[1]), row(ssd.shape[1]), row(lru.shape[1]),
                  const(wo.shape), const((1, d)), const(wq.shape)],
        out_specs=[row(d), row(wq.shape[1])],
        out_shape=[jax.ShapeDtypeStruct((t, d), F32), jax.ShapeDtypeStruct((t, wq.shape[1]), F32)],
        compiler_params=_cparams(("arbitrary",), 48),
        name="out_proj",
    )(x, attn, ssd, lru, wo, nw, wq)


def _mem_kv_body(m_ref, wk_ref, wv_ref, k_ref, v_ref):
    mb = m_ref[...].astype(BF16)
    k_ref[...] = _mm(mb, wk_ref[...])
    v_ref[...] = _mm(mb, wv_ref[...])


def _mem_kv(mem, wk, wv):
    t, d = mem.shape
    tm = _tile(t, (256, 128))
    w = wk.shape[1]
    return pl.pallas_call(
        _mem_kv_body,
        grid=(t // tm,),
        in_specs=[pl.BlockSpec((tm, d), lambda i: (i, 0)),
                  pl.BlockSpec((d, w), lambda i: (0, 0)), pl.BlockSpec((d, w), lambda i: (0, 0))],
        out_specs=[pl.BlockSpec((tm, w), lambda i: (i, 0)), pl.BlockSpec((tm, w), lambda i: (i, 0))],
        out_shape=[jax.ShapeDtypeStruct((t, w), F32), jax.ShapeDtypeStruct((t, w), F32)],
        compiler_params=_cparams(("arbitrary",), 32),
        name="mem_kv",
    )(mem, wk, wv)


def _xattn_body(q_ref, k_ref, v_ref, o_ref):
    for h in range(MEM_HEADS):
        sl = slice(h * MEM_DH, (h + 1) * MEM_DH)
        s = _nt(q_ref[:, sl].astype(BF16), k_ref[:, sl].astype(BF16)) * (MEM_DH ** -0.5)
        e = jnp.exp(s - jnp.max(s, axis=1, keepdims=True))
        o = _mm(e.astype(BF16), v_ref[:, sl].astype(BF16))
        o_ref[:, sl] = o / jnp.sum(e, axis=1, keepdims=True)


def _xattn(qm, mem_k, mem_v, bsz, seq):
    tq = _tile(seq, (256, 128))
    nq = seq // tq
    w = qm.shape[1]
    m = mem_k.shape[1]
    return pl.pallas_call(
        _xattn_body,
        grid=(bsz, nq),
        in_specs=[pl.BlockSpec((tq, w), lambda b, i: (b * nq + i, 0)),
                  pl.BlockSpec((None, m, w), lambda b, i: (b, 0, 0)),
                  pl.BlockSpec((None, m, w), lambda b, i: (b, 0, 0))],
        out_specs=pl.BlockSpec((tq, w), lambda b, i: (b * nq + i, 0)),
        out_shape=jax.ShapeDtypeStruct(qm.shape, F32),
        compiler_params=_cparams(("arbitrary", "arbitrary"), 32),
        name="xattn",
    )(qm, mem_k, mem_v)


def _route_body(x1_ref, om_ref, wo_ref, nw_ref, rw_ref, rb_ref, x2_ref, hf_ref, eid_ref, gate_ref):
    x2 = x1_ref[...] + _mm(om_ref[...].astype(BF16), wo_ref[...])
    x2_ref[...] = x2
    hf = _rms(x2, nw_ref[...])
    hf_ref[...] = hf
    logits = jnp.dot(hf, rw_ref[...], precision=lax.Precision.HIGHEST, preferred_element_type=F32) + rb_ref[...]
    lane = lax.broadcasted_iota(I32, logits.shape, 1).astype(F32)
    ninf = -jnp.inf
    big = float(LANES)

    def first_argmax(v, mx):
        return jnp.min(jnp.where(v == mx, lane, big), axis=1, keepdims=True)

    lg = jnp.where(lane < MOE_GROUPS, logits, ninf)
    mg = jnp.max(lg, axis=1, keepdims=True)
    p_grp = 1.0 / jnp.sum(jnp.exp(lg - mg), axis=1, keepdims=True)
    lo = MOE_GROUPS + MOE_EXPERTS_PER_GROUP * first_argmax(lg, mg)
    le = jnp.where((lane >= lo) & (lane < lo + MOE_EXPERTS_PER_GROUP), logits, ninf)
    v1 = jnp.max(le, axis=1, keepdims=True)
    i1 = first_argmax(le, v1)
    le2 = jnp.where(lane == i1, ninf, le)
    v2 = jnp.max(le2, axis=1, keepdims=True)
    i2 = first_argmax(le2, v2)
    e2 = jnp.exp(v2 - v1)
    den = 1.0 + e2
    g1 = p_grp * (1.0 / den)
    g2 = p_grp * (e2 / den)
    gate_ref[...] = jnp.where(lane == 0.0, g1, jnp.where(lane == 1.0, g2, 0.0))
    eid_ref[...] = jnp.where(lane == 0.0, i1 - MOE_GROUPS, jnp.where(lane == 1.0, i2 - MOE_GROUPS, 0.0)).astype(I32)


def _route(x1, om, wo, nw, rw, rb):
    t, d = x1.shape
    tm = _tile(t, (256, 128))
    row = lambda w: pl.BlockSpec((tm, w), lambda i: (i, 0))
    const = lambda shape: pl.BlockSpec(shape, lambda i: (0, 0))
    return pl.pallas_call(
        _route_body,
        grid=(t // tm,),
        in_specs=[row(d), row(om.shape[1]), const(wo.shape), const((1, d)), const(rw.shape), const((1, LANES))],
        out_specs=[row(d), row(d), row(LANES), row(LANES)],
        out_shape=[jax.ShapeDtypeStruct((t, d), F32), jax.ShapeDtypeStruct((t, d), F32),
                   jax.ShapeDtypeStruct((t, LANES), I32), jax.ShapeDtypeStruct((t, LANES), F32)],
        compiler_params=_cparams(("arbitrary",), 40),
        name="route",
    )(x1, om, wo, nw, rw, rb)


ROW_UNROLL = 8


def _moe_body(grow_ref, srow_ref, be_ref, base_ref, cnt_ref, hf_hbm, wg_ref, wu_ref, wd_ref, o_hbm,
              xbuf, ybuf, wgb, wub, wdb, gsem, ssem, *, rb):
    g = pl.program_id(0)
    ng = pl.num_programs(0)
    slot = lax.rem(g, 2)
    cnt = cnt_ref[g]

    def for_rows(n, fn):
        n_grp = lax.shift_right_logical(n, ROW_UNROLL.bit_length() - 1)

        def grp(q, c):
            for u in range(ROW_UNROLL):
                fn(q * ROW_UNROLL + u)
            return c

        def one(i, c):
            fn(i)
            return c

        lax.fori_loop(0, n_grp, grp, 0)
        lax.fori_loop(n_grp * ROW_UNROLL, n, one, 0)

    def gather_row(sl, i, src):
        return pltpu.make_async_copy(hf_hbm.at[pl.ds(src, 1), :], xbuf.at[sl, pl.ds(i, 1), :], gsem.at[sl])

    def scatter_row(sl, i, dst):
        return pltpu.make_async_copy(ybuf.at[sl, pl.ds(i, 1), :], o_hbm.at[pl.ds(dst, 1), :], ssem.at[sl])

    def start_gathers(blk, sl):
        base = base_ref[blk]
        for_rows(cnt_ref[blk], lambda i: gather_row(sl, i, grow_ref[base + i]).start())

    def start_scatters(blk, sl):
        base = base_ref[blk]
        for_rows(cnt_ref[blk], lambda i: scatter_row(sl, i, srow_ref[base + i]).start())

    def wait_rows(blk, sl, row_copy, block_copy):
        n = cnt_ref[blk]

        @pl.when(n == rb)
        def _():
            block_copy.wait()

        @pl.when(n < rb)
        def _():
            for_rows(n, lambda i: row_copy(sl, i, 0).wait())

    def wait_gathers(blk, sl):
        wait_rows(blk, sl, gather_row,
                  pltpu.make_async_copy(hf_hbm.at[pl.ds(0, rb), :], xbuf.at[sl], gsem.at[sl]))

    def wait_scatters(blk, sl):
        wait_rows(blk, sl, scatter_row,
                  pltpu.make_async_copy(ybuf.at[sl], o_hbm.at[pl.ds(0, rb), :], ssem.at[sl]))

    @pl.when(g == 0)
    def _():
        xbuf[...] = jnp.zeros_like(xbuf)
        start_gathers(0, 0)

    prev_e = be_ref[jnp.maximum(g - 1, 0)]

    @pl.when((cnt > 0) & ((g == 0) | (prev_e != be_ref[g])))
    def _():
        wgb[...] = wg_ref[...].astype(BF16)
        wub[...] = wu_ref[...].astype(BF16)
        wdb[...] = wd_ref[...].astype(BF16)

    wait_gathers(g, slot)

    @pl.when(g + 1 < ng)
    def _():
        start_gathers(g + 1, 1 - slot)

    @pl.when(g >= 2)
    def _():
        wait_scatters(g - 2, slot)

    @pl.when(cnt > 0)
    def _():
        xb = xbuf[slot].astype(BF16)
        hg = _mm(xb, wgb[...])
        act = (hg * jax.nn.sigmoid(hg)) * _mm(xb, wub[...])
        ybuf[slot] = _mm(act.astype(BF16), wdb[...])
        start_scatters(g, slot)

    @pl.when(g == ng - 1)
    def _():
        @pl.when(g >= 1)
        def _():
            wait_scatters(g - 1, 1 - slot)

        wait_scatters(g, slot)


def _moe_ffn(hf, grow, srow, blk_e, blk_base, blk_cnt, wg, wu, wd, layer, rb):
    t, d = hf.shape
    n_asg = grow.shape[0]
    ff = wg.shape[-1]
    n_blk = blk_e.shape[0]
    wspec = lambda shape: pl.BlockSpec((None, None) + shape, lambda g, gr, sr, be, bb, bc: (layer, be[g], 0, 0))
    return pl.pallas_call(
        functools.partial(_moe_body, rb=rb),
        grid_spec=pltpu.PrefetchScalarGridSpec(
            num_scalar_prefetch=5,
            grid=(n_blk,),
            in_specs=[pl.BlockSpec(memory_space=pl.ANY), wspec((d, ff)), wspec((d, ff)), wspec((ff, d))],
            out_specs=pl.BlockSpec(memory_space=pl.ANY),
            scratch_shapes=[pltpu.VMEM((2, rb, d), F32), pltpu.VMEM((2, rb, d), F32),
                            pltpu.VMEM((d, ff), BF16), pltpu.VMEM((d, ff), BF16), pltpu.VMEM((ff, d), BF16),
                            pltpu.SemaphoreType.DMA((2,)), pltpu.SemaphoreType.DMA((2,))]),
        out_shape=jax.ShapeDtypeStruct((n_asg, d), F32),
        compiler_params=_cparams(("arbitrary",), 56),
        name="moe_ffn",
    )(grow, srow, blk_e, blk_base, blk_cnt, hf, wg, wu, wd)


def _moe_plan(eid, n_tok, rb):
    n_asg = n_tok * MOE_TOPK
    flat_e = eid.reshape(-1)
    order = jnp.argsort(flat_e, stable=True).astype(I32)
    grow = order // MOE_TOPK
    srow = (order % MOE_TOPK) * n_tok + grow
    counts = jnp.sum((flat_e[:, None] == jnp.arange(MOE_EXPERTS, dtype=I32)[None, :]).astype(I32), axis=0)
    starts = jnp.cumsum(counts) - counts
    nblk = (counts + rb - 1) // rb
    blk_end = jnp.cumsum(nblk)
    n_blk = -(-n_asg // rb) + MOE_EXPERTS
    gi = jnp.arange(n_blk, dtype=I32)
    be = jnp.minimum(jnp.sum((gi[:, None] >= blk_end[None, :]).astype(I32), axis=1), MOE_EXPERTS - 1)
    r = gi - (blk_end - nblk)[be]
    cnt = jnp.clip(counts[be] - r * rb, 0, rb).astype(I32)
    base = jnp.minimum(starts[be] + r * rb, n_asg - 1).astype(I32)
    return grow.astype(I32), srow.astype(I32), be, base, cnt


def _combine_body(x2_ref, o0_ref, o1_ref, g_ref, fw_ref, x3_ref, *, final):
    g = g_ref[...]
    x3 = x2_ref[...] + (g[:, 0:1] * o0_ref[...] + g[:, 1:2] * o1_ref[...])
    x3_ref[...] = _rms(x3, fw_ref[...]) if final else x3


def _combine(x2, o2, gates, fw, final):
    t, d = x2.shape
    tm = _tile(t, (256, 128))
    nt = t // tm
    return pl.pallas_call(
        functools.partial(_combine_body, final=final),
        grid=(nt,),
        in_specs=[pl.BlockSpec((tm, d), lambda i: (i, 0)),
                  pl.BlockSpec((tm, d), lambda i: (i, 0)),
                  pl.BlockSpec((tm, d), lambda i: (nt + i, 0)),
                  pl.BlockSpec((tm, LANES), lambda i: (i, 0)),
                  pl.BlockSpec((1, d), lambda i: (0, 0))],
        out_specs=pl.BlockSpec((tm, d), lambda i: (i, 0)),
        out_shape=jax.ShapeDtypeStruct((t, d), F32),
        compiler_params=_cparams(("arbitrary",), 32),
        name="combine",
    )(x2, o2, o2, gates, fw)


def _rope_tables(pos):
    half = ATTN_DH // 2
    inv = ROPE_THETA ** (-jnp.arange(half, dtype=F32) / half)
    ang = pos.astype(F32)[:, None] * inv[None, :]
    cos, sin = jnp.cos(ang), jnp.sin(ang)
    reps = LANES // ATTN_DH
    return jnp.tile(jnp.concatenate([cos, cos], axis=1), (1, reps)), jnp.tile(jnp.concatenate([-sin, sin], axis=1), (1, reps))


def _block_diag(w):
    g, a, b = w.shape
    eye = jnp.eye(g, dtype=w.dtype)
    return (w[:, :, None, :] * eye[:, None, :, None]).reshape(g * a, g * b)


def _pad_lanes(v, n):
    return jnp.pad(v.reshape(1, -1), ((0, 0), (0, n - v.size)))


def _layer(x, grp, lw, layer, lam_init, final, final_norm):
    bsz, seq = grp["bsz"], grp["seq"]
    u = _in_proj(x, lw["norm_mix"], lw["w_in"], grp["cos"], grp["sin"])
    lams = lw["lams"]
    if grp["paged"]:
        attn = _attn_sample(u, grp["cache_k"], grp["cache_v"], grp["page_flat"], layer, grp["n_pool"], lams,
                            lw["subln"], bsz, seq, grp["n_pages"], lam_init)
    else:
        attn = _attn_prompt(u, lams, lw["subln"], bsz, seq, lam_init)
    ssd, h_ssd = _ssd(u, grp["ssd_conv"][layer], grp["ssd_state"][layer], lw["ssd_conv_w"], lw["ssd_conv_b"],
                      lw["ssd_dt_bias"], lw["ssd_a_log"], lw["ssd_d"], lw["ssd_norm"], bsz, seq)
    lru, h_lru = _lru(u, grp["lru_conv"][layer], grp["lru_state"][layer], lw["lru_conv_w"], lw["lru_conv_b"],
                      lw["lru_wa"], lw["lru_ba"], lw["lru_wx"], lw["lru_bx"], lw["lru_lambda"], bsz, seq)
    x1, qm = _out_proj(x, attn, ssd, lru, lw["w_out"], lw["norm_mem"], lw["wq_mem"])
    om = _xattn(qm, grp["mem_k"][layer], grp["mem_v"][layer], bsz, seq)
    x2, hf, eid, gate = _route(x1, om, lw["wo_mem"], lw["norm_ffn"], lw["router_w"], lw["router_b"])
    n_tok = x.shape[0]
    rb = 256 if n_tok * MOE_TOPK >= 256 * MOE_EXPERTS else 128
    plan = _moe_plan(eid[:, :MOE_TOPK], n_tok, rb)
    o2 = _moe_ffn(hf, *plan, lw["moe_w_gate"], lw["moe_w_up"], lw["moe_w_down"], layer, rb)
    x3 = _combine(x2, o2, gate, final_norm, final)
    ur = u.reshape(bsz, seq, UW)
    width_kv = ATTN_KV_HEADS * ATTN_VD
    outs = dict(
        k=ur[:, :, K0:K0 + width_kv].reshape(bsz, seq, ATTN_KV_HEADS, ATTN_VD),
        v=ur[:, :, V0:V0 + width_kv].reshape(bsz, seq, ATTN_KV_HEADS, ATTN_VD),
        ssd_conv=ur[:, seq - (SSD_CONV - 1):, XBC0:Z0],
        ssd_state=h_ssd.reshape(bsz, SSD_HEADS, SSD_HEAD_DIM, SSD_STATE),
        lru_conv=ur[:, seq - (LRU_CONV - 1):, XR0:GATE0],
        lru_state=h_lru.reshape(bsz, -1),
    )
    return x3, outs


def _conv_state8(st):
    return jnp.pad(st, ((0, 0), (0, 0), (SUBLANES - st.shape[2], 0), (0, 0)))


def kernel(x_prompt, x_sample, cache_k, cache_v, cache_mem_k, cache_mem_v, state_ssd_conv, state_ssd, state_lru_conv, state_lru, page_table, mem_prompt, norm_mix, w_in, attn_lambda_q1, attn_lambda_k1, attn_lambda_q2, attn_lambda_k2, attn_subln, ssd_conv_w, ssd_conv_b, ssd_dt_bias, ssd_a_log, ssd_d, ssd_norm, lru_conv_w, lru_conv_b, lru_wa, lru_ba, lru_wx, lru_bx, lru_lambda, w_out, norm_mem, wq_mem, wk_mem, wv_mem, wo_mem, norm_ffn, router_group_w, router_group_b, router_expert_w, router_expert_b, moe_w_gate, moe_w_up, moe_w_down, final_norm):
    depth = w_in.shape[0]
    bp, tp, d = x_prompt.shape
    bs, ts, _ = x_sample.shape
    n_pool, page = cache_k.shape[1], cache_k.shape[2]
    n_pages = page_table.shape[1]
    past_len = n_pages * page
    n_mem = mem_prompt.shape[1]
    ssd_cc = state_ssd_conv.shape[-1]
    lru_w = state_lru.shape[-1]
    kvw = ATTN_KV_HEADS * ATTN_VD
    memw = MEM_HEADS * MEM_DH

    cos_p, sin_p = _rope_tables(jnp.tile(jnp.arange(tp, dtype=I32), bp))
    cos_s, sin_s = _rope_tables(jnp.tile(past_len + jnp.arange(ts, dtype=I32), bs))

    prompt = dict(bsz=bp, seq=tp, paged=False, cos=cos_p, sin=sin_p,
                  ssd_conv=jnp.zeros((depth, bp, SUBLANES, ssd_cc), F32),
                  ssd_state=jnp.zeros((depth, bp, SSD_HEADS * SSD_HEAD_DIM, SSD_STATE), F32),
                  lru_conv=jnp.zeros((depth, bp, SUBLANES, lru_w), F32),
                  lru_state=jnp.zeros((depth, bp, 1, lru_w), F32))
    sample = dict(bsz=bs, seq=ts, paged=True, cos=cos_s, sin=sin_s,
                  cache_k=cache_k, cache_v=cache_v,
                  page_flat=page_table.reshape(-1), n_pool=n_pool, n_pages=n_pages,
                  ssd_conv=_conv_state8(state_ssd_conv),
                  ssd_state=state_ssd.reshape(depth, bs, SSD_HEADS * SSD_HEAD_DIM, SSD_STATE),
                  lru_conv=_conv_state8(state_lru_conv),
                  lru_state=state_lru.reshape(depth, bs, 1, lru_w),
                  mem_k=cache_mem_k.reshape(depth, bs, n_mem, memw), mem_v=cache_mem_v.reshape(depth, bs, n_mem, memw))

    xp = x_prompt.reshape(bp * tp, d)
    xs = x_sample.reshape(bs * ts, d)
    po, so, mks, mvs = [], [], [], []
    for l in range(depth):
        w = w_in[l]
        zc = V0 + kvw
        xc0 = zc + SSD_HEADS * SSD_HEAD_DIM
        dc = xc0 + ssd_cc
        rc = dc + SSD_HEADS
        w_pad = jnp.concatenate([w[:, :zc], w[:, xc0:dc], w[:, zc:xc0], w[:, rc:rc + 2 * lru_w], w[:, dc:rc],
                                 jnp.zeros((d, UW - DT0 - SSD_HEADS), F32)], axis=1).astype(BF16)
        router_w = jnp.pad(jnp.concatenate([router_group_w[l], router_expert_w[l]], axis=1),
                           ((0, 0), (0, LANES - MOE_GROUPS - MOE_EXPERTS)))
        router_b = _pad_lanes(jnp.concatenate([router_group_b[l], router_expert_b[l]]), LANES)
        lw = dict(
            norm_mix=norm_mix[l].reshape(1, d), w_in=w_pad,
            lams=[v[l].reshape(1, ATTN_DH) for v in (attn_lambda_q1, attn_lambda_k1, attn_lambda_q2, attn_lambda_k2)],
            subln=attn_subln[l].reshape(1, ATTN_VD),
            ssd_conv_w=ssd_conv_w[l], ssd_conv_b=ssd_conv_b[l].reshape(1, -1),
            ssd_dt_bias=_pad_lanes(ssd_dt_bias[l], LANES), ssd_a_log=_pad_lanes(ssd_a_log[l], LANES),
            ssd_d=jnp.repeat(ssd_d[l], SSD_HEAD_DIM).reshape(1, -1), ssd_norm=ssd_norm[l].reshape(1, -1),
            lru_conv_w=lru_conv_w[l], lru_conv_b=lru_conv_b[l].reshape(1, -1),
            lru_wa=_block_diag(lru_wa[l]).astype(BF16), lru_ba=lru_ba[l].reshape(1, -1),
            lru_wx=_block_diag(lru_wx[l]).astype(BF16), lru_bx=lru_bx[l].reshape(1, -1),
            lru_lambda=lru_lambda[l].reshape(1, -1),
            w_out=w_out[l].astype(BF16), norm_mem=norm_mem[l].reshape(1, d), wq_mem=wq_mem[l].astype(BF16),
            wo_mem=wo_mem[l].astype(BF16), norm_ffn=norm_ffn[l].reshape(1, d),
            router_w=router_w, router_b=router_b,
            moe_w_gate=moe_w_gate, moe_w_up=moe_w_up, moe_w_down=moe_w_down,
        )
        lam_init = 0.8 - 0.6 * math.exp(-0.3 * l)
        mk, mv = _mem_kv(mem_prompt.reshape(bp * n_mem, d), wk_mem[l].astype(BF16), wv_mem[l].astype(BF16))
        mks.append(mk.reshape(bp, n_mem, MEM_HEADS, MEM_DH))
        mvs.append(mv.reshape(bp, n_mem, MEM_HEADS, MEM_DH))
        pg = dict(prompt, mem_k=[None] * l + [mk.reshape(bp, n_mem, memw)], mem_v=[None] * l + [mv.reshape(bp, n_mem, memw)])
        final = l == depth - 1
        xp, o = _layer(xp, pg, lw, l, lam_init, final, final_norm.reshape(1, d))
        po.append(o)
        xs, o = _layer(xs, sample, lw, l, lam_init, final, final_norm.reshape(1, d))
        so.append(o)

    st = lambda outs, key: jnp.stack([o[key] for o in outs])
    return (xp.reshape(bp, tp, d), xs.reshape(bs, ts, d),
            st(po, "k"), st(po, "v"), jnp.stack(mks), jnp.stack(mvs),
            st(po, "ssd_conv"), st(po, "ssd_state"), st(po, "lru_conv"), st(po, "lru_state"),
            st(so, "k"), st(so, "v"), st(so, "ssd_conv"), st(so, "ssd_state"), st(so, "lru_conv"), st(so, "lru_state"))
```

```python
import functools
import math

import jax
import jax.numpy as jnp
from jax import lax
from jax.experimental import pallas as pl
from jax.experimental.pallas import tpu as pltpu

F32, BF16, I32 = jnp.float32, jnp.bfloat16, jnp.int32
EPS = 1e-6
LANES = 128
SUBLANES = 8
MIB = 1024 * 1024

ATTN_HEADS, ATTN_KV_HEADS, ATTN_DH = 8, 4, 64
ATTN_VD = 2 * ATTN_DH
ROPE_THETA = 10000.0
SSD_HEADS, SSD_HEAD_DIM, SSD_GROUPS, SSD_STATE, SSD_CONV = 8, 64, 2, 128, 4
SSD_CHUNK = 128
LRU_BLOCKS, LRU_CONV, LRU_C = 8, 4, 8.0
MEM_HEADS, MEM_DH = 4, 128
MOE_GROUPS, MOE_EXPERTS_PER_GROUP, MOE_TOPK = 4, 8, 2
MOE_EXPERTS = MOE_GROUPS * MOE_EXPERTS_PER_GROUP

Q0, K0, V0, XBC0, Z0, XR0, GATE0, DT0, UW = 0, 1024, 1536, 2048, 3072, 3584, 4096, 4608, 5120
PROJ_TN = 1024
PAGE_GROUP = 4


def _cparams(sem, vmem_mib):
    return pltpu.CompilerParams(dimension_semantics=sem, vmem_limit_bytes=vmem_mib * MIB)


def _tile(n, prefs):
    for p in prefs:
        if n % p == 0:
            return p
    return n


def _rms(x, w):
    return (x * lax.rsqrt(jnp.mean(x * x, axis=-1, keepdims=True) + EPS)) * w


def _nt(a, b):
    return lax.dot_general(a, b, (((1,), (1,)), ((), ())), preferred_element_type=F32)


def _mm(a, b):
    return jnp.dot(a, b, preferred_element_type=F32)


def _rope128(yc, cos, sin):
    lane = lax.broadcasted_iota(I32, yc.shape, 1)
    sw = jnp.where((lane % 64) < 32, pltpu.roll(yc, 96, 1), pltpu.roll(yc, 32, 1))
    return yc * cos + sw * sin


def _in_proj_body(x_ref, nw_ref, w_ref, cos_ref, sin_ref, o_ref):
    j = pl.program_id(0)
    h = _rms(x_ref[...], nw_ref[...])
    y = _mm(h.astype(BF16), w_ref[...])
    n_chunks = y.shape[1] // LANES

    def store_rope(n_rope):
        cos, sin = cos_ref[...], sin_ref[...]
        for c in range(n_rope):
            o_ref[:, c * LANES:(c + 1) * LANES] = _rope128(y[:, c * LANES:(c + 1) * LANES], cos, sin)
        if n_rope < n_chunks:
            o_ref[:, n_rope * LANES:] = y[:, n_rope * LANES:]

    @pl.when(j == 0)
    def _():
        store_rope(n_chunks)

    @pl.when(j == 1)
    def _():
        store_rope((V0 - K0) // LANES)

    @pl.when(j >= 2)
    def _():
        o_ref[...] = y


def _in_proj(x, nw, w_pad, cos, sin):
    t, d = x.shape
    tm = _tile(t, (512, 256, 128))
    return pl.pallas_call(
        _in_proj_body,
        grid=(UW // PROJ_TN, t // tm),
        in_specs=[pl.BlockSpec((tm, d), lambda j, i: (i, 0)),
                  pl.BlockSpec((1, d), lambda j, i: (0, 0)),
                  pl.BlockSpec((d, PROJ_TN), lambda j, i: (0, j)),
                  pl.BlockSpec((tm, LANES), lambda j, i: (i, 0)),
                  pl.BlockSpec((tm, LANES), lambda j, i: (i, 0))],
        out_specs=pl.BlockSpec((tm, PROJ_TN), lambda j, i: (i, j)),
        out_shape=jax.ShapeDtypeStruct((t, UW), F32),
        compiler_params=_cparams(("arbitrary", "arbitrary"), 48),
        name="in_proj",
    )(x, nw, w_pad, cos, sin)


def _lambda(lq1, lk1, lq2, lk2, lam_init):
    s1 = jnp.sum(lq1[...] * lk1[...], axis=-1, keepdims=True)
    s2 = jnp.sum(lq2[...] * lk2[...], axis=-1, keepdims=True)
    return jnp.exp(s1) - jnp.exp(s2) + lam_init


def _attn_prompt_body(lq1, lk1, lq2, lk2, sub_ref, q_ref, k_ref, v_ref, o_ref, acc_sc, *, tq, tk, lam_init):
    i = pl.program_id(2)
    lam = _lambda(lq1, lk1, lq2, lk2, lam_init)
    q = q_ref[...] * (ATTN_DH ** -0.5 * math.log2(math.e))
    lane = lax.broadcasted_iota(I32, (tq, LANES), 1)
    n4 = 4 * tq
    parts = []
    for g in range(2):
        qh = q[:, g * LANES:(g + 1) * LANES]
        parts.append(jnp.where(lane < ATTN_DH, qh, 0.0))
        parts.append(jnp.where(lane >= ATTN_DH, qh, 0.0))
    q4 = jnp.concatenate(parts, axis=0).astype(BF16)
    acc_sc[...] = jnp.zeros_like(acc_sc)

    def chunk(c, stats, masked):
        m_old, l_old = stats
        start = pl.multiple_of(c * tk, tk)
        k = k_ref[pl.ds(start, tk), :].astype(BF16)
        v = v_ref[pl.ds(start, tk), :].astype(BF16)
        st = _nt(k, q4)
        if masked:
            key = start + lax.broadcasted_iota(I32, (tk, n4), 0)
            t = i * tq + jnp.bitwise_and(lax.broadcasted_iota(I32, (tk, n4), 1), tq - 1)
            st = jnp.where(key <= t, st, -jnp.inf)
        m_new = jnp.maximum(m_old, jnp.max(st, axis=0, keepdims=True))
        alpha = jnp.exp2(m_old - m_new)
        p = jnp.exp2(st - m_new)
        pv = lax.dot_general(v, p.astype(BF16), (((0,), (0,)), ((), ())), preferred_element_type=F32)
        acc_sc[...] = alpha * acc_sc[...] + pv
        return m_new, alpha * l_old + jnp.sum(p, axis=0, keepdims=True)

    last = (i * tq) // tk
    stats = (jnp.full((1, n4), -jnp.inf, F32), jnp.zeros((1, n4), F32))
    stats = lax.fori_loop(0, last, lambda c, s: chunk(c, s, False), stats)
    _, l = chunk(last, stats, True)
    ot = acc_sc[...] / l
    for g in range(2):
        og = ot[:, (2 * g) * tq:(2 * g + 1) * tq] - lam * ot[:, (2 * g + 1) * tq:(2 * g + 2) * tq]
        ms = jnp.mean(og * og, axis=0, keepdims=True)
        nrm = (og * lax.rsqrt(ms + EPS)) * sub_ref[...] * (1.0 - lam_init)
        o_ref[:, g * LANES:(g + 1) * LANES] = nrm.T


def _attn_prompt(u, lams, subln, bsz, seq, lam_init):
    tq = _tile(seq, (256, 128))
    tk = 2 * tq if seq % (2 * tq) == 0 else tq
    nq = seq // tq
    vec = pl.BlockSpec((1, ATTN_DH), lambda b, h, i: (0, 0))
    return pl.pallas_call(
        functools.partial(_attn_prompt_body, tq=tq, tk=tk, lam_init=lam_init),
        grid=(bsz, ATTN_KV_HEADS, nq),
        in_specs=[vec, vec, vec, vec,
                  pl.BlockSpec((ATTN_VD, 1), lambda b, h, i: (0, 0)),
                  pl.BlockSpec((tq, 2 * LANES), lambda b, h, i: (b * nq + i, h)),
                  pl.BlockSpec((seq, LANES), lambda b, h, i: (b, K0 // LANES + h)),
                  pl.BlockSpec((seq, LANES), lambda b, h, i: (b, V0 // LANES + h))],
        out_specs=pl.BlockSpec((tq, 2 * LANES), lambda b, h, i: (b * nq + i, h)),
        out_shape=jax.ShapeDtypeStruct((bsz * seq, ATTN_HEADS * ATTN_VD), F32),
        scratch_shapes=[pltpu.VMEM((ATTN_VD, 4 * tq), F32)],
        compiler_params=_cparams(("arbitrary", "arbitrary", "arbitrary"), 40),
        name="attn_prompt",
    )(*lams, subln.reshape(ATTN_VD, 1), u, u, u)


def _attn_sample_body(pt_ref, lq1, lk1, lq2, lk2, sub_ref, q_ref, kn_ref, vn_ref, *rest, n_pg, lam_init, tdec):
    del pt_ref
    k_refs, v_refs = rest[:n_pg], rest[n_pg:2 * n_pg]
    o_ref, m_sc, l_sc, acc_sc, qr_sc, bias_sc = rest[2 * n_pg:]
    j = pl.program_id(1)
    nrow = ATTN_KV_HEADS * 2 * 2 * tdec
    per_kv = nrow // ATTN_KV_HEADS
    ncol = bias_sc.shape[1]

    @pl.when(j == 0)
    def _():
        q = q_ref[...] * (ATTN_DH ** -0.5)
        lane = lax.broadcasted_iota(I32, (tdec, LANES), 1)
        for kv in range(ATTN_KV_HEADS):
            for c in range(2):
                for g in range(2):
                    hd = kv * 2 + g
                    qh = q[:, hd * LANES:(hd + 1) * LANES]
                    n0 = ((kv * 2 + c) * 2 + g) * tdec
                    qr_sc[n0:n0 + tdec, :] = jnp.where((lane >= c * ATTN_DH) & (lane < (c + 1) * ATTN_DH), qh, 0.0)
        r = lax.broadcasted_iota(I32, (nrow, ncol), 0) // per_kv
        c = jnp.bitwise_and(lax.broadcasted_iota(I32, (nrow, ncol), 1), ATTN_KV_HEADS - 1)
        bias_sc[...] = jnp.where(r == c, 0.0, -jnp.inf)
        m_sc[...] = jnp.full_like(m_sc, -jnp.inf)
        l_sc[...] = jnp.zeros_like(l_sc)
        acc_sc[...] = jnp.zeros_like(acc_sc)

    qr = qr_sc[...].astype(BF16)

    def update(s_list, v_list):
        m_old = m_sc[...]
        m_new = m_old
        for s in s_list:
            m_new = jnp.maximum(m_new, jnp.max(s, axis=1, keepdims=True))
        alpha = jnp.exp(m_old - m_new)
        l = alpha * l_sc[...]
        pv = None
        for s, v in zip(s_list, v_list):
            p = jnp.exp(s - m_new)
            l = l + jnp.sum(p, axis=1, keepdims=True)
            d = _mm(p.astype(BF16), v)
            pv = d if pv is None else pv + d
        m_sc[...] = m_new
        l_sc[...] = l
        acc_sc[...] = alpha * acc_sc[...] + pv

    for g0 in range(0, n_pg, PAGE_GROUP):
        grp = range(g0, min(g0 + PAGE_GROUP, n_pg))
        update([_nt(qr, k_refs[i][...].astype(BF16)) + bias_sc[...] for i in grp],
               [v_refs[i][...].astype(BF16) for i in grp])

    @pl.when(j == pl.num_programs(1) - 1)
    def _():
        lam = _lambda(lq1, lk1, lq2, lk2, lam_init)
        pad = jnp.zeros((LANES - ATTN_KV_HEADS * tdec, LANES), F32)
        kn = jnp.concatenate([kn_ref[:, kv * LANES:(kv + 1) * LANES] for kv in range(ATTN_KV_HEADS)] + [pad], axis=0)
        vn = jnp.concatenate([vn_ref[:, kv * LANES:(kv + 1) * LANES] for kv in range(ATTN_KV_HEADS)] + [pad], axis=0)
        s = _nt(qr, kn.astype(BF16))
        row = lax.broadcasted_iota(I32, (nrow, LANES), 0)
        col = lax.broadcasted_iota(I32, (nrow, LANES), 1)
        ok = (col // tdec == row // per_kv) & (jnp.bitwise_and(col, tdec - 1) <= jnp.bitwise_and(row, tdec - 1))
        update([jnp.where(ok, s, -jnp.inf)], [vn.astype(BF16)])
        o = acc_sc[...] / l_sc[...]
        for kv in range(ATTN_KV_HEADS):
            for g in range(2):
                n0 = ((kv * 2 + 0) * 2 + g) * tdec
                n1 = ((kv * 2 + 1) * 2 + g) * tdec
                og = o[n0:n0 + tdec, :] - lam * o[n1:n1 + tdec, :]
                hd = kv * 2 + g
                o_ref[:, hd * LANES:(hd + 1) * LANES] = _rms(og, sub_ref[...]) * (1.0 - lam_init)


def _attn_sample(u, cache_k, cache_v, page_flat, layer, n_pool, lams, subln, bsz, tdec, n_pages, lam_init):
    del n_pool
    n_pg = _tile(n_pages, (16, 8, 4, 2, 1))
    nchunks = n_pages // n_pg
    prow = cache_k.shape[2]
    width = ATTN_KV_HEADS * LANES
    nrow = ATTN_KV_HEADS * 2 * 2 * tdec
    assert nrow == LANES and tdec == SUBLANES
    vec = pl.BlockSpec((1, ATTN_DH), lambda b, j, pt: (0, 0))

    def page_spec(i):
        return pl.BlockSpec((None, None, prow, ATTN_VD),
                            lambda b, j, pt: (layer, pt[b * n_pages + j * n_pg + i], 0, 0))

    in_specs = [vec, vec, vec, vec,
                pl.BlockSpec((1, ATTN_VD), lambda b, j, pt: (0, 0)),
                pl.BlockSpec((tdec, ATTN_HEADS * ATTN_VD), lambda b, j, pt: (b, 0)),
                pl.BlockSpec((tdec, width), lambda b, j, pt: (b, K0 // width)),
                pl.BlockSpec((tdec, width), lambda b, j, pt: (b, V0 // width))]
    in_specs += [page_spec(i) for i in range(n_pg)] * 2
    return pl.pallas_call(
        functools.partial(_attn_sample_body, n_pg=n_pg, lam_init=lam_init, tdec=tdec),
        grid_spec=pltpu.PrefetchScalarGridSpec(
            num_scalar_prefetch=1,
            grid=(bsz, nchunks),
            in_specs=in_specs,
            out_specs=pl.BlockSpec((tdec, ATTN_HEADS * ATTN_VD), lambda b, j, pt: (b, 0)),
            scratch_shapes=[pltpu.VMEM((nrow, 1), F32), pltpu.VMEM((nrow, 1), F32),
                            pltpu.VMEM((nrow, ATTN_VD), F32), pltpu.VMEM((nrow, LANES), F32),
                            pltpu.VMEM((nrow, prow), F32)]),
        out_shape=jax.ShapeDtypeStruct((bsz * tdec, ATTN_HEADS * ATTN_VD), F32),
        compiler_params=_cparams(("arbitrary", "arbitrary"), 48),
        name="attn_sample",
    )(page_flat, *lams, subln, u, u, u, *([cache_k] * n_pg), *([cache_v] * n_pg))


def _causal_conv(x, prev, cw_ref, cb_ref):
    taps = cw_ref.shape[0]
    rowi = lax.broadcasted_iota(I32, x.shape, 0)
    acc = cb_ref[...] + x * cw_ref[taps - 1:taps, :]
    for s in range(1, taps):
        sh = jnp.where(rowi < s, pltpu.roll(prev, s, 0), pltpu.roll(x, s, 0))
        acc = acc + sh * cw_ref[taps - 1 - s:taps - s, :]
    return acc


def _pad_rows(x, rows):
    if x.shape[0] == rows:
        return x
    return jnp.concatenate([x, jnp.zeros((rows - x.shape[0], x.shape[1]), x.dtype)], axis=0)


def _ssd_body(xbc_ref, z_ref, dt_ref, st_ref, h0_ref, cw_ref, cb_ref, dtb_ref, alog_ref, dvec_ref, nw_ref,
              y_ref, hout_ref, prev_sc, h_sc, *, rows):
    c = pl.program_id(1)
    L = SSD_CHUNK
    width = SSD_HEADS * SSD_HEAD_DIM
    gw = SSD_GROUPS * SSD_STATE

    @pl.when(c == 0)
    def _():
        prev_sc[...] = jnp.concatenate([jnp.zeros((L - SUBLANES, prev_sc.shape[1]), F32), st_ref[...]], axis=0)
        h_sc[...] = h0_ref[...]

    x = _pad_rows(xbc_ref[...], L)
    conv = _causal_conv(x, prev_sc[...], cw_ref, cb_ref)
    prev_sc[...] = x
    xc = conv * jax.nn.sigmoid(conv)
    xs, bm, cm = xc[:, :width], xc[:, width:width + gw], xc[:, width + gw:]

    dt = jax.nn.softplus(_pad_rows(dt_ref[...], L) + dtb_ref[...])
    if rows < L:
        dt = jnp.where(lax.broadcasted_iota(I32, dt.shape, 0) < rows, dt, 0.0)
    da = dt * (-jnp.exp(alog_ref[...]))
    r0 = lax.broadcasted_iota(I32, (L, L), 0)
    c0 = lax.broadcasted_iota(I32, (L, L), 1)
    causal = r0 >= c0
    acum = jnp.dot(causal.astype(F32), da, precision=lax.Precision.HIGHEST, preferred_element_type=F32)
    acum_t, dt_t = acum.T, dt.T
    last = acum[L - 1:L, :]
    wend = jnp.exp(last - acum) * dt
    eac = jnp.exp(acum)
    elast = jnp.exp(last)
    lane = lax.broadcasted_iota(I32, (L, LANES), 1)
    first = lane < SSD_HEAD_DIM
    top = lax.broadcasted_iota(I32, (2 * SSD_HEAD_DIM, SSD_STATE), 0) < SSD_HEAD_DIM

    ys = []
    for pr in range(SSD_HEADS // 2):
        g = (2 * pr) // (SSD_HEADS // SSD_GROUPS)
        bg = bm[:, g * SSD_STATE:(g + 1) * SSD_STATE].astype(BF16)
        cg = cm[:, g * SSD_STATE:(g + 1) * SSD_STATE].astype(BF16)
        gmat = _nt(cg, bg)
        xp = xs[:, pr * LANES:(pr + 1) * LANES]
        xpb = xp.astype(BF16)
        hp = h_sc[pr * LANES:(pr + 1) * LANES, :]
        outs = []
        for hh in (2 * pr, 2 * pr + 1):
            seg = acum[:, hh:hh + 1] - acum_t[hh:hh + 1, :]
            dec = jnp.exp(jnp.where(causal, seg, -jnp.inf))
            w = gmat * dec * dt_t[hh:hh + 1, :]
            outs.append(_mm(w.astype(BF16), xpb))
        h_a, h_b = 2 * pr, 2 * pr + 1
        y_intra = jnp.where(first, outs[0], outs[1])
        e_pair = jnp.where(first, eac[:, h_a:h_a + 1], eac[:, h_b:h_b + 1])
        ys.append(y_intra + _nt(cg, hp.astype(BF16)) * e_pair)
        w_pair = jnp.where(first, wend[:, h_a:h_a + 1], wend[:, h_b:h_b + 1])
        upd = _mm((xp * w_pair).T.astype(BF16), bg)
        keep = jnp.where(top, elast[:, h_a:h_a + 1], elast[:, h_b:h_b + 1])
        h_sc[pr * LANES:(pr + 1) * LANES, :] = hp * keep + upd

    y = jnp.concatenate(ys, axis=1) + dvec_ref[...] * xs
    zz = _pad_rows(z_ref[...], L)
    gated = y * (zz * jax.nn.sigmoid(zz))
    y_ref[...] = _rms(gated, nw_ref[...])[:rows]

    @pl.when(c == pl.num_programs(1) - 1)
    def _():
        hout_ref[...] = h_sc[...]


def _ssd(u, st8, h0, cw, cb, dtb, alog, dvec, nw, bsz, seq):
    rows = SSD_CHUNK if seq % SSD_CHUNK == 0 else seq
    assert rows == SSD_CHUNK or (rows == seq and rows % SUBLANES == 0 and rows <= SSD_CHUNK)
    nch = seq // rows
    cc = cw.shape[1]
    width = SSD_HEADS * SSD_HEAD_DIM
    hp = SSD_HEADS * SSD_HEAD_DIM
    const = lambda shape: pl.BlockSpec(shape, lambda b, c: (0,) * len(shape))
    return pl.pallas_call(
        functools.partial(_ssd_body, rows=rows),
        grid=(bsz, nch),
        in_specs=[pl.BlockSpec((rows, cc), lambda b, c: (b * nch + c, XBC0 // cc)),
                  pl.BlockSpec((rows, width), lambda b, c: (b * nch + c, Z0 // width)),
                  pl.BlockSpec((rows, LANES), lambda b, c: (b * nch + c, DT0 // LANES)),
                  pl.BlockSpec((None, SUBLANES, cc), lambda b, c: (b, 0, 0)),
                  pl.BlockSpec((None, hp, SSD_STATE), lambda b, c: (b, 0, 0)),
                  const((SSD_CONV, cc)), const((1, cc)), const((1, LANES)), const((1, LANES)),
                  const((1, width)), const((1, width))],
        out_specs=[pl.BlockSpec((rows, width), lambda b, c: (b * nch + c, 0)),
                   pl.BlockSpec((None, hp, SSD_STATE), lambda b, c: (b, 0, 0))],
        out_shape=[jax.ShapeDtypeStruct((bsz * seq, width), F32),
                   jax.ShapeDtypeStruct((bsz, hp, SSD_STATE), F32)],
        scratch_shapes=[pltpu.VMEM((SSD_CHUNK, cc), F32), pltpu.VMEM((hp, SSD_STATE), F32)],
        compiler_params=_cparams(("arbitrary", "arbitrary"), 32),
        name="ssd",
    )(u, u, u, st8, h0, cw, cb, dtb, alog, dvec, nw)


def _expm1(t):
    u = jnp.exp(t)
    small = (u - 1.0) * t / jnp.log(u)
    return jnp.where(t < -1.0, u - 1.0, jnp.where(u == 1.0, t, small))


def _lru_body(xr_ref, gate_ref, st_ref, h0_ref, cw_ref, cb_ref, wa_ref, ba_ref, wx_ref, bx_ref, lam_ref,
              y_ref, hout_ref, prev_sc, h_sc, *, rows):
    c = pl.program_id(1)

    @pl.when(c == 0)
    def _():
        st = st_ref[...]
        if rows > SUBLANES:
            st = jnp.concatenate([jnp.zeros((rows - SUBLANES, st.shape[1]), F32), st], axis=0)
        prev_sc[...] = st
        h_sc[...] = h0_ref[...]

    x = xr_ref[...]
    xc = _causal_conv(x, prev_sc[...], cw_ref, cb_ref)
    prev_sc[...] = x
    xcb = xc.astype(BF16)
    r = jax.nn.sigmoid(_mm(xcb, wa_ref[...]) + ba_ref[...])
    ig = jax.nn.sigmoid(_mm(xcb, wx_ref[...]) + bx_ref[...])
    log_a = (-LRU_C) * r * jax.nn.softplus(-lam_ref[...])
    a = jnp.exp(log_a)
    b = jnp.sqrt(-_expm1(2.0 * log_a)) * (ig * xc)
    rowi = lax.broadcasted_iota(I32, a.shape, 0)
    d = 1
    while d < rows:
        ok = rowi >= d
        b = jnp.where(ok, a * pltpu.roll(b, d, 0) + b, b)
        a = jnp.where(ok, a * pltpu.roll(a, d, 0), a)
        d *= 2
    h = b + a * h_sc[...]
    h_sc[...] = h[rows - 1:rows, :]
    y_ref[...] = h * jax.nn.gelu(gate_ref[...])

    @pl.when(c == pl.num_programs(1) - 1)
    def _():
        hout_ref[...] = h[rows - 1:rows, :]


def _lru(u, st8, h0, cw, cb, wa, ba, wx, bx, lam, bsz, seq):
    rows = _tile(seq, (128, 64, 32, 16, 8))
    nch = seq // rows
    w = cw.shape[1]
    const = lambda shape: pl.BlockSpec(shape, lambda b, c: (0,) * len(shape))
    return pl.pallas_call(
        functools.partial(_lru_body, rows=rows),
        grid=(bsz, nch),
        in_specs=[pl.BlockSpec((rows, w), lambda b, c: (b * nch + c, XR0 // w)),
                  pl.BlockSpec((rows, w), lambda b, c: (b * nch + c, GATE0 // w)),
                  pl.BlockSpec((None, SUBLANES, w), lambda b, c: (b, 0, 0)),
                  pl.BlockSpec((None, 1, w), lambda b, c: (b, 0, 0)),
                  const((LRU_CONV, w)), const((1, w)), const((w, w)), const((1, w)), const((w, w)), const((1, w)),
                  const((1, w))],
        out_specs=[pl.BlockSpec((rows, w), lambda b, c: (b * nch + c, 0)),
                   pl.BlockSpec((None, 1, w), lambda b, c: (b, 0, 0))],
        out_shape=[jax.ShapeDtypeStruct((bsz * seq, w), F32), jax.ShapeDtypeStruct((bsz, 1, w), F32)],
        scratch_shapes=[pltpu.VMEM((rows, w), F32), pltpu.VMEM((1, w), F32)],
        compiler_params=_cparams(("arbitrary", "arbitrary"), 32),
        name="lru",
    )(u, u, st8, h0, cw, cb, wa, ba, wx, bx, lam)


def _out_proj_body(x_ref, a_ref, s_ref, l_ref, wo_ref, nw_ref, wq_ref, x1_ref, qm_ref):
    wa, ws = a_ref.shape[1], s_ref.shape[1]
    acc = x_ref[...] + _mm(a_ref[...].astype(BF16), wo_ref[0:wa, :])
    acc = acc + _mm(s_ref[...].astype(BF16), wo_ref[wa:wa + ws, :])
    acc = acc + _mm(l_ref[...].astype(BF16), wo_ref[wa + ws:, :])
    x1_ref[...] = acc
    qm_ref[...] = _mm(_rms(acc, nw_ref[...]).astype(BF16), wq_ref[...])


def _out_proj(x, attn, ssd, lru, wo, nw, wq):
    t, d = x.shape
    tm = _tile(t, (256, 128))
    row = lambda w: pl.BlockSpec((tm, w), lambda i: (i, 0))
    const = lambda shape: pl.BlockSpec(shape, lambda i: (0, 0))
    return pl.pallas_call(
        _out_proj_body,
        grid=(t // tm,),
        in_specs=[row(d), row(attn.shape[1]), row(ssd.shape[1]), row(lru.shape[1]),
                  const(wo.shape), const((1, d)), const(wq.shape)],
        out_specs=[row(d), row(wq.shape[1])],
        out_shape=[jax.ShapeDtypeStruct((t, d), F32), jax.ShapeDtypeStruct((t, wq.shape[1]), F32)],
        compiler_params=_cparams(("arbitrary",), 48),
        name="out_proj",
    )(x, attn, ssd, lru, wo, nw, wq)


def _mem_kv_body(m_ref, wk_ref, wv_ref, k_ref, v_ref):
    mb = m_ref[...].astype(BF16)
    k_ref[...] = _mm(mb, wk_ref[...])
    v_ref[...] = _mm(mb, wv_ref[...])


def _mem_kv(mem, wk, wv):
    t, d = mem.shape
    tm = _tile(t, (256, 128))
    w = wk.shape[1]
    return pl.pallas_call(
        _mem_kv_body,
        grid=(t // tm,),
        in_specs=[pl.BlockSpec((tm, d), lambda i: (i, 0)),
                  pl.BlockSpec((d, w), lambda i: (0, 0)), pl.BlockSpec((d, w), lambda i: (0, 0))],
        out_specs=[pl.BlockSpec((tm, w), lambda i: (i, 0)), pl.BlockSpec((tm, w), lambda i: (i, 0))],
        out_shape=[jax.ShapeDtypeStruct((t, w), F32), jax.ShapeDtypeStruct((t, w), F32)],
        compiler_params=_cparams(("arbitrary",), 32),
        name="mem_kv",
    )(mem, wk, wv)


def _xattn_body(q_ref, k_ref, v_ref, o_ref):
    for h in range(MEM_HEADS):
        sl = slice(h * MEM_DH, (h + 1) * MEM_DH)
        s = _nt(q_ref[:, sl].astype(BF16), k_ref[:, sl].astype(BF16)) * (MEM_DH ** -0.5)
        e = jnp.exp(s - jnp.max(s, axis=1, keepdims=True))
        o = _mm(e.astype(BF16), v_ref[:, sl].astype(BF16))
        o_ref[:, sl] = o / jnp.sum(e, axis=1, keepdims=True)


def _xattn(qm, mem_k, mem_v, bsz, seq):
    tq = _tile(seq, (256, 128))
    nq = seq // tq
    w = qm.shape[1]
    m = mem_k.shape[1]
    return pl.pallas_call(
        _xattn_body,
        grid=(bsz, nq),
        in_specs=[pl.BlockSpec((tq, w), lambda b, i: (b * nq + i, 0)),
                  pl.BlockSpec((None, m, w), lambda b, i: (b, 0, 0)),
                  pl.BlockSpec((None, m, w), lambda b, i: (b, 0, 0))],
        out_specs=pl.BlockSpec((tq, w), lambda b, i: (b * nq + i, 0)),
        out_shape=jax.ShapeDtypeStruct(qm.shape, F32),
        compiler_params=_cparams(("arbitrary", "arbitrary"), 32),
        name="xattn",
    )(qm, mem_k, mem_v)


def _route_body(x1_ref, om_ref, wo_ref, nw_ref, rw_ref, rb_ref, x2_ref, hf_ref, eid_ref, gate_ref):
    x2 = x1_ref[...] + _mm(om_ref[...].astype(BF16), wo_ref[...])
    x2_ref[...] = x2
    hf = _rms(x2, nw_ref[...])
    hf_ref[...] = hf
    logits = jnp.dot(hf, rw_ref[...], precision=lax.Precision.HIGHEST, preferred_element_type=F32) + rb_ref[...]
    lane = lax.broadcasted_iota(I32, logits.shape, 1).astype(F32)
    ninf = -jnp.inf
    big = float(LANES)

    def first_argmax(v, mx):
        return jnp.min(jnp.where(v == mx, lane, big), axis=1, keepdims=True)

    lg = jnp.where(lane < MOE_GROUPS, logits, ninf)
    mg = jnp.max(lg, axis=1, keepdims=True)
    p_grp = 1.0 / jnp.sum(jnp.exp(lg - mg), axis=1, keepdims=True)
    lo = MOE_GROUPS + MOE_EXPERTS_PER_GROUP * first_argmax(lg, mg)
    le = jnp.where((lane >= lo) & (lane < lo + MOE_EXPERTS_PER_GROUP), logits, ninf)
    v1 = jnp.max(le, axis=1, keepdims=True)
    i1 = first_argmax(le, v1)
    le2 = jnp.where(lane == i1, ninf, le)
    v2 = jnp.max(le2, axis=1, keepdims=True)
    i2 = first_argmax(le2, v2)
    e2 = jnp.exp(v2 - v1)
    den = 1.0 + e2
    g1 = p_grp * (1.0 / den)
    g2 = p_grp * (e2 / den)
    gate_ref[...] = jnp.where(lane == 0.0, g1, jnp.where(lane == 1.0, g2, 0.0))
    eid_ref[...] = jnp.where(lane == 0.0, i1 - MOE_GROUPS, jnp.where(lane == 1.0, i2 - MOE_GROUPS, 0.0)).astype(I32)


def _route(x1, om, wo, nw, rw, rb):
    t, d = x1.shape
    tm = _tile(t, (256, 128))
    row = lambda w: pl.BlockSpec((tm, w), lambda i: (i, 0))
    const = lambda shape: pl.BlockSpec(shape, lambda i: (0, 0))
    return pl.pallas_call(
        _route_body,
        grid=(t // tm,),
        in_specs=[row(d), row(om.shape[1]), const(wo.shape), const((1, d)), const(rw.shape), const((1, LANES))],
        out_specs=[row(d), row(d), row(LANES), row(LANES)],
        out_shape=[jax.ShapeDtypeStruct((t, d), F32), jax.ShapeDtypeStruct((t, d), F32),
                   jax.ShapeDtypeStruct((t, LANES), I32), jax.ShapeDtypeStruct((t, LANES), F32)],
        compiler_params=_cparams(("arbitrary",), 40),
        name="route",
    )(x1, om, wo, nw, rw, rb)


ROW_UNROLL = 8


def _moe_body(grow_ref, srow_ref, be_ref, base_ref, cnt_ref, hf_hbm, wg_ref, wu_ref, wd_ref, o_hbm,
              xbuf, ybuf, wgb, wub, wdb, gsem, ssem, *, rb):
    g = pl.program_id(0)
    ng = pl.num_programs(0)
    slot = lax.rem(g, 2)
    cnt = cnt_ref[g]

    def for_rows(n, fn):
        n_grp = lax.shift_right_logical(n, ROW_UNROLL.bit_length() - 1)

        def grp(q, c):
            for u in range(ROW_UNROLL):
                fn(q * ROW_UNROLL + u)
            return c

        def one(i, c):
            fn(i)
            return c

        lax.fori_loop(0, n_grp, grp, 0)
        lax.fori_loop(n_grp * ROW_UNROLL, n, one, 0)

    def gather_row(sl, i, src):
        return pltpu.make_async_copy(hf_hbm.at[pl.ds(src, 1), :], xbuf.at[sl, pl.ds(i, 1), :], gsem.at[sl])

    def scatter_row(sl, i, dst):
        return pltpu.make_async_copy(ybuf.at[sl, pl.ds(i, 1), :], o_hbm.at[pl.ds(dst, 1), :], ssem.at[sl])

    def start_gathers(blk, sl):
        base = base_ref[blk]
        for_rows(cnt_ref[blk], lambda i: gather_row(sl, i, grow_ref[base + i]).start())

    def start_scatters(blk, sl):
        base = base_ref[blk]
        for_rows(cnt_ref[blk], lambda i: scatter_row(sl, i, srow_ref[base + i]).start())

    def wait_rows(blk, sl, row_copy, block_copy):
        n = cnt_ref[blk]

        @pl.when(n == rb)
        def _():
            block_copy.wait()

        @pl.when(n < rb)
        def _():
            for_rows(n, lambda i: row_copy(sl, i, 0).wait())

    def wait_gathers(blk, sl):
        wait_rows(blk, sl, gather_row,
                  pltpu.make_async_copy(hf_hbm.at[pl.ds(0, rb), :], xbuf.at[sl], gsem.at[sl]))

    def wait_scatters(blk, sl):
        wait_rows(blk, sl, scatter_row,
                  pltpu.make_async_copy(ybuf.at[sl], o_hbm.at[pl.ds(0, rb), :], ssem.at[sl]))

    @pl.when(g == 0)
    def _():
        xbuf[...] = jnp.zeros_like(xbuf)
        start_gathers(0, 0)

    prev_e = be_ref[jnp.maximum(g - 1, 0)]

    @pl.when((cnt > 0) & ((g == 0) | (prev_e != be_ref[g])))
    def _():
        wgb[...] = wg_ref[...].astype(BF16)
        wub[...] = wu_ref[...].astype(BF16)
        wdb[...] = wd_ref[...].astype(BF16)

    wait_gathers(g, slot)

    @pl.when(g + 1 < ng)
    def _():
        start_gathers(g + 1, 1 - slot)

    @pl.when(g >= 2)
    def _():
        wait_scatters(g - 2, slot)

    @pl.when(cnt > 0)
    def _():
        xb = xbuf[slot].astype(BF16)
        hg = _mm(xb, wgb[...])
        act = (hg * jax.nn.sigmoid(hg)) * _mm(xb, wub[...])
        ybuf[slot] = _mm(act.astype(BF16), wdb[...])
        start_scatters(g, slot)

    @pl.when(g == ng - 1)
    def _():
        @pl.when(g >= 1)
        def _():
            wait_scatters(g - 1, 1 - slot)

        wait_scatters(g, slot)


def _moe_ffn(hf, grow, srow, blk_e, blk_base, blk_cnt, wg, wu, wd, layer, rb):
    t, d = hf.shape
    n_asg = grow.shape[0]
    ff = wg.shape[-1]
    n_blk = blk_e.shape[0]
    wspec = lambda shape: pl.BlockSpec((None, None) + shape, lambda g, gr, sr, be, bb, bc: (layer, be[g], 0, 0))
    return pl.pallas_call(
        functools.partial(_moe_body, rb=rb),
        grid_spec=pltpu.PrefetchScalarGridSpec(
            num_scalar_prefetch=5,
            grid=(n_blk,),
            in_specs=[pl.BlockSpec(memory_space=pl.ANY), wspec((d, ff)), wspec((d, ff)), wspec((ff, d))],
            out_specs=pl.BlockSpec(memory_space=pl.ANY),
            scratch_shapes=[pltpu.VMEM((2, rb, d), F32), pltpu.VMEM((2, rb, d), F32),
                            pltpu.VMEM((d, ff), BF16), pltpu.VMEM((d, ff), BF16), pltpu.VMEM((ff, d), BF16),
                            pltpu.SemaphoreType.DMA((2,)), pltpu.SemaphoreType.DMA((2,))]),
        out_shape=jax.ShapeDtypeStruct((n_asg, d), F32),
        compiler_params=_cparams(("arbitrary",), 56),
        name="moe_ffn",
    )(grow, srow, blk_e, blk_base, blk_cnt, hf, wg, wu, wd)


def _moe_plan(eid, n_tok, rb):
    n_asg = n_tok * MOE_TOPK
    flat_e = eid.reshape(-1)
    order = jnp.argsort(flat_e, stable=True).astype(I32)
    grow = order // MOE_TOPK
    srow = (order % MOE_TOPK) * n_tok + grow
    counts = jnp.sum((flat_e[:, None] == jnp.arange(MOE_EXPERTS, dtype=I32)[None, :]).astype(I32), axis=0)
    starts = jnp.cumsum(counts) - counts
    nblk = (counts + rb - 1) // rb
    blk_end = jnp.cumsum(nblk)
    n_blk = -(-n_asg // rb) + MOE_EXPERTS
    gi = jnp.arange(n_blk, dtype=I32)
    be = jnp.minimum(jnp.sum((gi[:, None] >= blk_end[None, :]).astype(I32), axis=1), MOE_EXPERTS - 1)
    r = gi - (blk_end - nblk)[be]
    cnt = jnp.clip(counts[be] - r * rb, 0, rb).astype(I32)
    base = jnp.minimum(starts[be] + r * rb, n_asg - 1).astype(I32)
    return grow.astype(I32), srow.astype(I32), be, base, cnt


def _combine_body(x2_ref, o0_ref, o1_ref, g_ref, fw_ref, x3_ref, *, final):
    g = g_ref[...]
    x3 = x2_ref[...] + (g[:, 0:1] * o0_ref[...] + g[:, 1:2] * o1_ref[...])
    x3_ref[...] = _rms(x3, fw_ref[...]) if final else x3


def _combine(x2, o2, gates, fw, final, row0, n_all):
    t, d = x2.shape
    tm = _tile(math.gcd(t, row0, n_all), (256, 128, 64, 32, 16, 8))
    assert t % tm == 0 and row0 % tm == 0 and n_all % tm == 0
    nt = t // tm
    b0, b1 = row0 // tm, (n_all + row0) // tm
    return pl.pallas_call(
        functools.partial(_combine_body, final=final),
        grid=(nt,),
        in_specs=[pl.BlockSpec((tm, d), lambda i: (i, 0)),
                  pl.BlockSpec((tm, d), lambda i: (b0 + i, 0)),
                  pl.BlockSpec((tm, d), lambda i: (b1 + i, 0)),
                  pl.BlockSpec((tm, LANES), lambda i: (i, 0)),
                  pl.BlockSpec((1, d), lambda i: (0, 0))],
        out_specs=pl.BlockSpec((tm, d), lambda i: (i, 0)),
        out_shape=jax.ShapeDtypeStruct((t, d), F32),
        compiler_params=_cparams(("arbitrary",), 32),
        name="combine",
    )(x2, o2, o2, gates, fw)


def _rope_tables(pos):
    half = ATTN_DH // 2
    inv = ROPE_THETA ** (-jnp.arange(half, dtype=F32) / half)
    ang = pos.astype(F32)[:, None] * inv[None, :]
    cos, sin = jnp.cos(ang), jnp.sin(ang)
    reps = LANES // ATTN_DH
    return jnp.tile(jnp.concatenate([cos, cos], axis=1), (1, reps)), jnp.tile(jnp.concatenate([-sin, sin], axis=1), (1, reps))


def _block_diag(w):
    g, a, b = w.shape
    eye = jnp.eye(g, dtype=w.dtype)
    return (w[:, :, None, :] * eye[:, None, :, None]).reshape(g * a, g * b)


def _pad_lanes(v, n):
    return jnp.pad(v.reshape(1, -1), ((0, 0), (0, n - v.size)))


def _layer_front(x, grp, lw, layer, lam_init):
    bsz, seq = grp["bsz"], grp["seq"]
    u = _in_proj(x, lw["norm_mix"], lw["w_in"], grp["cos"], grp["sin"])
    lams = lw["lams"]
    if grp["paged"]:
        attn = _attn_sample(u, grp["cache_k"], grp["cache_v"], grp["page_flat"], layer, grp["n_pool"], lams,
                            lw["subln"], bsz, seq, grp["n_pages"], lam_init)
    else:
        attn = _attn_prompt(u, lams, lw["subln"], bsz, seq, lam_init)
    ssd, h_ssd = _ssd(u, grp["ssd_conv"][layer], grp["ssd_state"][layer], lw["ssd_conv_w"], lw["ssd_conv_b"],
                      lw["ssd_dt_bias"], lw["ssd_a_log"], lw["ssd_d"], lw["ssd_norm"], bsz, seq)
    lru, h_lru = _lru(u, grp["lru_conv"][layer], grp["lru_state"][layer], lw["lru_conv_w"], lw["lru_conv_b"],
                      lw["lru_wa"], lw["lru_ba"], lw["lru_wx"], lw["lru_bx"], lw["lru_lambda"], bsz, seq)
    x1, qm = _out_proj(x, attn, ssd, lru, lw["w_out"], lw["norm_mem"], lw["wq_mem"])
    om = _xattn(qm, grp["mem_k"][layer], grp["mem_v"][layer], bsz, seq)
    routed = _route(x1, om, lw["wo_mem"], lw["norm_ffn"], lw["router_w"], lw["router_b"])
    ur = u.reshape(bsz, seq, UW)
    width_kv = ATTN_KV_HEADS * ATTN_VD
    outs = dict(
        k=ur[:, :, K0:K0 + width_kv].reshape(bsz, seq, ATTN_KV_HEADS, ATTN_VD),
        v=ur[:, :, V0:V0 + width_kv].reshape(bsz, seq, ATTN_KV_HEADS, ATTN_VD),
        ssd_conv=ur[:, seq - (SSD_CONV - 1):, XBC0:Z0],
        ssd_state=h_ssd.reshape(bsz, SSD_HEADS, SSD_HEAD_DIM, SSD_STATE),
        lru_conv=ur[:, seq - (LRU_CONV - 1):, XR0:GATE0],
        lru_state=h_lru.reshape(bsz, -1),
    )
    return routed, outs


def _moe_groups(routed, lw, layer, final, final_norm):
    hf = jnp.concatenate([r[1] for r in routed], axis=0)
    eid = jnp.concatenate([r[2][:, :MOE_TOPK] for r in routed], axis=0)
    n_tok = hf.shape[0]
    rb = 256 if n_tok * MOE_TOPK >= 256 * MOE_EXPERTS else 128
    plan = _moe_plan(eid, n_tok, rb)
    o2 = _moe_ffn(hf, *plan, lw["moe_w_gate"], lw["moe_w_up"], lw["moe_w_down"], layer, rb)
    outs, row0 = [], 0
    for x2, _, _, gate in routed:
        outs.append(_combine(x2, o2, gate, final_norm, final, row0, n_tok))
        row0 += x2.shape[0]
    return outs


def _conv_state8(st):
    return jnp.pad(st, ((0, 0), (0, 0), (SUBLANES - st.shape[2], 0), (0, 0)))


def kernel(x_prompt, x_sample, cache_k, cache_v, cache_mem_k, cache_mem_v, state_ssd_conv, state_ssd, state_lru_conv, state_lru, page_table, mem_prompt, norm_mix, w_in, attn_lambda_q1, attn_lambda_k1, attn_lambda_q2, attn_lambda_k2, attn_subln, ssd_conv_w, ssd_conv_b, ssd_dt_bias, ssd_a_log, ssd_d, ssd_norm, lru_conv_w, lru_conv_b, lru_wa, lru_ba, lru_wx, lru_bx, lru_lambda, w_out, norm_mem, wq_mem, wk_mem, wv_mem, wo_mem, norm_ffn, router_group_w, router_group_b, router_expert_w, router_expert_b, moe_w_gate, moe_w_up, moe_w_down, final_norm):
    depth = w_in.shape[0]
    bp, tp, d = x_prompt.shape
    bs, ts, _ = x_sample.shape
    n_pool, page = cache_k.shape[1], cache_k.shape[2]
    n_pages = page_table.shape[1]
    past_len = n_pages * page
    n_mem = mem_prompt.shape[1]
    ssd_cc = state_ssd_conv.shape[-1]
    lru_w = state_lru.shape[-1]
    kvw = ATTN_KV_HEADS * ATTN_VD
    memw = MEM_HEADS * MEM_DH

    cos_p, sin_p = _rope_tables(jnp.tile(jnp.arange(tp, dtype=I32), bp))
    cos_s, sin_s = _rope_tables(jnp.tile(past_len + jnp.arange(ts, dtype=I32), bs))

    prompt = dict(bsz=bp, seq=tp, paged=False, cos=cos_p, sin=sin_p,
                  ssd_conv=jnp.zeros((depth, bp, SUBLANES, ssd_cc), F32),
                  ssd_state=jnp.zeros((depth, bp, SSD_HEADS * SSD_HEAD_DIM, SSD_STATE), F32),
                  lru_conv=jnp.zeros((depth, bp, SUBLANES, lru_w), F32),
                  lru_state=jnp.zeros((depth, bp, 1, lru_w), F32))
    sample = dict(bsz=bs, seq=ts, paged=True, cos=cos_s, sin=sin_s,
                  cache_k=cache_k.reshape(depth, n_pool, page * ATTN_KV_HEADS, ATTN_VD),
                  cache_v=cache_v.reshape(depth, n_pool, page * ATTN_KV_HEADS, ATTN_VD),
                  page_flat=page_table.reshape(-1), n_pool=n_pool, n_pages=n_pages,
                  ssd_conv=_conv_state8(state_ssd_conv),
                  ssd_state=state_ssd.reshape(depth, bs, SSD_HEADS * SSD_HEAD_DIM, SSD_STATE),
                  lru_conv=_conv_state8(state_lru_conv),
                  lru_state=state_lru.reshape(depth, bs, 1, lru_w),
                  mem_k=cache_mem_k.reshape(depth, bs, n_mem, memw), mem_v=cache_mem_v.reshape(depth, bs, n_mem, memw))

    xp = x_prompt.reshape(bp * tp, d)
    xs = x_sample.reshape(bs * ts, d)
    po, so, mks, mvs = [], [], [], []
    for l in range(depth):
        w = w_in[l]
        zc = V0 + kvw
        xc0 = zc + SSD_HEADS * SSD_HEAD_DIM
        dc = xc0 + ssd_cc
        rc = dc + SSD_HEADS
        w_pad = jnp.concatenate([w[:, :zc], w[:, xc0:dc], w[:, zc:xc0], w[:, rc:rc + 2 * lru_w], w[:, dc:rc],
                                 jnp.zeros((d, UW - DT0 - SSD_HEADS), F32)], axis=1).astype(BF16)
        router_w = jnp.pad(jnp.concatenate([router_group_w[l], router_expert_w[l]], axis=1),
                           ((0, 0), (0, LANES - MOE_GROUPS - MOE_EXPERTS)))
        router_b = _pad_lanes(jnp.concatenate([router_group_b[l], router_expert_b[l]]), LANES)
        lw = dict(
            norm_mix=norm_mix[l].reshape(1, d), w_in=w_pad,
            lams=[v[l].reshape(1, ATTN_DH) for v in (attn_lambda_q1, attn_lambda_k1, attn_lambda_q2, attn_lambda_k2)],
            subln=attn_subln[l].reshape(1, ATTN_VD),
            ssd_conv_w=ssd_conv_w[l], ssd_conv_b=ssd_conv_b[l].reshape(1, -1),
            ssd_dt_bias=_pad_lanes(ssd_dt_bias[l], LANES), ssd_a_log=_pad_lanes(ssd_a_log[l], LANES),
            ssd_d=jnp.repeat(ssd_d[l], SSD_HEAD_DIM).reshape(1, -1), ssd_norm=ssd_norm[l].reshape(1, -1),
            lru_conv_w=lru_conv_w[l], lru_conv_b=lru_conv_b[l].reshape(1, -1),
            lru_wa=_block_diag(lru_wa[l]).astype(BF16), lru_ba=lru_ba[l].reshape(1, -1),
            lru_wx=_block_diag(lru_wx[l]).astype(BF16), lru_bx=lru_bx[l].reshape(1, -1),
            lru_lambda=lru_lambda[l].reshape(1, -1),
            w_out=w_out[l].astype(BF16), norm_mem=norm_mem[l].reshape(1, d), wq_mem=wq_mem[l].astype(BF16),
            wo_mem=wo_mem[l].astype(BF16), norm_ffn=norm_ffn[l].reshape(1, d),
            router_w=router_w, router_b=router_b,
            moe_w_gate=moe_w_gate, moe_w_up=moe_w_up, moe_w_down=moe_w_down,
        )
        lam_init = 0.8 - 0.6 * math.exp(-0.3 * l)
        mk, mv = _mem_kv(mem_prompt.reshape(bp * n_mem, d), wk_mem[l].astype(BF16), wv_mem[l].astype(BF16))
        mks.append(mk.reshape(bp, n_mem, MEM_HEADS, MEM_DH))
        mvs.append(mv.reshape(bp, n_mem, MEM_HEADS, MEM_DH))
        pg = dict(prompt, mem_k=[None] * l + [mk.reshape(bp, n_mem, memw)], mem_v=[None] * l + [mv.reshape(bp, n_mem, memw)])
        routed_p, o = _layer_front(xp, pg, lw, l, lam_init)
        po.append(o)
        routed_s, o = _layer_front(xs, sample, lw, l, lam_init)
        so.append(o)
        xp, xs = _moe_groups([routed_p, routed_s], lw, l, l == depth - 1, final_norm.reshape(1, d))

    st = lambda outs, key: jnp.stack([o[key] for o in outs])
    return (xp.reshape(bp, tp, d), xs.reshape(bs, ts, d),
            st(po, "k"), st(po, "v"), jnp.stack(mks), jnp.stack(mvs),
            st(po, "ssd_conv"), st(po, "ssd_state"), st(po, "lru_conv"), st(po, "lru_state"),
            st(so, "k"), st(so, "v"), st(so, "ssd_conv"), st(so, "ssd_state"), st(so, "lru_conv"), st(so, "lru_state"))
```

```python
import functools
import math

import jax
import jax.numpy as jnp
from jax import lax
from jax.experimental import pallas as pl
from jax.experimental.pallas import tpu as pltpu

F32, BF16, I32 = jnp.float32, jnp.bfloat16, jnp.int32
EPS = 1e-6
LANES = 128
SUBLANES = 8
MIB = 1024 * 1024

ATTN_HEADS, ATTN_KV_HEADS, ATTN_DH = 8, 4, 64
ATTN_VD = 2 * ATTN_DH
ROPE_THETA = 10000.0
SSD_HEADS, SSD_HEAD_DIM, SSD_GROUPS, SSD_STATE, SSD_CONV = 8, 64, 2, 128, 4
SSD_CHUNK = 128
LRU_BLOCKS, LRU_CONV, LRU_C = 8, 4, 8.0
MEM_HEADS, MEM_DH = 4, 128
MOE_GROUPS, MOE_EXPERTS_PER_GROUP, MOE_TOPK = 4, 8, 2
MOE_EXPERTS = MOE_GROUPS * MOE_EXPERTS_PER_GROUP

Q0, K0, V0, XBC0, Z0, XR0, GATE0, DT0, UW = 0, 1024, 1536, 2048, 3072, 3584, 4096, 4608, 5120
PROJ_TN = 1024
PAGE_GROUP = 16


def _cparams(sem, vmem_mib):
    return pltpu.CompilerParams(dimension_semantics=sem, vmem_limit_bytes=vmem_mib * MIB)


def _tile(n, prefs):
    for p in prefs:
        if n % p == 0:
            return p
    return n


def _rms(x, w):
    return (x * lax.rsqrt(jnp.mean(x * x, axis=-1, keepdims=True) + EPS)) * w


def _nt(a, b):
    return lax.dot_general(a, b, (((1,), (1,)), ((), ())), preferred_element_type=F32)


def _mm(a, b):
    return jnp.dot(a, b, preferred_element_type=F32)


def _rope128(yc, cos, sin):
    lane = lax.broadcasted_iota(I32, yc.shape, 1)
    sw = jnp.where((lane % 64) < 32, pltpu.roll(yc, 96, 1), pltpu.roll(yc, 32, 1))
    return yc * cos + sw * sin


def _in_proj_body(x_ref, nw_ref, w_ref, cos_ref, sin_ref, o_ref):
    hb = _rms(x_ref[...], nw_ref[...]).astype(BF16)
    cos, sin = cos_ref[...], sin_ref[...]
    rope_end = V0
    for c0 in range(0, UW, PROJ_TN):
        y = _mm(hb, w_ref[:, c0:c0 + PROJ_TN])
        for c in range(c0, c0 + PROJ_TN, LANES):
            yc = y[:, c - c0:c - c0 + LANES]
            o_ref[:, c:c + LANES] = _rope128(yc, cos, sin) if c < rope_end else yc


def _in_proj(x, nw, w_pad, cos, sin):
    t, d = x.shape
    tm = _tile(t, (256, 128))
    return pl.pallas_call(
        _in_proj_body,
        grid=(t // tm,),
        in_specs=[pl.BlockSpec((tm, d), lambda i: (i, 0)),
                  pl.BlockSpec((1, d), lambda i: (0, 0)),
                  pl.BlockSpec((d, UW), lambda i: (0, 0), pipeline_mode=pl.Buffered(1)),
                  pl.BlockSpec((tm, LANES), lambda i: (i, 0)),
                  pl.BlockSpec((tm, LANES), lambda i: (i, 0))],
        out_specs=pl.BlockSpec((tm, UW), lambda i: (i, 0)),
        out_shape=jax.ShapeDtypeStruct((t, UW), F32),
        compiler_params=_cparams(("arbitrary",), 52),
        name="in_proj",
    )(x, nw, w_pad, cos, sin)


def _lambda(lq1, lk1, lq2, lk2, lam_init):
    s1 = jnp.sum(lq1[...] * lk1[...], axis=-1, keepdims=True)
    s2 = jnp.sum(lq2[...] * lk2[...], axis=-1, keepdims=True)
    return jnp.exp(s1) - jnp.exp(s2) + lam_init


def _attn_prompt_body(lq1, lk1, lq2, lk2, sub_ref, q_ref, k_ref, v_ref, o_ref, ko_ref, vo_ref, acc_sc,
                      *, tq, tk, lam_init):
    i = pl.program_id(2)
    lam = _lambda(lq1, lk1, lq2, lk2, lam_init)

    @pl.when(i == 0)
    def _():
        h = pl.program_id(1)
        rows = pl.ds(h, k_ref.shape[0], stride=ATTN_KV_HEADS)
        ko_ref[rows, :] = k_ref[...]
        vo_ref[rows, :] = v_ref[...]

    q = q_ref[...] * (ATTN_DH ** -0.5 * math.log2(math.e))
    lane = lax.broadcasted_iota(I32, (tq, LANES), 1)
    n4 = 4 * tq
    parts = []
    for g in range(2):
        qh = q[:, g * LANES:(g + 1) * LANES]
        parts.append(jnp.where(lane < ATTN_DH, qh, 0.0))
        parts.append(jnp.where(lane >= ATTN_DH, qh, 0.0))
    q4 = jnp.concatenate(parts, axis=0).astype(BF16)
    acc_sc[...] = jnp.zeros_like(acc_sc)

    def chunk(c, stats, masked):
        m_old, l_old = stats
        start = pl.multiple_of(c * tk, tk)
        k = k_ref[pl.ds(start, tk), :].astype(BF16)
        v = v_ref[pl.ds(start, tk), :].astype(BF16)
        st = _nt(k, q4)
        if masked:
            key = start + lax.broadcasted_iota(I32, (tk, n4), 0)
            t = i * tq + jnp.bitwise_and(lax.broadcasted_iota(I32, (tk, n4), 1), tq - 1)
            st = jnp.where(key <= t, st, -jnp.inf)
        m_new = jnp.maximum(m_old, jnp.max(st, axis=0, keepdims=True))
        alpha = jnp.exp2(m_old - m_new)
        p = jnp.exp2(st - m_new)
        pv = lax.dot_general(v, p.astype(BF16), (((0,), (0,)), ((), ())), preferred_element_type=F32)
        acc_sc[...] = alpha * acc_sc[...] + pv
        return m_new, alpha * l_old + jnp.sum(p, axis=0, keepdims=True)

    last = (i * tq) // tk
    stats = (jnp.full((1, n4), -jnp.inf, F32), jnp.zeros((1, n4), F32))
    stats = lax.fori_loop(0, last, lambda c, s: chunk(c, s, False), stats)
    _, l = chunk(last, stats, True)
    ot = acc_sc[...] / l
    for g in range(2):
        og = ot[:, (2 * g) * tq:(2 * g + 1) * tq] - lam * ot[:, (2 * g + 1) * tq:(2 * g + 2) * tq]
        ms = jnp.mean(og * og, axis=0, keepdims=True)
        nrm = (og * lax.rsqrt(ms + EPS)) * sub_ref[...] * (1.0 - lam_init)
        o_ref[:, g * LANES:(g + 1) * LANES] = nrm.T


def _attn_prompt(u, lams, subln, bsz, seq, lam_init):
    tq = _tile(seq, (256, 128))
    tk = 2 * tq if seq % (2 * tq) == 0 else tq
    nq = seq // tq
    vec = pl.BlockSpec((1, ATTN_DH), lambda b, h, i: (0, 0))
    return pl.pallas_call(
        functools.partial(_attn_prompt_body, tq=tq, tk=tk, lam_init=lam_init),
        grid=(bsz, ATTN_KV_HEADS, nq),
        in_specs=[vec, vec, vec, vec,
                  pl.BlockSpec((ATTN_VD, 1), lambda b, h, i: (0, 0)),
                  pl.BlockSpec((tq, 2 * LANES), lambda b, h, i: (b * nq + i, h)),
                  pl.BlockSpec((seq, LANES), lambda b, h, i: (b, K0 // LANES + h)),
                  pl.BlockSpec((seq, LANES), lambda b, h, i: (b, V0 // LANES + h))],
        out_specs=[pl.BlockSpec((tq, 2 * LANES), lambda b, h, i: (b * nq + i, h)),
                   pl.BlockSpec((seq * ATTN_KV_HEADS, ATTN_VD), lambda b, h, i: (b, 0)),
                   pl.BlockSpec((seq * ATTN_KV_HEADS, ATTN_VD), lambda b, h, i: (b, 0))],
        out_shape=[jax.ShapeDtypeStruct((bsz * seq, ATTN_HEADS * ATTN_VD), F32),
                   jax.ShapeDtypeStruct((bsz * seq * ATTN_KV_HEADS, ATTN_VD), F32),
                   jax.ShapeDtypeStruct((bsz * seq * ATTN_KV_HEADS, ATTN_VD), F32)],
        scratch_shapes=[pltpu.VMEM((ATTN_VD, 4 * tq), F32)],
        compiler_params=_cparams(("arbitrary", "arbitrary", "arbitrary"), 56),
        name="attn_prompt",
    )(*lams, subln.reshape(ATTN_VD, 1), u, u, u)


def _attn_sample_body(pt_ref, lq1, lk1, lq2, lk2, sub_ref, q_ref, kn_ref, vn_ref, *rest, n_pg, lam_init, tdec):
    del pt_ref
    k_refs, v_refs = rest[:n_pg], rest[n_pg:2 * n_pg]
    o_ref, m_sc, l_sc, acc_sc, qr_sc = rest[2 * n_pg:]
    j = pl.program_id(1)
    nrow = ATTN_KV_HEADS * 2 * 2 * tdec
    page_tok = k_refs[0].shape[0] // ATTN_KV_HEADS

    @pl.when(j == 0)
    def _():
        q = q_ref[...] * (ATTN_DH ** -0.5)
        lane = lax.broadcasted_iota(I32, (tdec, LANES), 1)
        qr_sc[...] = jnp.zeros_like(qr_sc)
        for kv in range(ATTN_KV_HEADS):
            for c in range(2):
                for g in range(2):
                    hd = kv * 2 + g
                    qh = q[:, hd * LANES:(hd + 1) * LANES]
                    n0 = ((kv * 2 + c) * 2 + g) * tdec
                    qr_sc[n0:n0 + tdec, kv * LANES:(kv + 1) * LANES] = jnp.where(
                        (lane >= c * ATTN_DH) & (lane < (c + 1) * ATTN_DH), qh, 0.0)
        m_sc[...] = jnp.full_like(m_sc, -jnp.inf)
        l_sc[...] = jnp.zeros_like(l_sc)
        acc_sc[...] = jnp.zeros_like(acc_sc)

    qr = qr_sc[...].astype(BF16)

    def page(ref):
        return jnp.concatenate([ref[pl.ds(kv, page_tok, stride=ATTN_KV_HEADS), :] for kv in range(ATTN_KV_HEADS)],
                               axis=1).astype(BF16)

    def update(s_list, v_list):
        m_old = m_sc[...]
        m_new = m_old
        for s in s_list:
            m_new = jnp.maximum(m_new, jnp.max(s, axis=1, keepdims=True))
        alpha = jnp.exp(m_old - m_new)
        l = alpha * l_sc[...]
        pv = None
        for s, v in zip(s_list, v_list):
            p = jnp.exp(s - m_new)
            l = l + jnp.sum(p, axis=1, keepdims=True)
            d = _mm(p.astype(BF16), v)
            pv = d if pv is None else pv + d
        m_sc[...] = m_new
        l_sc[...] = l
        acc_sc[...] = alpha * acc_sc[...] + pv

    for g0 in range(0, n_pg, PAGE_GROUP):
        grp = range(g0, min(g0 + PAGE_GROUP, n_pg))
        update([_nt(qr, page(k_refs[i])) for i in grp], [page(v_refs[i]) for i in grp])

    @pl.when(j == pl.num_programs(1) - 1)
    def _():
        lam = _lambda(lq1, lk1, lq2, lk2, lam_init)
        pad = jnp.zeros((LANES - tdec, kn_ref.shape[1]), F32)
        kn = jnp.concatenate([kn_ref[...], pad], axis=0).astype(BF16)
        vn = jnp.concatenate([vn_ref[...], pad], axis=0).astype(BF16)
        s = _nt(qr, kn)
        row = jnp.bitwise_and(lax.broadcasted_iota(I32, (nrow, LANES), 0), tdec - 1)
        col = lax.broadcasted_iota(I32, (nrow, LANES), 1)
        update([jnp.where(col <= row, s, -jnp.inf)], [vn])
        o = acc_sc[...] / l_sc[...]
        for kv in range(ATTN_KV_HEADS):
            for g in range(2):
                n0 = ((kv * 2 + 0) * 2 + g) * tdec
                n1 = ((kv * 2 + 1) * 2 + g) * tdec
                og = (o[n0:n0 + tdec, kv * LANES:(kv + 1) * LANES]
                      - lam * o[n1:n1 + tdec, kv * LANES:(kv + 1) * LANES])
                hd = kv * 2 + g
                o_ref[:, hd * LANES:(hd + 1) * LANES] = _rms(og, sub_ref[...]) * (1.0 - lam_init)


def _attn_sample(u, cache_k, cache_v, page_flat, layer, n_pool, lams, subln, bsz, tdec, n_pages, lam_init):
    del n_pool
    n_pg = _tile(n_pages, (16, 8, 4, 2, 1))
    nchunks = n_pages // n_pg
    prow = cache_k.shape[2]
    width = ATTN_KV_HEADS * LANES
    nrow = ATTN_KV_HEADS * 2 * 2 * tdec
    assert nrow == LANES and tdec == SUBLANES
    vec = pl.BlockSpec((1, ATTN_DH), lambda b, j, pt: (0, 0))

    def page_spec(i):
        return pl.BlockSpec((None, None, prow, ATTN_VD),
                            lambda b, j, pt: (layer, pt[b * n_pages + j * n_pg + i], 0, 0))

    in_specs = [vec, vec, vec, vec,
                pl.BlockSpec((1, ATTN_VD), lambda b, j, pt: (0, 0)),
                pl.BlockSpec((tdec, ATTN_HEADS * ATTN_VD), lambda b, j, pt: (b, 0)),
                pl.BlockSpec((tdec, width), lambda b, j, pt: (b, K0 // width)),
                pl.BlockSpec((tdec, width), lambda b, j, pt: (b, V0 // width))]
    in_specs += [page_spec(i) for i in range(n_pg)] * 2
    return pl.pallas_call(
        functools.partial(_attn_sample_body, n_pg=n_pg, lam_init=lam_init, tdec=tdec),
        grid_spec=pltpu.PrefetchScalarGridSpec(
            num_scalar_prefetch=1,
            grid=(bsz, nchunks),
            in_specs=in_specs,
            out_specs=pl.BlockSpec((tdec, ATTN_HEADS * ATTN_VD), lambda b, j, pt: (b, 0)),
            scratch_shapes=[pltpu.VMEM((nrow, 1), F32), pltpu.VMEM((nrow, 1), F32),
                            pltpu.VMEM((nrow, width), F32), pltpu.VMEM((nrow, width), F32)]),
        out_shape=jax.ShapeDtypeStruct((bsz * tdec, ATTN_HEADS * ATTN_VD), F32),
        compiler_params=_cparams(("arbitrary", "arbitrary"), 48),
        name="attn_sample",
    )(page_flat, *lams, subln, u, u, u, *([cache_k] * n_pg), *([cache_v] * n_pg))


def _causal_conv(x, prev, cw_ref, cb_ref):
    taps = cw_ref.shape[0]
    rowi = lax.broadcasted_iota(I32, x.shape, 0)
    acc = cb_ref[...] + x * cw_ref[taps - 1:taps, :]
    for s in range(1, taps):
        sh = jnp.where(rowi < s, pltpu.roll(prev, s, 0), pltpu.roll(x, s, 0))
        acc = acc + sh * cw_ref[taps - 1 - s:taps - s, :]
    return acc


def _pad_rows(x, rows):
    if x.shape[0] == rows:
        return x
    return jnp.concatenate([x, jnp.zeros((rows - x.shape[0], x.shape[1]), x.dtype)], axis=0)


def _ssd_body(xbc_ref, z_ref, dt_ref, st_ref, h0_ref, cw_ref, cb_ref, dtb_ref, alog_ref, dvec_ref, nw_ref,
              y_ref, hout_ref, prev_sc, h_sc, *, rows):
    c = pl.program_id(1)
    L = SSD_CHUNK
    width = SSD_HEADS * SSD_HEAD_DIM
    gw = SSD_GROUPS * SSD_STATE

    @pl.when(c == 0)
    def _():
        prev_sc[...] = jnp.concatenate([jnp.zeros((L - SUBLANES, prev_sc.shape[1]), F32), st_ref[...]], axis=0)
        h_sc[...] = h0_ref[...]

    x = _pad_rows(xbc_ref[...], L)
    conv = _causal_conv(x, prev_sc[...], cw_ref, cb_ref)
    prev_sc[...] = x
    xc = conv * jax.nn.sigmoid(conv)
    xs, bm, cm = xc[:, :width], xc[:, width:width + gw], xc[:, width + gw:]

    dt = jax.nn.softplus(_pad_rows(dt_ref[...], L) + dtb_ref[...])
    if rows < L:
        dt = jnp.where(lax.broadcasted_iota(I32, dt.shape, 0) < rows, dt, 0.0)
    da = dt * (-jnp.exp(alog_ref[...]))
    r0 = lax.broadcasted_iota(I32, (L, L), 0)
    c0 = lax.broadcasted_iota(I32, (L, L), 1)
    causal = r0 >= c0
    acum = jnp.dot(causal.astype(F32), da, precision=lax.Precision.HIGHEST, preferred_element_type=F32)
    acum_t, dt_t = acum.T, dt.T
    last = acum[L - 1:L, :]
    wend = jnp.exp(last - acum) * dt
    eac = jnp.exp(acum)
    elast = jnp.exp(last)
    lane = lax.broadcasted_iota(I32, (L, LANES), 1)
    first = lane < SSD_HEAD_DIM
    top = lax.broadcasted_iota(I32, (2 * SSD_HEAD_DIM, SSD_STATE), 0) < SSD_HEAD_DIM

    ys = []
    for pr in range(SSD_HEADS // 2):
        g = (2 * pr) // (SSD_HEADS // SSD_GROUPS)
        bg = bm[:, g * SSD_STATE:(g + 1) * SSD_STATE].astype(BF16)
        cg = cm[:, g * SSD_STATE:(g + 1) * SSD_STATE].astype(BF16)
        gmat = _nt(cg, bg)
        xp = xs[:, pr * LANES:(pr + 1) * LANES]
        xpb = xp.astype(BF16)
        hp = h_sc[pr * LANES:(pr + 1) * LANES, :]
        outs = []
        for hh in (2 * pr, 2 * pr + 1):
            seg = acum[:, hh:hh + 1] - acum_t[hh:hh + 1, :]
            dec = jnp.exp(jnp.where(causal, seg, -jnp.inf))
            w = gmat * dec * dt_t[hh:hh + 1, :]
            outs.append(_mm(w.astype(BF16), xpb))
        h_a, h_b = 2 * pr, 2 * pr + 1
        y_intra = jnp.where(first, outs[0], outs[1])
        e_pair = jnp.where(first, eac[:, h_a:h_a + 1], eac[:, h_b:h_b + 1])
        ys.append(y_intra + _nt(cg, hp.astype(BF16)) * e_pair)
        w_pair = jnp.where(first, wend[:, h_a:h_a + 1], wend[:, h_b:h_b + 1])
        upd = _mm((xp * w_pair).T.astype(BF16), bg)
        keep = jnp.where(top, elast[:, h_a:h_a + 1], elast[:, h_b:h_b + 1])
        h_sc[pr * LANES:(pr + 1) * LANES, :] = hp * keep + upd

    y = jnp.concatenate(ys, axis=1) + dvec_ref[...] * xs
    zz = _pad_rows(z_ref[...], L)
    gated = y * (zz * jax.nn.sigmoid(zz))
    y_ref[...] = _rms(gated, nw_ref[...])[:rows]

    @pl.when(c == pl.num_programs(1) - 1)
    def _():
        hout_ref[...] = h_sc[...]


def _ssd(u, st8, h0, cw, cb, dtb, alog, dvec, nw, bsz, seq):
    rows = SSD_CHUNK if seq % SSD_CHUNK == 0 else seq
    assert rows == SSD_CHUNK or (rows == seq and rows % SUBLANES == 0 and rows <= SSD_CHUNK)
    nch = seq // rows
    cc = cw.shape[1]
    width = SSD_HEADS * SSD_HEAD_DIM
    hp = SSD_HEADS * SSD_HEAD_DIM
    const = lambda shape: pl.BlockSpec(shape, lambda b, c: (0,) * len(shape))
    return pl.pallas_call(
        functools.partial(_ssd_body, rows=rows),
        grid=(bsz, nch),
        in_specs=[pl.BlockSpec((rows, cc), lambda b, c: (b * nch + c, XBC0 // cc)),
                  pl.BlockSpec((rows, width), lambda b, c: (b * nch + c, Z0 // width)),
                  pl.BlockSpec((rows, LANES), lambda b, c: (b * nch + c, DT0 // LANES)),
                  pl.BlockSpec((None, SUBLANES, cc), lambda b, c: (b, 0, 0)),
                  pl.BlockSpec((None, hp, SSD_STATE), lambda b, c: (b, 0, 0)),
                  const((SSD_CONV, cc)), const((1, cc)), const((1, LANES)), const((1, LANES)),
                  const((1, width)), const((1, width))],
        out_specs=[pl.BlockSpec((rows, width), lambda b, c: (b * nch + c, 0)),
                   pl.BlockSpec((None, hp, SSD_STATE), lambda b, c: (b, 0, 0))],
        out_shape=[jax.ShapeDtypeStruct((bsz * seq, width), F32),
                   jax.ShapeDtypeStruct((bsz, hp, SSD_STATE), F32)],
        scratch_shapes=[pltpu.VMEM((SSD_CHUNK, cc), F32), pltpu.VMEM((hp, SSD_STATE), F32)],
        compiler_params=_cparams(("arbitrary", "arbitrary"), 32),
        name="ssd",
    )(u, u, u, st8, h0, cw, cb, dtb, alog, dvec, nw)


def _expm1(t):
    u = jnp.exp(t)
    small = (u - 1.0) * t / jnp.log(u)
    return jnp.where(t < -1.0, u - 1.0, jnp.where(u == 1.0, t, small))


def _lru_body(xr_ref, gate_ref, st_ref, h0_ref, cw_ref, cb_ref, wa_ref, ba_ref, wx_ref, bx_ref, lam_ref,
              y_ref, hout_ref, prev_sc, h_sc, *, rows):
    c = pl.program_id(1)

    @pl.when(c == 0)
    def _():
        st = st_ref[...]
        if rows > SUBLANES:
            st = jnp.concatenate([jnp.zeros((rows - SUBLANES, st.shape[1]), F32), st], axis=0)
        prev_sc[...] = st
        h_sc[...] = h0_ref[...]

    x = xr_ref[...]
    xc = _causal_conv(x, prev_sc[...], cw_ref, cb_ref)
    prev_sc[...] = x
    xcb = xc.astype(BF16)
    r = jax.nn.sigmoid(_mm(xcb, wa_ref[...]) + ba_ref[...])
    ig = jax.nn.sigmoid(_mm(xcb, wx_ref[...]) + bx_ref[...])
    log_a = (-LRU_C) * r * jax.nn.softplus(-lam_ref[...])
    a = jnp.exp(log_a)
    b = jnp.sqrt(-_expm1(2.0 * log_a)) * (ig * xc)
    rowi = lax.broadcasted_iota(I32, a.shape, 0)
    d = 1
    while d < rows:
        ok = rowi >= d
        b = jnp.where(ok, a * pltpu.roll(b, d, 0) + b, b)
        a = jnp.where(ok, a * pltpu.roll(a, d, 0), a)
        d *= 2
    h = b + a * h_sc[...]
    h_sc[...] = h[rows - 1:rows, :]
    y_ref[...] = h * jax.nn.gelu(gate_ref[...])

    @pl.when(c == pl.num_programs(1) - 1)
    def _():
        hout_ref[...] = h[rows - 1:rows, :]


def _lru(u, st8, h0, cw, cb, wa, ba, wx, bx, lam, bsz, seq):
    rows = _tile(seq, (128, 64, 32, 16, 8))
    nch = seq // rows
    w = cw.shape[1]
    const = lambda shape: pl.BlockSpec(shape, lambda b, c: (0,) * len(shape))
    return pl.pallas_call(
        functools.partial(_lru_body, rows=rows),
        grid=(bsz, nch),
        in_specs=[pl.BlockSpec((rows, w), lambda b, c: (b * nch + c, XR0 // w)),
                  pl.BlockSpec((rows, w), lambda b, c: (b * nch + c, GATE0 // w)),
                  pl.BlockSpec((None, SUBLANES, w), lambda b, c: (b, 0, 0)),
                  pl.BlockSpec((None, 1, w), lambda b, c: (b, 0, 0)),
                  const((LRU_CONV, w)), const((1, w)), const((w, w)), const((1, w)), const((w, w)), const((1, w)),
                  const((1, w))],
        out_specs=[pl.BlockSpec((rows, w), lambda b, c: (b * nch + c, 0)),
                   pl.BlockSpec((None, 1, w), lambda b, c: (b, 0, 0))],
        out_shape=[jax.ShapeDtypeStruct((bsz * seq, w), F32), jax.ShapeDtypeStruct((bsz, 1, w), F32)],
        scratch_shapes=[pltpu.VMEM((rows, w), F32), pltpu.VMEM((1, w), F32)],
        compiler_params=_cparams(("arbitrary", "arbitrary"), 32),
        name="lru",
    )(u, u, st8, h0, cw, cb, wa, ba, wx, bx, lam)


def _out_proj_body(x_ref, a_ref, s_ref, l_ref, wo_ref, nw_ref, wq_ref, x1_ref, qm_ref):
    wa, ws = a_ref.shape[1], s_ref.shape[1]
    acc = x_ref[...] + _mm(a_ref[...].astype(BF16), wo_ref[0:wa, :])
    acc = acc + _mm(s_ref[...].astype(BF16), wo_ref[wa:wa + ws, :])
    acc = acc + _mm(l_ref[...].astype(BF16), wo_ref[wa + ws:, :])
    x1_ref[...] = acc
    qm_ref[...] = _mm(_rms(acc, nw_ref[...]).astype(BF16), wq_ref[...])


def _out_proj(x, attn, ssd, lru, wo, nw, wq):
    t, d = x.shape
    tm = _tile(t, (256, 128))
    row = lambda w: pl.BlockSpec((tm, w), lambda i: (i, 0))
    const = lambda shape: pl.BlockSpec(shape, lambda i: (0, 0))
    return pl.pallas_call(
        _out_proj_body,
        grid=(t // tm,),
        in_specs=[row(d), row(attn.shape[1]), row(ssd.shape[1]), row(lru.shape[1]),
                  const(wo.shape), const((1, d)), const(wq.shape)],
        out_specs=[row(d), row(wq.shape[1])],
        out_shape=[jax.ShapeDtypeStruct((t, d), F32), jax.ShapeDtypeStruct((t, wq.shape[1]), F32)],
        compiler_params=_cparams(("arbitrary",), 48),
        name="out_proj",
    )(x, attn, ssd, lru, wo, nw, wq)


def _mem_kv_body(m_ref, wk_ref, wv_ref, k_ref, v_ref):
    mb = m_ref[...].astype(BF16)
    k_ref[...] = _mm(mb, wk_ref[...])
    v_ref[...] = _mm(mb, wv_ref[...])


def _mem_kv(mem, wk, wv):
    t, d = mem.shape
    tm = _tile(t, (256, 128))
    w = wk.shape[1]
    return pl.pallas_call(
        _mem_kv_body,
        grid=(t // tm,),
        in_specs=[pl.BlockSpec((tm, d), lambda i: (i, 0)),
                  pl.BlockSpec((d, w), lambda i: (0, 0)), pl.BlockSpec((d, w), lambda i: (0, 0))],
        out_specs=[pl.BlockSpec((tm, w), lambda i: (i, 0)), pl.BlockSpec((tm, w), lambda i: (i, 0))],
        out_shape=[jax.ShapeDtypeStruct((t, w), F32), jax.ShapeDtypeStruct((t, w), F32)],
        compiler_params=_cparams(("arbitrary",), 32),
        name="mem_kv",
    )(mem, wk, wv)


def _xattn_body(q_ref, k_ref, v_ref, o_ref):
    for h in range(MEM_HEADS):
        sl = slice(h * MEM_DH, (h + 1) * MEM_DH)
        s = _nt(q_ref[:, sl].astype(BF16), k_ref[:, sl].astype(BF16)) * (MEM_DH ** -0.5)
        e = jnp.exp(s - jnp.max(s, axis=1, keepdims=True))
        o = _mm(e.astype(BF16), v_ref[:, sl].astype(BF16))
        o_ref[:, sl] = o / jnp.sum(e, axis=1, keepdims=True)


def _xattn(qm, mem_k, mem_v, bsz, seq):
    tq = _tile(seq, (256, 128))
    nq = seq // tq
    w = qm.shape[1]
    m = mem_k.shape[1]
    return pl.pallas_call(
        _xattn_body,
        grid=(bsz, nq),
        in_specs=[pl.BlockSpec((tq, w), lambda b, i: (b * nq + i, 0)),
                  pl.BlockSpec((None, m, w), lambda b, i: (b, 0, 0)),
                  pl.BlockSpec((None, m, w), lambda b, i: (b, 0, 0))],
        out_specs=pl.BlockSpec((tq, w), lambda b, i: (b * nq + i, 0)),
        out_shape=jax.ShapeDtypeStruct(qm.shape, F32),
        compiler_params=_cparams(("arbitrary", "arbitrary"), 32),
        name="xattn",
    )(qm, mem_k, mem_v)


def _route_body(*refs, n_own):
    hf_ref = refs[-3]
    i = pl.program_id(0)

    @pl.when(i < n_own)
    def _():
        _route_tile(*refs)

    @pl.when(i >= n_own)
    def _():
        hf_ref[...] = jnp.zeros_like(hf_ref)


def _route_tile(x1_ref, om_ref, wo_ref, nw_ref, rw_ref, rb_ref, *rest):
    x2_ref, hf_ref, eid_ref, gate_ref = rest[-4:]
    x2 = x1_ref[...] + _mm(om_ref[...].astype(BF16), wo_ref[...])
    x2_ref[...] = x2
    hf = _rms(x2, nw_ref[...])
    hf_ref[...] = hf
    logits = jnp.dot(hf, rw_ref[...], precision=lax.Precision.HIGHEST, preferred_element_type=F32) + rb_ref[...]
    lane = lax.broadcasted_iota(I32, logits.shape, 1).astype(F32)
    ninf = -jnp.inf
    big = float(LANES)

    def first_argmax(v, mx):
        return jnp.min(jnp.where(v == mx, lane, big), axis=1, keepdims=True)

    lg = jnp.where(lane < MOE_GROUPS, logits, ninf)
    mg = jnp.max(lg, axis=1, keepdims=True)
    p_grp = 1.0 / jnp.sum(jnp.exp(lg - mg), axis=1, keepdims=True)
    lo = MOE_GROUPS + MOE_EXPERTS_PER_GROUP * first_argmax(lg, mg)
    le = jnp.where((lane >= lo) & (lane < lo + MOE_EXPERTS_PER_GROUP), logits, ninf)
    v1 = jnp.max(le, axis=1, keepdims=True)
    i1 = first_argmax(le, v1)
    le2 = jnp.where(lane == i1, ninf, le)
    v2 = jnp.max(le2, axis=1, keepdims=True)
    i2 = first_argmax(le2, v2)
    e2 = jnp.exp(v2 - v1)
    den = 1.0 + e2
    g1 = p_grp * (1.0 / den)
    g2 = p_grp * (e2 / den)
    gate_ref[...] = jnp.where(lane == 0.0, g1, jnp.where(lane == 1.0, g2, 0.0))
    eid_ref[...] = jnp.where(lane == 0.0, i1 - MOE_GROUPS, jnp.where(lane == 1.0, i2 - MOE_GROUPS, 0.0)).astype(I32)


def _route(x1, om, wo, nw, rw, rb, hf_rows, hf_row0, hf_shared):
    t, d = x1.shape
    creating = hf_shared is None
    tail = hf_rows - hf_row0 - t if creating else 0
    tm = _tile(math.gcd(t, hf_row0, tail), (256, 128, 64, 32, 16, 8))
    blk0 = hf_row0 // tm
    n_own = t // tm
    n_steps = n_own + tail // tm
    row = lambda w: pl.BlockSpec((tm, w), lambda i: (jnp.minimum(i, n_own - 1), 0))
    const = lambda shape: pl.BlockSpec(shape, lambda i: (0, 0))
    in_specs = [row(d), row(om.shape[1]), const(wo.shape), const((1, d)), const(rw.shape), const((1, LANES))]
    args = [x1, om, wo, nw, rw, rb]
    aliases = {}
    if hf_shared is not None:
        in_specs.append(pl.BlockSpec(memory_space=pl.ANY))
        args.append(hf_shared)
        aliases = {len(args) - 1: 1}
    return pl.pallas_call(
        functools.partial(_route_body, n_own=n_own),
        grid=(n_steps,),
        in_specs=in_specs,
        out_specs=[row(d), pl.BlockSpec((tm, d), lambda i: (blk0 + i, 0)), row(LANES), row(LANES)],
        out_shape=[jax.ShapeDtypeStruct((t, d), F32), jax.ShapeDtypeStruct((hf_rows, d), F32),
                   jax.ShapeDtypeStruct((t, LANES), I32), jax.ShapeDtypeStruct((t, LANES), F32)],
        input_output_aliases=aliases,
        compiler_params=_cparams(("arbitrary",), 40),
        name="route",
    )(*args)


ROW_UNROLL = 8


def _moe_body(grow_ref, srow_ref, be_ref, base_ref, cnt_ref, hf_hbm, wg_ref, wu_ref, wd_ref, o_hbm,
              xbuf, ybuf, wgb, wub, wdb, gsem, ssem, *, rb):
    g = pl.program_id(0)
    ng = pl.num_programs(0)
    slot = lax.rem(g, 2)
    cnt = cnt_ref[g]

    def for_rows(n, fn):
        n_grp = lax.shift_right_logical(n, ROW_UNROLL.bit_length() - 1)

        def grp(q, c):
            for u in range(ROW_UNROLL):
                fn(q * ROW_UNROLL + u)
            return c

        def one(i, c):
            fn(i)
            return c

        lax.fori_loop(0, n_grp, grp, 0)
        lax.fori_loop(n_grp * ROW_UNROLL, n, one, 0)

    def gather_row(sl, i, src):
        return pltpu.make_async_copy(hf_hbm.at[pl.ds(src, 1), :], xbuf.at[sl, pl.ds(i, 1), :], gsem.at[sl])

    def scatter_row(sl, i, dst):
        return pltpu.make_async_copy(ybuf.at[sl, pl.ds(i, 1), :], o_hbm.at[pl.ds(dst, 1), :], ssem.at[sl])

    def start_gathers(blk, sl):
        base = base_ref[blk]
        for_rows(cnt_ref[blk], lambda i: gather_row(sl, i, grow_ref[base + i]).start())

    def start_scatters(blk, sl):
        base = base_ref[blk]
        for_rows(cnt_ref[blk], lambda i: scatter_row(sl, i, srow_ref[base + i]).start())

    def wait_rows(blk, sl, row_copy, block_copy):
        n = cnt_ref[blk]

        @pl.when(n == rb)
        def _():
            block_copy.wait()

        @pl.when(n < rb)
        def _():
            for_rows(n, lambda i: row_copy(sl, i, 0).wait())

    def wait_gathers(blk, sl):
        wait_rows(blk, sl, gather_row,
                  pltpu.make_async_copy(hf_hbm.at[pl.ds(0, rb), :], xbuf.at[sl], gsem.at[sl]))

    def wait_scatters(blk, sl):
        wait_rows(blk, sl, scatter_row,
                  pltpu.make_async_copy(ybuf.at[sl], o_hbm.at[pl.ds(0, rb), :], ssem.at[sl]))

    @pl.when(g == 0)
    def _():
        xbuf[...] = jnp.zeros_like(xbuf)
        start_gathers(0, 0)

    prev_e = be_ref[jnp.maximum(g - 1, 0)]

    @pl.when((cnt > 0) & ((g == 0) | (prev_e != be_ref[g])))
    def _():
        wgb[...] = wg_ref[...].astype(BF16)
        wub[...] = wu_ref[...].astype(BF16)
        wdb[...] = wd_ref[...].astype(BF16)

    wait_gathers(g, slot)

    @pl.when(g + 1 < ng)
    def _():
        start_gathers(g + 1, 1 - slot)

    @pl.when(g >= 2)
    def _():
        wait_scatters(g - 2, slot)

    @pl.when(cnt > 0)
    def _():
        xb = xbuf[slot].astype(BF16)
        hg = _mm(xb, wgb[...])
        act = (hg * jax.nn.sigmoid(hg)) * _mm(xb, wub[...])
        ybuf[slot] = _mm(act.astype(BF16), wdb[...])
        start_scatters(g, slot)

    @pl.when(g == ng - 1)
    def _():
        @pl.when(g >= 1)
        def _():
            wait_scatters(g - 1, 1 - slot)

        wait_scatters(g, slot)


def _moe_ffn(hf, grow, srow, blk_e, blk_base, blk_cnt, wg, wu, wd, layer, rb):
    t, d = hf.shape
    n_asg = grow.shape[0]
    ff = wg.shape[-1]
    n_blk = blk_e.shape[0]
    wspec = lambda shape: pl.BlockSpec((None, None) + shape, lambda g, gr, sr, be, bb, bc: (layer, be[g], 0, 0))
    return pl.pallas_call(
        functools.partial(_moe_body, rb=rb),
        grid_spec=pltpu.PrefetchScalarGridSpec(
            num_scalar_prefetch=5,
            grid=(n_blk,),
            in_specs=[pl.BlockSpec(memory_space=pl.ANY), wspec((d, ff)), wspec((d, ff)), wspec((ff, d))],
            out_specs=pl.BlockSpec(memory_space=pl.ANY),
            scratch_shapes=[pltpu.VMEM((2, rb, d), F32), pltpu.VMEM((2, rb, d), F32),
                            pltpu.VMEM((d, ff), BF16), pltpu.VMEM((d, ff), BF16), pltpu.VMEM((ff, d), BF16),
                            pltpu.SemaphoreType.DMA((2,)), pltpu.SemaphoreType.DMA((2,))]),
        out_shape=jax.ShapeDtypeStruct((n_asg, d), F32),
        compiler_params=_cparams(("arbitrary",), 56),
        name="moe_ffn",
    )(grow, srow, blk_e, blk_base, blk_cnt, hf, wg, wu, wd)


def _moe_plan(eid, n_tok, rb):
    n_asg = n_tok * MOE_TOPK
    flat_e = eid.reshape(-1)
    order = jnp.argsort(flat_e, stable=True).astype(I32)
    grow = order // MOE_TOPK
    srow = (order % MOE_TOPK) * n_tok + grow
    counts = jnp.sum((flat_e[:, None] == jnp.arange(MOE_EXPERTS, dtype=I32)[None, :]).astype(I32), axis=0)
    starts = jnp.cumsum(counts) - counts
    nblk = (counts + rb - 1) // rb
    blk_end = jnp.cumsum(nblk)
    n_blk = -(-n_asg // rb) + MOE_EXPERTS
    gi = jnp.arange(n_blk, dtype=I32)
    be = jnp.minimum(jnp.sum((gi[:, None] >= blk_end[None, :]).astype(I32), axis=1), MOE_EXPERTS - 1)
    r = gi - (blk_end - nblk)[be]
    cnt = jnp.clip(counts[be] - r * rb, 0, rb).astype(I32)
    base = jnp.minimum(starts[be] + r * rb, n_asg - 1).astype(I32)
    return grow.astype(I32), srow.astype(I32), be, base, cnt


def _combine_body(x2_ref, o0_ref, o1_ref, g_ref, fw_ref, x3_ref, *, final):
    g = g_ref[...]
    x3 = x2_ref[...] + (g[:, 0:1] * o0_ref[...] + g[:, 1:2] * o1_ref[...])
    x3_ref[...] = _rms(x3, fw_ref[...]) if final else x3


def _combine(x2, o2, gates, fw, final, row0, n_all):
    t, d = x2.shape
    tm = _tile(math.gcd(t, row0, n_all), (256, 128, 64, 32, 16, 8))
    assert t % tm == 0 and row0 % tm == 0 and n_all % tm == 0
    nt = t // tm
    b0, b1 = row0 // tm, (n_all + row0) // tm
    return pl.pallas_call(
        functools.partial(_combine_body, final=final),
        grid=(nt,),
        in_specs=[pl.BlockSpec((tm, d), lambda i: (i, 0)),
                  pl.BlockSpec((tm, d), lambda i: (b0 + i, 0)),
                  pl.BlockSpec((tm, d), lambda i: (b1 + i, 0)),
                  pl.BlockSpec((tm, LANES), lambda i: (i, 0)),
                  pl.BlockSpec((1, d), lambda i: (0, 0))],
        out_specs=pl.BlockSpec((tm, d), lambda i: (i, 0)),
        out_shape=jax.ShapeDtypeStruct((t, d), F32),
        compiler_params=_cparams(("arbitrary",), 32),
        name="combine",
    )(x2, o2, o2, gates, fw)


def _rope_tables(pos):
    half = ATTN_DH // 2
    inv = ROPE_THETA ** (-jnp.arange(half, dtype=F32) / half)
    ang = pos.astype(F32)[:, None] * inv[None, :]
    cos, sin = jnp.cos(ang), jnp.sin(ang)
    reps = LANES // ATTN_DH
    return jnp.tile(jnp.concatenate([cos, cos], axis=1), (1, reps)), jnp.tile(jnp.concatenate([-sin, sin], axis=1), (1, reps))


def _block_diag(w):
    g, a, b = w.shape
    eye = jnp.eye(g, dtype=w.dtype)
    return (w[:, :, None, :] * eye[:, None, :, None]).reshape(g * a, g * b)


def _pad_lanes(v, n):
    return jnp.pad(v.reshape(1, -1), ((0, 0), (0, n - v.size)))


def _layer_front(x, grp, lw, layer, lam_init, hf_rows, hf_row0, hf_shared):
    bsz, seq = grp["bsz"], grp["seq"]
    u = _in_proj(x, lw["norm_mix"], lw["w_in"], grp["cos"], grp["sin"])
    lams = lw["lams"]
    if grp["paged"]:
        attn = _attn_sample(u, grp["cache_k"], grp["cache_v"], grp["page_flat"], layer, grp["n_pool"], lams,
                            lw["subln"], bsz, seq, grp["n_pages"], lam_init)
        k_new = v_new = None
    else:
        attn, k_new, v_new = _attn_prompt(u, lams, lw["subln"], bsz, seq, lam_init)
    ssd, h_ssd = _ssd(u, grp["ssd_conv"][layer], grp["ssd_state"][layer], lw["ssd_conv_w"], lw["ssd_conv_b"],
                      lw["ssd_dt_bias"], lw["ssd_a_log"], lw["ssd_d"], lw["ssd_norm"], bsz, seq)
    lru, h_lru = _lru(u, grp["lru_conv"][layer], grp["lru_state"][layer], lw["lru_conv_w"], lw["lru_conv_b"],
                      lw["lru_wa"], lw["lru_ba"], lw["lru_wx"], lw["lru_bx"], lw["lru_lambda"], bsz, seq)
    x1, qm = _out_proj(x, attn, ssd, lru, lw["w_out"], lw["norm_mem"], lw["wq_mem"])
    om = _xattn(qm, grp["mem_k"][layer], grp["mem_v"][layer], bsz, seq)
    routed = _route(x1, om, lw["wo_mem"], lw["norm_ffn"], lw["router_w"], lw["router_b"],
                    hf_rows, hf_row0, hf_shared)
    ur = u.reshape(bsz, seq, UW)
    width_kv = ATTN_KV_HEADS * ATTN_VD
    kv_shape = (bsz, seq, ATTN_KV_HEADS, ATTN_VD)
    outs = dict(
        k=(ur[:, :, K0:K0 + width_kv] if k_new is None else k_new).reshape(kv_shape),
        v=(ur[:, :, V0:V0 + width_kv] if v_new is None else v_new).reshape(kv_shape),
        ssd_conv=ur[:, seq - (SSD_CONV - 1):, XBC0:Z0],
        ssd_state=h_ssd.reshape(bsz, SSD_HEADS, SSD_HEAD_DIM, SSD_STATE),
        lru_conv=ur[:, seq - (LRU_CONV - 1):, XR0:GATE0],
        lru_state=h_lru.reshape(bsz, -1),
    )
    return routed, outs


def _moe_groups(routed, lw, layer, final, final_norm):
    hf = routed[-1][1]
    eid = jnp.concatenate([r[2][:, :MOE_TOPK] for r in routed], axis=0)
    n_tok = hf.shape[0]
    rb = 256 if n_tok * MOE_TOPK >= 256 * MOE_EXPERTS else 128
    plan = _moe_plan(eid, n_tok, rb)
    o2 = _moe_ffn(hf, *plan, lw["moe_w_gate"], lw["moe_w_up"], lw["moe_w_down"], layer, rb)
    outs, row0 = [], 0
    for x2, _, _, gate in routed:
        outs.append(_combine(x2, o2, gate, final_norm, final, row0, n_tok))
        row0 += x2.shape[0]
    return outs


def _conv_state8(st):
    return jnp.pad(st, ((0, 0), (0, 0), (SUBLANES - st.shape[2], 0), (0, 0)))


def kernel(x_prompt, x_sample, cache_k, cache_v, cache_mem_k, cache_mem_v, state_ssd_conv, state_ssd, state_lru_conv, state_lru, page_table, mem_prompt, norm_mix, w_in, attn_lambda_q1, attn_lambda_k1, attn_lambda_q2, attn_lambda_k2, attn_subln, ssd_conv_w, ssd_conv_b, ssd_dt_bias, ssd_a_log, ssd_d, ssd_norm, lru_conv_w, lru_conv_b, lru_wa, lru_ba, lru_wx, lru_bx, lru_lambda, w_out, norm_mem, wq_mem, wk_mem, wv_mem, wo_mem, norm_ffn, router_group_w, router_group_b, router_expert_w, router_expert_b, moe_w_gate, moe_w_up, moe_w_down, final_norm):
    depth = w_in.shape[0]
    bp, tp, d = x_prompt.shape
    bs, ts, _ = x_sample.shape
    n_pool, page = cache_k.shape[1], cache_k.shape[2]
    n_pages = page_table.shape[1]
    past_len = n_pages * page
    n_mem = mem_prompt.shape[1]
    ssd_cc = state_ssd_conv.shape[-1]
    lru_w = state_lru.shape[-1]
    kvw = ATTN_KV_HEADS * ATTN_VD
    memw = MEM_HEADS * MEM_DH

    cos_p, sin_p = _rope_tables(jnp.tile(jnp.arange(tp, dtype=I32), bp))
    cos_s, sin_s = _rope_tables(jnp.tile(past_len + jnp.arange(ts, dtype=I32), bs))

    prompt = dict(bsz=bp, seq=tp, paged=False, cos=cos_p, sin=sin_p,
                  ssd_conv=jnp.zeros((depth, bp, SUBLANES, ssd_cc), F32),
                  ssd_state=jnp.zeros((depth, bp, SSD_HEADS * SSD_HEAD_DIM, SSD_STATE), F32),
                  lru_conv=jnp.zeros((depth, bp, SUBLANES, lru_w), F32),
                  lru_state=jnp.zeros((depth, bp, 1, lru_w), F32))
    sample = dict(bsz=bs, seq=ts, paged=True, cos=cos_s, sin=sin_s,
                  cache_k=cache_k.reshape(depth, n_pool, page * ATTN_KV_HEADS, ATTN_VD),
                  cache_v=cache_v.reshape(depth, n_pool, page * ATTN_KV_HEADS, ATTN_VD),
                  page_flat=page_table.reshape(-1), n_pool=n_pool, n_pages=n_pages,
                  ssd_conv=_conv_state8(state_ssd_conv),
                  ssd_state=state_ssd.reshape(depth, bs, SSD_HEADS * SSD_HEAD_DIM, SSD_STATE),
                  lru_conv=_conv_state8(state_lru_conv),
                  lru_state=state_lru.reshape(depth, bs, 1, lru_w),
                  mem_k=cache_mem_k.reshape(depth, bs, n_mem, memw), mem_v=cache_mem_v.reshape(depth, bs, n_mem, memw))

    xp = x_prompt.reshape(bp * tp, d)
    xs = x_sample.reshape(bs * ts, d)
    po, so, mks, mvs = [], [], [], []
    for l in range(depth):
        w = w_in[l]
        zc = V0 + kvw
        xc0 = zc + SSD_HEADS * SSD_HEAD_DIM
        dc = xc0 + ssd_cc
        rc = dc + SSD_HEADS
        w_pad = jnp.concatenate([w[:, :zc], w[:, xc0:dc], w[:, zc:xc0], w[:, rc:rc + 2 * lru_w], w[:, dc:rc],
                                 jnp.zeros((d, UW - DT0 - SSD_HEADS), F32)], axis=1).astype(BF16)
        router_w = jnp.pad(jnp.concatenate([router_group_w[l], router_expert_w[l]], axis=1),
                           ((0, 0), (0, LANES - MOE_GROUPS - MOE_EXPERTS)))
        router_b = _pad_lanes(jnp.concatenate([router_group_b[l], router_expert_b[l]]), LANES)
        lw = dict(
            norm_mix=norm_mix[l].reshape(1, d), w_in=w_pad,
            lams=[v[l].reshape(1, ATTN_DH) for v in (attn_lambda_q1, attn_lambda_k1, attn_lambda_q2, attn_lambda_k2)],
            subln=attn_subln[l].reshape(1, ATTN_VD),
            ssd_conv_w=ssd_conv_w[l], ssd_conv_b=ssd_conv_b[l].reshape(1, -1),
            ssd_dt_bias=_pad_lanes(ssd_dt_bias[l], LANES), ssd_a_log=_pad_lanes(ssd_a_log[l], LANES),
            ssd_d=jnp.repeat(ssd_d[l], SSD_HEAD_DIM).reshape(1, -1), ssd_norm=ssd_norm[l].reshape(1, -1),
            lru_conv_w=lru_conv_w[l], lru_conv_b=lru_conv_b[l].reshape(1, -1),
            lru_wa=_block_diag(lru_wa[l]).astype(BF16), lru_ba=lru_ba[l].reshape(1, -1),
            lru_wx=_block_diag(lru_wx[l]).astype(BF16), lru_bx=lru_bx[l].reshape(1, -1),
            lru_lambda=lru_lambda[l].reshape(1, -1),
            w_out=w_out[l].astype(BF16), norm_mem=norm_mem[l].reshape(1, d), wq_mem=wq_mem[l].astype(BF16),
            wo_mem=wo_mem[l].astype(BF16), norm_ffn=norm_ffn[l].reshape(1, d),
            router_w=router_w, router_b=router_b,
            moe_w_gate=moe_w_gate, moe_w_up=moe_w_up, moe_w_down=moe_w_down,
        )
        lam_init = 0.8 - 0.6 * math.exp(-0.3 * l)
        mk, mv = _mem_kv(mem_prompt.reshape(bp * n_mem, d), wk_mem[l].astype(BF16), wv_mem[l].astype(BF16))
        mks.append(mk.reshape(bp, n_mem, MEM_HEADS, MEM_DH))
        mvs.append(mv.reshape(bp, n_mem, MEM_HEADS, MEM_DH))
        pg = dict(prompt, mem_k=[None] * l + [mk.reshape(bp, n_mem, memw)], mem_v=[None] * l + [mv.reshape(bp, n_mem, memw)])
        n_all = xp.shape[0] + xs.shape[0]
        routed_p, o = _layer_front(xp, pg, lw, l, lam_init, n_all, 0, None)
        po.append(o)
        routed_s, o = _layer_front(xs, sample, lw, l, lam_init, n_all, xp.shape[0], routed_p[1])
        so.append(o)
        xp, xs = _moe_groups([routed_p, routed_s], lw, l, l == depth - 1, final_norm.reshape(1, d))

    st = lambda outs, key: jnp.stack([o[key] for o in outs])
    return (xp.reshape(bp, tp, d), xs.reshape(bs, ts, d),
            st(po, "k"), st(po, "v"), jnp.stack(mks), jnp.stack(mvs),
            st(po, "ssd_conv"), st(po, "ssd_state"), st(po, "lru_conv"), st(po, "lru_state"),
            st(so, "k"), st(so, "v"), st(so, "ssd_conv"), st(so, "ssd_state"), st(so, "lru_conv"), st(so, "lru_state"))
```

```python
import functools
import math

import jax
import jax.numpy as jnp
from jax import lax
from jax.experimental import pallas as pl
from jax.experimental.pallas import tpu as pltpu

F32, BF16, I32 = jnp.float32, jnp.bfloat16, jnp.int32
EPS = 1e-6
LANES = 128
SUBLANES = 8
MIB = 1024 * 1024

ATTN_HEADS, ATTN_KV_HEADS, ATTN_DH = 8, 4, 64
ATTN_VD = 2 * ATTN_DH
ROPE_THETA = 10000.0
SSD_HEADS, SSD_HEAD_DIM, SSD_GROUPS, SSD_STATE, SSD_CONV = 8, 64, 2, 128, 4
SSD_CHUNK = 128
LRU_BLOCKS, LRU_CONV, LRU_C = 8, 4, 8.0
MEM_HEADS, MEM_DH = 4, 128
MOE_GROUPS, MOE_EXPERTS_PER_GROUP, MOE_TOPK = 4, 8, 2
MOE_EXPERTS = MOE_GROUPS * MOE_EXPERTS_PER_GROUP

Q0, K0, V0, XBC0, Z0, XR0, GATE0, DT0, UW = 0, 1024, 1536, 2048, 3072, 3584, 4096, 4608, 5120
PROJ_TN = 1024
PAGE_GROUP = 16


def _cparams(sem, vmem_mib):
    return pltpu.CompilerParams(dimension_semantics=sem, vmem_limit_bytes=vmem_mib * MIB)


def _tile(n, prefs):
    for p in prefs:
        if n % p == 0:
            return p
    return n


def _rms(x, w):
    return (x * lax.rsqrt(jnp.mean(x * x, axis=-1, keepdims=True) + EPS)) * w


def _nt(a, b):
    return lax.dot_general(a, b, (((1,), (1,)), ((), ())), preferred_element_type=F32)


def _mm(a, b):
    return jnp.dot(a, b, preferred_element_type=F32)


def _rope128(yc, cos, sin):
    lane = lax.broadcasted_iota(I32, yc.shape, 1)
    sw = jnp.where((lane % 64) < 32, pltpu.roll(yc, 96, 1), pltpu.roll(yc, 32, 1))
    return yc * cos + sw * sin


def _in_proj_body(x_ref, nw_ref, w_ref, cos_ref, sin_ref, o_ref):
    hb = _rms(x_ref[...], nw_ref[...]).astype(BF16)
    cos, sin = cos_ref[...], sin_ref[...]
    rope_end = V0
    for c0 in range(0, UW, PROJ_TN):
        y = _mm(hb, w_ref[:, c0:c0 + PROJ_TN])
        for c in range(c0, c0 + PROJ_TN, LANES):
            yc = y[:, c - c0:c - c0 + LANES]
            o_ref[:, c:c + LANES] = _rope128(yc, cos, sin) if c < rope_end else yc


def _in_proj(x, nw, w_pad, cos, sin):
    t, d = x.shape
    tm = _tile(t, (256, 128))
    return pl.pallas_call(
        _in_proj_body,
        grid=(t // tm,),
        in_specs=[pl.BlockSpec((tm, d), lambda i: (i, 0)),
                  pl.BlockSpec((1, d), lambda i: (0, 0)),
                  pl.BlockSpec((d, UW), lambda i: (0, 0), pipeline_mode=pl.Buffered(1)),
                  pl.BlockSpec((tm, LANES), lambda i: (i, 0)),
                  pl.BlockSpec((tm, LANES), lambda i: (i, 0))],
        out_specs=pl.BlockSpec((tm, UW), lambda i: (i, 0)),
        out_shape=jax.ShapeDtypeStruct((t, UW), F32),
        compiler_params=_cparams(("arbitrary",), 52),
        name="in_proj",
    )(x, nw, w_pad, cos, sin)


def _lambda(lq1, lk1, lq2, lk2, lam_init):
    s1 = jnp.sum(lq1[...] * lk1[...], axis=-1, keepdims=True)
    s2 = jnp.sum(lq2[...] * lk2[...], axis=-1, keepdims=True)
    return jnp.exp(s1) - jnp.exp(s2) + lam_init


def _attn_prompt_body(lq1, lk1, lq2, lk2, sub_ref, q_ref, k_ref, v_ref, o_ref, ko_ref, vo_ref, acc_sc,
                      *, tq, tk, lam_init):
    i = pl.program_id(2)
    lam = _lambda(lq1, lk1, lq2, lk2, lam_init)

    @pl.when(i == 0)
    def _():
        h = pl.program_id(1)
        rows = pl.ds(h, k_ref.shape[0], stride=ATTN_KV_HEADS)
        ko_ref[rows, :] = k_ref[...]
        vo_ref[rows, :] = v_ref[...]

    q = q_ref[...] * (ATTN_DH ** -0.5 * math.log2(math.e))
    lane = lax.broadcasted_iota(I32, (tq, LANES), 1)
    n4 = 4 * tq
    parts = []
    for g in range(2):
        qh = q[:, g * LANES:(g + 1) * LANES]
        parts.append(jnp.where(lane < ATTN_DH, qh, 0.0))
        parts.append(jnp.where(lane >= ATTN_DH, qh, 0.0))
    q4 = jnp.concatenate(parts, axis=0).astype(BF16)
    acc_sc[...] = jnp.zeros_like(acc_sc)

    def chunk(c, stats, masked):
        m_old, l_old = stats
        start = pl.multiple_of(c * tk, tk)
        k = k_ref[pl.ds(start, tk), :].astype(BF16)
        v = v_ref[pl.ds(start, tk), :].astype(BF16)
        st = _nt(k, q4)
        if masked:
            key = start + lax.broadcasted_iota(I32, (tk, n4), 0)
            t = i * tq + jnp.bitwise_and(lax.broadcasted_iota(I32, (tk, n4), 1), tq - 1)
            st = jnp.where(key <= t, st, -jnp.inf)
        m_new = jnp.maximum(m_old, jnp.max(st, axis=0, keepdims=True))
        alpha = jnp.exp2(m_old - m_new)
        p = jnp.exp2(st - m_new)
        pv = lax.dot_general(v, p.astype(BF16), (((0,), (0,)), ((), ())), preferred_element_type=F32)
        acc_sc[...] = alpha * acc_sc[...] + pv
        return m_new, alpha * l_old + jnp.sum(p, axis=0, keepdims=True)

    last = (i * tq) // tk
    stats = (jnp.full((1, n4), -jnp.inf, F32), jnp.zeros((1, n4), F32))
    stats = lax.fori_loop(0, last, lambda c, s: chunk(c, s, False), stats)
    _, l = chunk(last, stats, True)
    ot = acc_sc[...] / l
    for g in range(2):
        og = ot[:, (2 * g) * tq:(2 * g + 1) * tq] - lam * ot[:, (2 * g + 1) * tq:(2 * g + 2) * tq]
        ms = jnp.mean(og * og, axis=0, keepdims=True)
        nrm = (og * lax.rsqrt(ms + EPS)) * sub_ref[...] * (1.0 - lam_init)
        o_ref[:, g * LANES:(g + 1) * LANES] = nrm.T


def _attn_prompt(u, lams, subln, bsz, seq, lam_init):
    tq = _tile(seq, (256, 128))
    tk = 2 * tq if seq % (2 * tq) == 0 else tq
    nq = seq // tq
    vec = pl.BlockSpec((1, ATTN_DH), lambda b, h, i: (0, 0))
    return pl.pallas_call(
        functools.partial(_attn_prompt_body, tq=tq, tk=tk, lam_init=lam_init),
        grid=(bsz, ATTN_KV_HEADS, nq),
        in_specs=[vec, vec, vec, vec,
                  pl.BlockSpec((ATTN_VD, 1), lambda b, h, i: (0, 0)),
                  pl.BlockSpec((tq, 2 * LANES), lambda b, h, i: (b * nq + i, h)),
                  pl.BlockSpec((seq, LANES), lambda b, h, i: (b, K0 // LANES + h)),
                  pl.BlockSpec((seq, LANES), lambda b, h, i: (b, V0 // LANES + h))],
        out_specs=[pl.BlockSpec((tq, 2 * LANES), lambda b, h, i: (b * nq + i, h)),
                   pl.BlockSpec((seq * ATTN_KV_HEADS, ATTN_VD), lambda b, h, i: (b, 0)),
                   pl.BlockSpec((seq * ATTN_KV_HEADS, ATTN_VD), lambda b, h, i: (b, 0))],
        out_shape=[jax.ShapeDtypeStruct((bsz * seq, ATTN_HEADS * ATTN_VD), F32),
                   jax.ShapeDtypeStruct((bsz * seq * ATTN_KV_HEADS, ATTN_VD), F32),
                   jax.ShapeDtypeStruct((bsz * seq * ATTN_KV_HEADS, ATTN_VD), F32)],
        scratch_shapes=[pltpu.VMEM((ATTN_VD, 4 * tq), F32)],
        compiler_params=_cparams(("arbitrary", "arbitrary", "arbitrary"), 56),
        name="attn_prompt",
    )(*lams, subln.reshape(ATTN_VD, 1), u, u, u)


def _attn_sample_body(pt_ref, lq1, lk1, lq2, lk2, sub_ref, q_ref, kn_ref, vn_ref, *rest, n_pg, lam_init, tdec):
    del pt_ref
    k_refs, v_refs = rest[:n_pg], rest[n_pg:2 * n_pg]
    o_ref, m_sc, l_sc, acc_sc, qr_sc = rest[2 * n_pg:]
    j = pl.program_id(1)
    nrow = ATTN_KV_HEADS * 2 * 2 * tdec
    page_tok = k_refs[0].shape[0] // ATTN_KV_HEADS

    @pl.when(j == 0)
    def _():
        q = q_ref[...] * (ATTN_DH ** -0.5)
        lane = lax.broadcasted_iota(I32, (tdec, LANES), 1)
        qr_sc[...] = jnp.zeros_like(qr_sc)
        for kv in range(ATTN_KV_HEADS):
            for c in range(2):
                for g in range(2):
                    hd = kv * 2 + g
                    qh = q[:, hd * LANES:(hd + 1) * LANES]
                    n0 = ((kv * 2 + c) * 2 + g) * tdec
                    qr_sc[n0:n0 + tdec, kv * LANES:(kv + 1) * LANES] = jnp.where(
                        (lane >= c * ATTN_DH) & (lane < (c + 1) * ATTN_DH), qh, 0.0)
        m_sc[...] = jnp.full_like(m_sc, -jnp.inf)
        l_sc[...] = jnp.zeros_like(l_sc)
        acc_sc[...] = jnp.zeros_like(acc_sc)

    qr = qr_sc[...].astype(BF16)

    def page(ref):
        return jnp.concatenate([ref[pl.ds(kv, page_tok, stride=ATTN_KV_HEADS), :] for kv in range(ATTN_KV_HEADS)],
                               axis=1).astype(BF16)

    def update(s_list, v_list):
        m_old = m_sc[...]
        m_new = m_old
        for s in s_list:
            m_new = jnp.maximum(m_new, jnp.max(s, axis=1, keepdims=True))
        alpha = jnp.exp(m_old - m_new)
        l = alpha * l_sc[...]
        pv = None
        for s, v in zip(s_list, v_list):
            p = jnp.exp(s - m_new)
            l = l + jnp.sum(p, axis=1, keepdims=True)
            d = _mm(p.astype(BF16), v)
            pv = d if pv is None else pv + d
        m_sc[...] = m_new
        l_sc[...] = l
        acc_sc[...] = alpha * acc_sc[...] + pv

    for g0 in range(0, n_pg, PAGE_GROUP):
        grp = range(g0, min(g0 + PAGE_GROUP, n_pg))
        update([_nt(qr, page(k_refs[i])) for i in grp], [page(v_refs[i]) for i in grp])

    @pl.when(j == pl.num_programs(1) - 1)
    def _():
        lam = _lambda(lq1, lk1, lq2, lk2, lam_init)
        pad = jnp.zeros((LANES - tdec, kn_ref.shape[1]), F32)
        kn = jnp.concatenate([kn_ref[...], pad], axis=0).astype(BF16)
        vn = jnp.concatenate([vn_ref[...], pad], axis=0).astype(BF16)
        s = _nt(qr, kn)
        row = jnp.bitwise_and(lax.broadcasted_iota(I32, (nrow, LANES), 0), tdec - 1)
        col = lax.broadcasted_iota(I32, (nrow, LANES), 1)
        update([jnp.where(col <= row, s, -jnp.inf)], [vn])
        o = acc_sc[...] / l_sc[...]
        for kv in range(ATTN_KV_HEADS):
            for g in range(2):
                n0 = ((kv * 2 + 0) * 2 + g) * tdec
                n1 = ((kv * 2 + 1) * 2 + g) * tdec
                og = (o[n0:n0 + tdec, kv * LANES:(kv + 1) * LANES]
                      - lam * o[n1:n1 + tdec, kv * LANES:(kv + 1) * LANES])
                hd = kv * 2 + g
                o_ref[:, hd * LANES:(hd + 1) * LANES] = _rms(og, sub_ref[...]) * (1.0 - lam_init)


def _attn_sample(u, cache_k, cache_v, page_flat, layer, n_pool, lams, subln, bsz, tdec, n_pages, lam_init):
    del n_pool
    n_pg = _tile(n_pages, (32, 16, 8, 4, 2, 1))
    nchunks = n_pages // n_pg
    prow = cache_k.shape[2]
    width = ATTN_KV_HEADS * LANES
    nrow = ATTN_KV_HEADS * 2 * 2 * tdec
    assert nrow == LANES and tdec == SUBLANES
    vec = pl.BlockSpec((1, ATTN_DH), lambda b, j, pt: (0, 0))

    def page_spec(i):
        return pl.BlockSpec((None, None, prow, ATTN_VD),
                            lambda b, j, pt: (layer, pt[b * n_pages + j * n_pg + i], 0, 0))

    in_specs = [vec, vec, vec, vec,
                pl.BlockSpec((1, ATTN_VD), lambda b, j, pt: (0, 0)),
                pl.BlockSpec((tdec, ATTN_HEADS * ATTN_VD), lambda b, j, pt: (b, 0)),
                pl.BlockSpec((tdec, width), lambda b, j, pt: (b, K0 // width)),
                pl.BlockSpec((tdec, width), lambda b, j, pt: (b, V0 // width))]
    in_specs += [page_spec(i) for i in range(n_pg)] * 2
    return pl.pallas_call(
        functools.partial(_attn_sample_body, n_pg=n_pg, lam_init=lam_init, tdec=tdec),
        grid_spec=pltpu.PrefetchScalarGridSpec(
            num_scalar_prefetch=1,
            grid=(bsz, nchunks),
            in_specs=in_specs,
            out_specs=pl.BlockSpec((tdec, ATTN_HEADS * ATTN_VD), lambda b, j, pt: (b, 0)),
            scratch_shapes=[pltpu.VMEM((nrow, 1), F32), pltpu.VMEM((nrow, 1), F32),
                            pltpu.VMEM((nrow, width), F32), pltpu.VMEM((nrow, width), F32)]),
        out_shape=jax.ShapeDtypeStruct((bsz * tdec, ATTN_HEADS * ATTN_VD), F32),
        compiler_params=_cparams(("arbitrary", "arbitrary"), 56),
        name="attn_sample",
    )(page_flat, *lams, subln, u, u, u, *([cache_k] * n_pg), *([cache_v] * n_pg))


def _causal_conv(x, prev, cw_ref, cb_ref):
    taps = cw_ref.shape[0]
    rowi = lax.broadcasted_iota(I32, x.shape, 0)
    acc = cb_ref[...] + x * cw_ref[taps - 1:taps, :]
    for s in range(1, taps):
        sh = jnp.where(rowi < s, pltpu.roll(prev, s, 0), pltpu.roll(x, s, 0))
        acc = acc + sh * cw_ref[taps - 1 - s:taps - s, :]
    return acc


def _pad_rows(x, rows):
    if x.shape[0] == rows:
        return x
    return jnp.concatenate([x, jnp.zeros((rows - x.shape[0], x.shape[1]), x.dtype)], axis=0)


def _ssd_body(xbc_ref, z_ref, dt_ref, st_ref, h0_ref, cw_ref, cb_ref, dtb_ref, alog_ref, dvec_ref, nw_ref,
              y_ref, hout_ref, prev_sc, h_sc, *, rows):
    c = pl.program_id(1)
    L = SSD_CHUNK
    width = SSD_HEADS * SSD_HEAD_DIM
    gw = SSD_GROUPS * SSD_STATE

    @pl.when(c == 0)
    def _():
        prev_sc[...] = jnp.concatenate([jnp.zeros((L - SUBLANES, prev_sc.shape[1]), F32), st_ref[...]], axis=0)
        h_sc[...] = h0_ref[...]

    x = _pad_rows(xbc_ref[...], L)
    conv = _causal_conv(x, prev_sc[...], cw_ref, cb_ref)
    prev_sc[...] = x
    xc = conv * jax.nn.sigmoid(conv)
    xs, bm, cm = xc[:, :width], xc[:, width:width + gw], xc[:, width + gw:]

    dt = jax.nn.softplus(_pad_rows(dt_ref[...], L) + dtb_ref[...])
    if rows < L:
        dt = jnp.where(lax.broadcasted_iota(I32, dt.shape, 0) < rows, dt, 0.0)
    da = dt * (-jnp.exp(alog_ref[...]))
    r0 = lax.broadcasted_iota(I32, (L, L), 0)
    c0 = lax.broadcasted_iota(I32, (L, L), 1)
    causal = r0 >= c0
    acum = jnp.dot(causal.astype(F32), da, precision=lax.Precision.HIGHEST, preferred_element_type=F32)
    acum_t, dt_t = acum.T, dt.T
    last = acum[L - 1:L, :]
    wend = jnp.exp(last - acum) * dt
    eac = jnp.exp(acum)
    elast = jnp.exp(last)
    lane = lax.broadcasted_iota(I32, (L, LANES), 1)
    first = lane < SSD_HEAD_DIM
    top = lax.broadcasted_iota(I32, (2 * SSD_HEAD_DIM, SSD_STATE), 0) < SSD_HEAD_DIM

    ys = []
    for pr in range(SSD_HEADS // 2):
        g = (2 * pr) // (SSD_HEADS // SSD_GROUPS)
        bg = bm[:, g * SSD_STATE:(g + 1) * SSD_STATE].astype(BF16)
        cg = cm[:, g * SSD_STATE:(g + 1) * SSD_STATE].astype(BF16)
        gmat = _nt(cg, bg)
        xp = xs[:, pr * LANES:(pr + 1) * LANES]
        xpb = xp.astype(BF16)
        hp = h_sc[pr * LANES:(pr + 1) * LANES, :]
        outs = []
        for hh in (2 * pr, 2 * pr + 1):
            seg = acum[:, hh:hh + 1] - acum_t[hh:hh + 1, :]
            dec = jnp.exp(jnp.where(causal, seg, -jnp.inf))
            w = gmat * dec * dt_t[hh:hh + 1, :]
            outs.append(_mm(w.astype(BF16), xpb))
        h_a, h_b = 2 * pr, 2 * pr + 1
        y_intra = jnp.where(first, outs[0], outs[1])
        e_pair = jnp.where(first, eac[:, h_a:h_a + 1], eac[:, h_b:h_b + 1])
        ys.append(y_intra + _nt(cg, hp.astype(BF16)) * e_pair)
        w_pair = jnp.where(first, wend[:, h_a:h_a + 1], wend[:, h_b:h_b + 1])
        upd = _mm((xp * w_pair).T.astype(BF16), bg)
        keep = jnp.where(top, elast[:, h_a:h_a + 1], elast[:, h_b:h_b + 1])
        h_sc[pr * LANES:(pr + 1) * LANES, :] = hp * keep + upd

    y = jnp.concatenate(ys, axis=1) + dvec_ref[...] * xs
    zz = _pad_rows(z_ref[...], L)
    gated = y * (zz * jax.nn.sigmoid(zz))
    y_ref[...] = _rms(gated, nw_ref[...])[:rows]

    @pl.when(c == pl.num_programs(1) - 1)
    def _():
        hout_ref[...] = h_sc[...]


def _ssd(u, st8, h0, layer, cw, cb, dtb, alog, dvec, nw, bsz, seq):
    rows = SSD_CHUNK if seq % SSD_CHUNK == 0 else seq
    assert rows == SSD_CHUNK or (rows == seq and rows % SUBLANES == 0 and rows <= SSD_CHUNK)
    nch = seq // rows
    cc = cw.shape[1]
    width = SSD_HEADS * SSD_HEAD_DIM
    hp = SSD_HEADS * SSD_HEAD_DIM
    const = lambda shape: pl.BlockSpec(shape, lambda b, c: (0,) * len(shape))
    return pl.pallas_call(
        functools.partial(_ssd_body, rows=rows),
        grid=(bsz, nch),
        in_specs=[pl.BlockSpec((rows, cc), lambda b, c: (b * nch + c, XBC0 // cc)),
                  pl.BlockSpec((rows, width), lambda b, c: (b * nch + c, Z0 // width)),
                  pl.BlockSpec((rows, LANES), lambda b, c: (b * nch + c, DT0 // LANES)),
                  pl.BlockSpec((None, SUBLANES, cc), lambda b, c: (b, 0, 0)),
                  pl.BlockSpec((None, None, hp, SSD_STATE), lambda b, c: (layer, b, 0, 0)),
                  const((SSD_CONV, cc)), const((1, cc)), const((1, LANES)), const((1, LANES)),
                  const((1, width)), const((1, width))],
        out_specs=[pl.BlockSpec((rows, width), lambda b, c: (b * nch + c, 0)),
                   pl.BlockSpec((None, hp, SSD_STATE), lambda b, c: (b, 0, 0))],
        out_shape=[jax.ShapeDtypeStruct((bsz * seq, width), F32),
                   jax.ShapeDtypeStruct((bsz, hp, SSD_STATE), F32)],
        scratch_shapes=[pltpu.VMEM((SSD_CHUNK, cc), F32), pltpu.VMEM((hp, SSD_STATE), F32)],
        compiler_params=_cparams(("arbitrary", "arbitrary"), 32),
        name="ssd",
    )(u, u, u, st8, h0, cw, cb, dtb, alog, dvec, nw)


def _expm1(t):
    u = jnp.exp(t)
    small = (u - 1.0) * t / jnp.log(u)
    return jnp.where(t < -1.0, u - 1.0, jnp.where(u == 1.0, t, small))


def _lru_body(xr_ref, gate_ref, st_ref, h0_ref, cw_ref, cb_ref, wa_ref, ba_ref, wx_ref, bx_ref, lam_ref,
              y_ref, hout_ref, prev_sc, h_sc, *, rows):
    c = pl.program_id(1)

    @pl.when(c == 0)
    def _():
        st = st_ref[...]
        if rows > SUBLANES:
            st = jnp.concatenate([jnp.zeros((rows - SUBLANES, st.shape[1]), F32), st], axis=0)
        prev_sc[...] = st
        h_sc[...] = h0_ref[...]

    x = xr_ref[...]
    xc = _causal_conv(x, prev_sc[...], cw_ref, cb_ref)
    prev_sc[...] = x
    xcb = xc.astype(BF16)
    r = jax.nn.sigmoid(_mm(xcb, wa_ref[...]) + ba_ref[...])
    ig = jax.nn.sigmoid(_mm(xcb, wx_ref[...]) + bx_ref[...])
    log_a = (-LRU_C) * r * jax.nn.softplus(-lam_ref[...])
    a = jnp.exp(log_a)
    b = jnp.sqrt(-_expm1(2.0 * log_a)) * (ig * xc)
    rowi = lax.broadcasted_iota(I32, a.shape, 0)
    d = 1
    while d < rows:
        ok = rowi >= d
        b = jnp.where(ok, a * pltpu.roll(b, d, 0) + b, b)
        a = jnp.where(ok, a * pltpu.roll(a, d, 0), a)
        d *= 2
    h = b + a * h_sc[...]
    h_sc[...] = h[rows - 1:rows, :]
    y_ref[...] = h * jax.nn.gelu(gate_ref[...])

    @pl.when(c == pl.num_programs(1) - 1)
    def _():
        hout_ref[...] = h[rows - 1:rows, :]


def _lru(u, st8, h0, cw, cb, wa, ba, wx, bx, lam, bsz, seq):
    rows = _tile(seq, (128, 64, 32, 16, 8))
    nch = seq // rows
    w = cw.shape[1]
    const = lambda shape: pl.BlockSpec(shape, lambda b, c: (0,) * len(shape))
    return pl.pallas_call(
        functools.partial(_lru_body, rows=rows),
        grid=(bsz, nch),
        in_specs=[pl.BlockSpec((rows, w), lambda b, c: (b * nch + c, XR0 // w)),
                  pl.BlockSpec((rows, w), lambda b, c: (b * nch + c, GATE0 // w)),
                  pl.BlockSpec((None, SUBLANES, w), lambda b, c: (b, 0, 0)),
                  pl.BlockSpec((None, 1, w), lambda b, c: (b, 0, 0)),
                  const((LRU_CONV, w)), const((1, w)), const((w, w)), const((1, w)), const((w, w)), const((1, w)),
                  const((1, w))],
        out_specs=[pl.BlockSpec((rows, w), lambda b, c: (b * nch + c, 0)),
                   pl.BlockSpec((None, 1, w), lambda b, c: (b, 0, 0))],
        out_shape=[jax.ShapeDtypeStruct((bsz * seq, w), F32), jax.ShapeDtypeStruct((bsz, 1, w), F32)],
        scratch_shapes=[pltpu.VMEM((rows, w), F32), pltpu.VMEM((1, w), F32)],
        compiler_params=_cparams(("arbitrary", "arbitrary"), 32),
        name="lru",
    )(u, u, st8, h0, cw, cb, wa, ba, wx, bx, lam)


def _out_proj_body(x_ref, a_ref, s_ref, l_ref, wo_ref, nw_ref, wq_ref, x1_ref, qm_ref):
    wa, ws = a_ref.shape[1], s_ref.shape[1]
    acc = x_ref[...] + _mm(a_ref[...].astype(BF16), wo_ref[0:wa, :])
    acc = acc + _mm(s_ref[...].astype(BF16), wo_ref[wa:wa + ws, :])
    acc = acc + _mm(l_ref[...].astype(BF16), wo_ref[wa + ws:, :])
    x1_ref[...] = acc
    qm_ref[...] = _mm(_rms(acc, nw_ref[...]).astype(BF16), wq_ref[...])


def _out_proj(x, attn, ssd, lru, wo, nw, wq):
    t, d = x.shape
    tm = _tile(t, (256, 128))
    row = lambda w: pl.BlockSpec((tm, w), lambda i: (i, 0))
    const = lambda shape: pl.BlockSpec(shape, lambda i: (0, 0))
    return pl.pallas_call(
        _out_proj_body,
        grid=(t // tm,),
        in_specs=[row(d), row(attn.shape[1]), row(ssd.shape[1]), row(lru.shape[1]),
                  const(wo.shape), const((1, d)), const(wq.shape)],
        out_specs=[row(d), row(wq.shape[1])],
        out_shape=[jax.ShapeDtypeStruct((t, d), F32), jax.ShapeDtypeStruct((t, wq.shape[1]), F32)],
        compiler_params=_cparams(("arbitrary",), 48),
        name="out_proj",
    )(x, attn, ssd, lru, wo, nw, wq)


def _mem_kv_body(m_ref, wk_ref, wv_ref, k_ref, v_ref):
    mb = m_ref[...].astype(BF16)
    k_ref[...] = _mm(mb, wk_ref[...])
    v_ref[...] = _mm(mb, wv_ref[...])


def _mem_kv(mem, wk, wv):
    t, d = mem.shape
    tm = _tile(t, (256, 128))
    w = wk.shape[1]
    return pl.pallas_call(
        _mem_kv_body,
        grid=(t // tm,),
        in_specs=[pl.BlockSpec((tm, d), lambda i: (i, 0)),
                  pl.BlockSpec((d, w), lambda i: (0, 0)), pl.BlockSpec((d, w), lambda i: (0, 0))],
        out_specs=[pl.BlockSpec((tm, w), lambda i: (i, 0)), pl.BlockSpec((tm, w), lambda i: (i, 0))],
        out_shape=[jax.ShapeDtypeStruct((t, w), F32), jax.ShapeDtypeStruct((t, w), F32)],
        compiler_params=_cparams(("arbitrary",), 32),
        name="mem_kv",
    )(mem, wk, wv)


def _xattn_body(q_ref, k_ref, v_ref, o_ref, *, head_rows):
    for h in range(MEM_HEADS):
        sl = slice(h * MEM_DH, (h + 1) * MEM_DH)
        if head_rows:
            rows = pl.ds(h, k_ref.shape[0] // MEM_HEADS, stride=MEM_HEADS)
            k, v = k_ref[rows, :], v_ref[rows, :]
        else:
            k, v = k_ref[:, sl], v_ref[:, sl]
        s = _nt(q_ref[:, sl].astype(BF16), k.astype(BF16)) * (MEM_DH ** -0.5)
        e = jnp.exp(s - jnp.max(s, axis=1, keepdims=True))
        o = _mm(e.astype(BF16), v.astype(BF16))
        o_ref[:, sl] = o / jnp.sum(e, axis=1, keepdims=True)


def _xattn(qm, mem_k, mem_v, bsz, seq, layer=None):
    tq = _tile(seq, (256, 128))
    nq = seq // tq
    w = qm.shape[1]
    if layer is None:
        mem_spec = pl.BlockSpec((None,) + mem_k.shape[1:], lambda b, i: (b, 0, 0))
    else:
        mem_spec = pl.BlockSpec((None, None) + mem_k.shape[2:], lambda b, i: (layer, b, 0, 0))
    return pl.pallas_call(
        functools.partial(_xattn_body, head_rows=layer is not None),
        grid=(bsz, nq),
        in_specs=[pl.BlockSpec((tq, w), lambda b, i: (b * nq + i, 0)), mem_spec, mem_spec],
        out_specs=pl.BlockSpec((tq, w), lambda b, i: (b * nq + i, 0)),
        out_shape=jax.ShapeDtypeStruct(qm.shape, F32),
        compiler_params=_cparams(("arbitrary", "arbitrary"), 32),
        name="xattn",
    )(qm, mem_k, mem_v)


def _route_body(*refs, n_own):
    hf_ref = refs[-3]
    i = pl.program_id(0)

    @pl.when(i < n_own)
    def _():
        _route_tile(*refs)

    @pl.when(i >= n_own)
    def _():
        hf_ref[...] = jnp.zeros_like(hf_ref)


def _route_tile(x1_ref, om_ref, wo_ref, nw_ref, rw_ref, rb_ref, *rest):
    x2_ref, hf_ref, eid_ref, gate_ref = rest[-4:]
    x2 = x1_ref[...] + _mm(om_ref[...].astype(BF16), wo_ref[...])
    x2_ref[...] = x2
    hf = _rms(x2, nw_ref[...])
    hf_ref[...] = hf
    logits = jnp.dot(hf, rw_ref[...], precision=lax.Precision.HIGHEST, preferred_element_type=F32) + rb_ref[...]
    lane = lax.broadcasted_iota(I32, logits.shape, 1).astype(F32)
    ninf = -jnp.inf
    big = float(LANES)

    def first_argmax(v, mx):
        return jnp.min(jnp.where(v == mx, lane, big), axis=1, keepdims=True)

    lg = jnp.where(lane < MOE_GROUPS, logits, ninf)
    mg = jnp.max(lg, axis=1, keepdims=True)
    p_grp = 1.0 / jnp.sum(jnp.exp(lg - mg), axis=1, keepdims=True)
    lo = MOE_GROUPS + MOE_EXPERTS_PER_GROUP * first_argmax(lg, mg)
    le = jnp.where((lane >= lo) & (lane < lo + MOE_EXPERTS_PER_GROUP), logits, ninf)
    v1 = jnp.max(le, axis=1, keepdims=True)
    i1 = first_argmax(le, v1)
    le2 = jnp.where(lane == i1, ninf, le)
    v2 = jnp.max(le2, axis=1, keepdims=True)
    i2 = first_argmax(le2, v2)
    e2 = jnp.exp(v2 - v1)
    den = 1.0 + e2
    g1 = p_grp * (1.0 / den)
    g2 = p_grp * (e2 / den)
    gate_ref[...] = jnp.where(lane == 0.0, g1, jnp.where(lane == 1.0, g2, 0.0))
    eid_ref[...] = jnp.where(lane == 0.0, i1 - MOE_GROUPS, jnp.where(lane == 1.0, i2 - MOE_GROUPS, 0.0)).astype(I32)


def _route(x1, om, wo, nw, rw, rb, hf_rows, hf_row0, hf_shared):
    t, d = x1.shape
    creating = hf_shared is None
    tail = hf_rows - hf_row0 - t if creating else 0
    tm = _tile(math.gcd(t, hf_row0, tail), (256, 128, 64, 32, 16, 8))
    blk0 = hf_row0 // tm
    n_own = t // tm
    n_steps = n_own + tail // tm
    row = lambda w: pl.BlockSpec((tm, w), lambda i: (jnp.minimum(i, n_own - 1), 0))
    const = lambda shape: pl.BlockSpec(shape, lambda i: (0, 0))
    in_specs = [row(d), row(om.shape[1]), const(wo.shape), const((1, d)), const(rw.shape), const((1, LANES))]
    args = [x1, om, wo, nw, rw, rb]
    aliases = {}
    if hf_shared is not None:
        in_specs.append(pl.BlockSpec(memory_space=pl.ANY))
        args.append(hf_shared)
        aliases = {len(args) - 1: 1}
    return pl.pallas_call(
        functools.partial(_route_body, n_own=n_own),
        grid=(n_steps,),
        in_specs=in_specs,
        out_specs=[row(d), pl.BlockSpec((tm, d), lambda i: (blk0 + i, 0)), row(LANES), row(LANES)],
        out_shape=[jax.ShapeDtypeStruct((t, d), F32), jax.ShapeDtypeStruct((hf_rows, d), F32),
                   jax.ShapeDtypeStruct((t, LANES), I32), jax.ShapeDtypeStruct((t, LANES), F32)],
        input_output_aliases=aliases,
        compiler_params=_cparams(("arbitrary",), 40),
        name="route",
    )(*args)


ROW_UNROLL = 8


def _moe_body(grow_ref, srow_ref, be_ref, base_ref, cnt_ref, hf_hbm, wg_ref, wu_ref, wd_ref, o_hbm,
              xbuf, ybuf, wgb, wub, wdb, gsem, ssem, *, rb):
    g = pl.program_id(0)
    ng = pl.num_programs(0)
    slot = lax.rem(g, 2)
    cnt = cnt_ref[g]

    def for_rows(n, fn):
        n_grp = lax.shift_right_logical(n, ROW_UNROLL.bit_length() - 1)

        def grp(q, c):
            for u in range(ROW_UNROLL):
                fn(q * ROW_UNROLL + u)
            return c

        def one(i, c):
            fn(i)
            return c

        lax.fori_loop(0, n_grp, grp, 0)
        lax.fori_loop(n_grp * ROW_UNROLL, n, one, 0)

    def gather_row(sl, i, src):
        return pltpu.make_async_copy(hf_hbm.at[pl.ds(src, 1), :], xbuf.at[sl, pl.ds(i, 1), :], gsem.at[sl])

    def scatter_row(sl, i, dst):
        return pltpu.make_async_copy(ybuf.at[sl, pl.ds(i, 1), :], o_hbm.at[pl.ds(dst, 1), :], ssem.at[sl])

    def start_gathers(blk, sl):
        base = base_ref[blk]
        for_rows(cnt_ref[blk], lambda i: gather_row(sl, i, grow_ref[base + i]).start())

    def start_scatters(blk, sl):
        base = base_ref[blk]
        for_rows(cnt_ref[blk], lambda i: scatter_row(sl, i, srow_ref[base + i]).start())

    def wait_rows(blk, sl, row_copy, block_copy):
        n = cnt_ref[blk]

        @pl.when(n == rb)
        def _():
            block_copy.wait()

        @pl.when(n < rb)
        def _():
            for_rows(n, lambda i: row_copy(sl, i, 0).wait())

    def wait_gathers(blk, sl):
        wait_rows(blk, sl, gather_row,
                  pltpu.make_async_copy(hf_hbm.at[pl.ds(0, rb), :], xbuf.at[sl], gsem.at[sl]))

    def wait_scatters(blk, sl):
        wait_rows(blk, sl, scatter_row,
                  pltpu.make_async_copy(ybuf.at[sl], o_hbm.at[pl.ds(0, rb), :], ssem.at[sl]))

    @pl.when(g == 0)
    def _():
        xbuf[...] = jnp.zeros_like(xbuf)
        start_gathers(0, 0)

    prev_e = be_ref[jnp.maximum(g - 1, 0)]

    @pl.when((cnt > 0) & ((g == 0) | (prev_e != be_ref[g])))
    def _():
        wgb[...] = wg_ref[...].astype(BF16)
        wub[...] = wu_ref[...].astype(BF16)
        wdb[...] = wd_ref[...].astype(BF16)

    wait_gathers(g, slot)

    @pl.when(g + 1 < ng)
    def _():
        start_gathers(g + 1, 1 - slot)

    @pl.when(g >= 2)
    def _():
        wait_scatters(g - 2, slot)

    @pl.when(cnt > 0)
    def _():
        xb = xbuf[slot].astype(BF16)
        hg = _mm(xb, wgb[...])
        act = (hg * jax.nn.sigmoid(hg)) * _mm(xb, wub[...])
        ybuf[slot] = _mm(act.astype(BF16), wdb[...])
        start_scatters(g, slot)

    @pl.when(g == ng - 1)
    def _():
        @pl.when(g >= 1)
        def _():
            wait_scatters(g - 1, 1 - slot)

        wait_scatters(g, slot)


def _moe_ffn(hf, grow, srow, blk_e, blk_base, blk_cnt, wg, wu, wd, layer, rb):
    t, d = hf.shape
    n_asg = grow.shape[0]
    ff = wg.shape[-1]
    n_blk = blk_e.shape[0]
    wspec = lambda shape: pl.BlockSpec((None, None) + shape, lambda g, gr, sr, be, bb, bc: (layer, be[g], 0, 0))
    return pl.pallas_call(
        functools.partial(_moe_body, rb=rb),
        grid_spec=pltpu.PrefetchScalarGridSpec(
            num_scalar_prefetch=5,
            grid=(n_blk,),
            in_specs=[pl.BlockSpec(memory_space=pl.ANY), wspec((d, ff)), wspec((d, ff)), wspec((ff, d))],
            out_specs=pl.BlockSpec(memory_space=pl.ANY),
            scratch_shapes=[pltpu.VMEM((2, rb, d), F32), pltpu.VMEM((2, rb, d), F32),
                            pltpu.VMEM((d, ff), BF16), pltpu.VMEM((d, ff), BF16), pltpu.VMEM((ff, d), BF16),
                            pltpu.SemaphoreType.DMA((2,)), pltpu.SemaphoreType.DMA((2,))]),
        out_shape=jax.ShapeDtypeStruct((n_asg, d), F32),
        compiler_params=_cparams(("arbitrary",), 56),
        name="moe_ffn",
    )(grow, srow, blk_e, blk_base, blk_cnt, hf, wg, wu, wd)


def _moe_plan(eid, n_tok, rb):
    n_asg = n_tok * MOE_TOPK
    flat_e = eid.reshape(-1)
    order = jnp.argsort(flat_e, stable=True).astype(I32)
    grow = order // MOE_TOPK
    srow = (order % MOE_TOPK) * n_tok + grow
    counts = jnp.sum((flat_e[:, None] == jnp.arange(MOE_EXPERTS, dtype=I32)[None, :]).astype(I32), axis=0)
    starts = jnp.cumsum(counts) - counts
    nblk = (counts + rb - 1) // rb
    blk_end = jnp.cumsum(nblk)
    n_blk = -(-n_asg // rb) + MOE_EXPERTS
    gi = jnp.arange(n_blk, dtype=I32)
    be = jnp.minimum(jnp.sum((gi[:, None] >= blk_end[None, :]).astype(I32), axis=1), MOE_EXPERTS - 1)
    r = gi - (blk_end - nblk)[be]
    cnt = jnp.clip(counts[be] - r * rb, 0, rb).astype(I32)
    base = jnp.minimum(starts[be] + r * rb, n_asg - 1).astype(I32)
    return grow.astype(I32), srow.astype(I32), be, base, cnt


def _combine_body(x2_ref, o0_ref, o1_ref, g_ref, fw_ref, x3_ref, *, final):
    g = g_ref[...]
    x3 = x2_ref[...] + (g[:, 0:1] * o0_ref[...] + g[:, 1:2] * o1_ref[...])
    x3_ref[...] = _rms(x3, fw_ref[...]) if final else x3


def _combine(x2, o2, gates, fw, final, row0, n_all):
    t, d = x2.shape
    tm = _tile(math.gcd(t, row0, n_all), (256, 128, 64, 32, 16, 8))
    assert t % tm == 0 and row0 % tm == 0 and n_all % tm == 0
    nt = t // tm
    b0, b1 = row0 // tm, (n_all + row0) // tm
    return pl.pallas_call(
        functools.partial(_combine_body, final=final),
        grid=(nt,),
        in_specs=[pl.BlockSpec((tm, d), lambda i: (i, 0)),
                  pl.BlockSpec((tm, d), lambda i: (b0 + i, 0)),
                  pl.BlockSpec((tm, d), lambda i: (b1 + i, 0)),
                  pl.BlockSpec((tm, LANES), lambda i: (i, 0)),
                  pl.BlockSpec((1, d), lambda i: (0, 0))],
        out_specs=pl.BlockSpec((tm, d), lambda i: (i, 0)),
        out_shape=jax.ShapeDtypeStruct((t, d), F32),
        compiler_params=_cparams(("arbitrary",), 32),
        name="combine",
    )(x2, o2, o2, gates, fw)


def _rope_tables(pos):
    half = ATTN_DH // 2
    inv = ROPE_THETA ** (-jnp.arange(half, dtype=F32) / half)
    ang = pos.astype(F32)[:, None] * inv[None, :]
    cos, sin = jnp.cos(ang), jnp.sin(ang)
    reps = LANES // ATTN_DH
    return jnp.tile(jnp.concatenate([cos, cos], axis=1), (1, reps)), jnp.tile(jnp.concatenate([-sin, sin], axis=1), (1, reps))


def _block_diag(w):
    g, a, b = w.shape
    eye = jnp.eye(g, dtype=w.dtype)
    return (w[:, :, None, :] * eye[:, None, :, None]).reshape(g * a, g * b)


def _pad_lanes(v, n):
    return jnp.pad(v.reshape(1, -1), ((0, 0), (0, n - v.size)))


def _layer_front(x, grp, lw, layer, lam_init, hf_rows, hf_row0, hf_shared):
    bsz, seq = grp["bsz"], grp["seq"]
    u = _in_proj(x, lw["norm_mix"], lw["w_in"], grp["cos"], grp["sin"])
    lams = lw["lams"]
    if grp["paged"]:
        attn = _attn_sample(u, grp["cache_k"], grp["cache_v"], grp["page_flat"], layer, grp["n_pool"], lams,
                            lw["subln"], bsz, seq, grp["n_pages"], lam_init)
        k_new = v_new = None
    else:
        attn, k_new, v_new = _attn_prompt(u, lams, lw["subln"], bsz, seq, lam_init)
    ssd, h_ssd = _ssd(u, grp["ssd_conv"][layer], grp["ssd_state"], layer, lw["ssd_conv_w"], lw["ssd_conv_b"],
                      lw["ssd_dt_bias"], lw["ssd_a_log"], lw["ssd_d"], lw["ssd_norm"], bsz, seq)
    lru, h_lru = _lru(u, grp["lru_conv"][layer], grp["lru_state"][layer], lw["lru_conv_w"], lw["lru_conv_b"],
                      lw["lru_wa"], lw["lru_ba"], lw["lru_wx"], lw["lru_bx"], lw["lru_lambda"], bsz, seq)
    x1, qm = _out_proj(x, attn, ssd, lru, lw["w_out"], lw["norm_mem"], lw["wq_mem"])
    if grp["paged"]:
        om = _xattn(qm, grp["mem_k"], grp["mem_v"], bsz, seq, layer)
    else:
        om = _xattn(qm, grp["mem_k"], grp["mem_v"], bsz, seq)
    routed = _route(x1, om, lw["wo_mem"], lw["norm_ffn"], lw["router_w"], lw["router_b"],
                    hf_rows, hf_row0, hf_shared)
    ur = u.reshape(bsz, seq, UW)
    width_kv = ATTN_KV_HEADS * ATTN_VD
    kv_shape = (bsz, seq, ATTN_KV_HEADS, ATTN_VD)
    outs = dict(
        k=(ur[:, :, K0:K0 + width_kv] if k_new is None else k_new).reshape(kv_shape),
        v=(ur[:, :, V0:V0 + width_kv] if v_new is None else v_new).reshape(kv_shape),
        ssd_conv=ur[:, seq - (SSD_CONV - 1):, XBC0:Z0],
        ssd_state=h_ssd.reshape(bsz, SSD_HEADS, SSD_HEAD_DIM, SSD_STATE),
        lru_conv=ur[:, seq - (LRU_CONV - 1):, XR0:GATE0],
        lru_state=h_lru.reshape(bsz, -1),
    )
    return routed, outs


def _moe_groups(routed, lw, layer, final, final_norm):
    hf = routed[-1][1]
    eid = jnp.concatenate([r[2][:, :MOE_TOPK] for r in routed], axis=0)
    n_tok = hf.shape[0]
    rb = 256 if n_tok * MOE_TOPK >= 256 * MOE_EXPERTS else 128
    plan = _moe_plan(eid, n_tok, rb)
    o2 = _moe_ffn(hf, *plan, lw["moe_w_gate"], lw["moe_w_up"], lw["moe_w_down"], layer, rb)
    outs, row0 = [], 0
    for x2, _, _, gate in routed:
        outs.append(_combine(x2, o2, gate, final_norm, final, row0, n_tok))
        row0 += x2.shape[0]
    return outs


def _conv_state8(st):
    return jnp.pad(st, ((0, 0), (0, 0), (SUBLANES - st.shape[2], 0), (0, 0)))


def kernel(x_prompt, x_sample, cache_k, cache_v, cache_mem_k, cache_mem_v, state_ssd_conv, state_ssd, state_lru_conv, state_lru, page_table, mem_prompt, norm_mix, w_in, attn_lambda_q1, attn_lambda_k1, attn_lambda_q2, attn_lambda_k2, attn_subln, ssd_conv_w, ssd_conv_b, ssd_dt_bias, ssd_a_log, ssd_d, ssd_norm, lru_conv_w, lru_conv_b, lru_wa, lru_ba, lru_wx, lru_bx, lru_lambda, w_out, norm_mem, wq_mem, wk_mem, wv_mem, wo_mem, norm_ffn, router_group_w, router_group_b, router_expert_w, router_expert_b, moe_w_gate, moe_w_up, moe_w_down, final_norm):
    depth = w_in.shape[0]
    bp, tp, d = x_prompt.shape
    bs, ts, _ = x_sample.shape
    n_pool, page = cache_k.shape[1], cache_k.shape[2]
    n_pages = page_table.shape[1]
    past_len = n_pages * page
    n_mem = mem_prompt.shape[1]
    ssd_cc = state_ssd_conv.shape[-1]
    lru_w = state_lru.shape[-1]
    kvw = ATTN_KV_HEADS * ATTN_VD
    memw = MEM_HEADS * MEM_DH

    cos_p, sin_p = _rope_tables(jnp.tile(jnp.arange(tp, dtype=I32), bp))
    cos_s, sin_s = _rope_tables(jnp.tile(past_len + jnp.arange(ts, dtype=I32), bs))

    prompt = dict(bsz=bp, seq=tp, paged=False, cos=cos_p, sin=sin_p,
                  ssd_conv=jnp.zeros((depth, bp, SUBLANES, ssd_cc), F32),
                  ssd_state=jnp.zeros((depth, bp, SSD_HEADS * SSD_HEAD_DIM, SSD_STATE), F32),
                  lru_conv=jnp.zeros((depth, bp, SUBLANES, lru_w), F32),
                  lru_state=jnp.zeros((depth, bp, 1, lru_w), F32))
    sample = dict(bsz=bs, seq=ts, paged=True, cos=cos_s, sin=sin_s,
                  cache_k=cache_k.reshape(depth, n_pool, page * ATTN_KV_HEADS, ATTN_VD),
                  cache_v=cache_v.reshape(depth, n_pool, page * ATTN_KV_HEADS, ATTN_VD),
                  page_flat=page_table.reshape(-1), n_pool=n_pool, n_pages=n_pages,
                  ssd_conv=_conv_state8(state_ssd_conv),
                  ssd_state=state_ssd.reshape(depth, bs, SSD_HEADS * SSD_HEAD_DIM, SSD_STATE),
                  lru_conv=_conv_state8(state_lru_conv),
                  lru_state=state_lru.reshape(depth, bs, 1, lru_w),
                  mem_k=cache_mem_k.reshape(depth, bs, n_mem * MEM_HEADS, MEM_DH),
                  mem_v=cache_mem_v.reshape(depth, bs, n_mem * MEM_HEADS, MEM_DH))

    xp = x_prompt.reshape(bp * tp, d)
    xs = x_sample.reshape(bs * ts, d)
    po, so, mks, mvs = [], [], [], []
    for l in range(depth):
        w = w_in[l]
        zc = V0 + kvw
        xc0 = zc + SSD_HEADS * SSD_HEAD_DIM
        dc = xc0 + ssd_cc
        rc = dc + SSD_HEADS
        w_pad = jnp.concatenate([w[:, :zc], w[:, xc0:dc], w[:, zc:xc0], w[:, rc:rc + 2 * lru_w], w[:, dc:rc],
                                 jnp.zeros((d, UW - DT0 - SSD_HEADS), F32)], axis=1).astype(BF16)
        router_w = jnp.pad(jnp.concatenate([router_group_w[l], router_expert_w[l]], axis=1),
                           ((0, 0), (0, LANES - MOE_GROUPS - MOE_EXPERTS)))
        router_b = _pad_lanes(jnp.concatenate([router_group_b[l], router_expert_b[l]]), LANES)
        lw = dict(
            norm_mix=norm_mix[l].reshape(1, d), w_in=w_pad,
            lams=[v[l].reshape(1, ATTN_DH) for v in (attn_lambda_q1, attn_lambda_k1, attn_lambda_q2, attn_lambda_k2)],
            subln=attn_subln[l].reshape(1, ATTN_VD),
            ssd_conv_w=ssd_conv_w[l], ssd_conv_b=ssd_conv_b[l].reshape(1, -1),
            ssd_dt_bias=_pad_lanes(ssd_dt_bias[l], LANES), ssd_a_log=_pad_lanes(ssd_a_log[l], LANES),
            ssd_d=jnp.repeat(ssd_d[l], SSD_HEAD_DIM).reshape(1, -1), ssd_norm=ssd_norm[l].reshape(1, -1),
            lru_conv_w=lru_conv_w[l], lru_conv_b=lru_conv_b[l].reshape(1, -1),
            lru_wa=_block_diag(lru_wa[l]).astype(BF16), lru_ba=lru_ba[l].reshape(1, -1),
            lru_wx=_block_diag(lru_wx[l]).astype(BF16), lru_bx=lru_bx[l].reshape(1, -1),
            lru_lambda=lru_lambda[l].reshape(1, -1),
            w_out=w_out[l].astype(BF16), norm_mem=norm_mem[l].reshape(1, d), wq_mem=wq_mem[l].astype(BF16),
            wo_mem=wo_mem[l].astype(BF16), norm_ffn=norm_ffn[l].reshape(1, d),
            router_w=router_w, router_b=router_b,
            moe_w_gate=moe_w_gate, moe_w_up=moe_w_up, moe_w_down=moe_w_down,
        )
        lam_init = 0.8 - 0.6 * math.exp(-0.3 * l)
        mk, mv = _mem_kv(mem_prompt.reshape(bp * n_mem, d), wk_mem[l].astype(BF16), wv_mem[l].astype(BF16))
        mks.append(mk.reshape(bp, n_mem, MEM_HEADS, MEM_DH))
        mvs.append(mv.reshape(bp, n_mem, MEM_HEADS, MEM_DH))
        pg = dict(prompt, mem_k=mk.reshape(bp, n_mem, memw), mem_v=mv.reshape(bp, n_mem, memw))
        n_all = xp.shape[0] + xs.shape[0]
        routed_p, o = _layer_front(xp, pg, lw, l, lam_init, n_all, 0, None)
        po.append(o)
        routed_s, o = _layer_front(xs, sample, lw, l, lam_init, n_all, xp.shape[0], routed_p[1])
        so.append(o)
        xp, xs = _moe_groups([routed_p, routed_s], lw, l, l == depth - 1, final_norm.reshape(1, d))

    st = lambda outs, key: jnp.stack([o[key] for o in outs])
    return (xp.reshape(bp, tp, d), xs.reshape(bs, ts, d),
            st(po, "k"), st(po, "v"), jnp.stack(mks), jnp.stack(mvs),
            st(po, "ssd_conv"), st(po, "ssd_state"), st(po, "lru_conv"), st(po, "lru_state"),
            st(so, "k"), st(so, "v"), st(so, "ssd_conv"), st(so, "ssd_state"), st(so, "lru_conv"), st(so, "lru_state"))
```

```python
import functools
import math

import jax
import jax.numpy as jnp
from jax import lax
from jax.experimental import pallas as pl
from jax.experimental.pallas import tpu as pltpu

F32, BF16, I32 = jnp.float32, jnp.bfloat16, jnp.int32
EPS = 1e-6
LANES = 128
SUBLANES = 8
MIB = 1024 * 1024

ATTN_HEADS, ATTN_KV_HEADS, ATTN_DH = 8, 4, 64
ATTN_VD = 2 * ATTN_DH
ROPE_THETA = 10000.0
SSD_HEADS, SSD_HEAD_DIM, SSD_GROUPS, SSD_STATE, SSD_CONV = 8, 64, 2, 128, 4
SSD_CHUNK = 128
LRU_BLOCKS, LRU_CONV, LRU_C = 8, 4, 8.0
MEM_HEADS, MEM_DH = 4, 128
MOE_GROUPS, MOE_EXPERTS_PER_GROUP, MOE_TOPK = 4, 8, 2
MOE_EXPERTS = MOE_GROUPS * MOE_EXPERTS_PER_GROUP

Q0, K0, V0, XBC0, Z0, XR0, GATE0, DT0, UW = 0, 1024, 1536, 2048, 3072, 3584, 4096, 4608, 5120
PROJ_TN = 1024
PAGE_GROUP = 16


def _cparams(sem, vmem_mib):
    return pltpu.CompilerParams(dimension_semantics=sem, vmem_limit_bytes=vmem_mib * MIB)


def _tile(n, prefs):
    for p in prefs:
        if n % p == 0:
            return p
    return n


def _rms(x, w):
    return (x * lax.rsqrt(jnp.mean(x * x, axis=-1, keepdims=True) + EPS)) * w


def _nt(a, b):
    return lax.dot_general(a, b, (((1,), (1,)), ((), ())), preferred_element_type=F32)


def _mm(a, b):
    return jnp.dot(a, b, preferred_element_type=F32)


def _rope128(yc, cos, sin):
    lane = lax.broadcasted_iota(I32, yc.shape, 1)
    sw = jnp.where((lane % 64) < 32, pltpu.roll(yc, 96, 1), pltpu.roll(yc, 32, 1))
    return yc * cos + sw * sin


def _in_proj_body(x_ref, nw_ref, w_ref, cos_ref, sin_ref, o_ref):
    hb = _rms(x_ref[...], nw_ref[...]).astype(BF16)
    cos, sin = cos_ref[...], sin_ref[...]
    rope_end = V0
    for c0 in range(0, UW, PROJ_TN):
        y = _mm(hb, w_ref[:, c0:c0 + PROJ_TN])
        for c in range(c0, c0 + PROJ_TN, LANES):
            yc = y[:, c - c0:c - c0 + LANES]
            o_ref[:, c:c + LANES] = _rope128(yc, cos, sin) if c < rope_end else yc


def _in_proj(x, nw, w_pad, cos, sin):
    t, d = x.shape
    tm = _tile(t, (256, 128))
    return pl.pallas_call(
        _in_proj_body,
        grid=(t // tm,),
        in_specs=[pl.BlockSpec((tm, d), lambda i: (i, 0)),
                  pl.BlockSpec((1, d), lambda i: (0, 0)),
                  pl.BlockSpec((d, UW), lambda i: (0, 0), pipeline_mode=pl.Buffered(1)),
                  pl.BlockSpec((tm, LANES), lambda i: (i, 0)),
                  pl.BlockSpec((tm, LANES), lambda i: (i, 0))],
        out_specs=pl.BlockSpec((tm, UW), lambda i: (i, 0)),
        out_shape=jax.ShapeDtypeStruct((t, UW), F32),
        compiler_params=_cparams(("arbitrary",), 52),
        name="in_proj",
    )(x, nw, w_pad, cos, sin)


def _lambda(lq1, lk1, lq2, lk2, lam_init):
    s1 = jnp.sum(lq1[...] * lk1[...], axis=-1, keepdims=True)
    s2 = jnp.sum(lq2[...] * lk2[...], axis=-1, keepdims=True)
    return jnp.exp(s1) - jnp.exp(s2) + lam_init


def _attn_prompt_body(lq1, lk1, lq2, lk2, sub_ref, q_ref, k_ref, v_ref, o_ref, ko_ref, vo_ref, acc_sc,
                      *, tq, tk, lam_init):
    i = pl.program_id(2)
    lam = _lambda(lq1, lk1, lq2, lk2, lam_init)

    @pl.when(i == 0)
    def _():
        h = pl.program_id(1)
        rows = pl.ds(h, k_ref.shape[0], stride=ATTN_KV_HEADS)
        ko_ref[rows, :] = k_ref[...]
        vo_ref[rows, :] = v_ref[...]

    q = q_ref[...] * (ATTN_DH ** -0.5 * math.log2(math.e))
    lane = lax.broadcasted_iota(I32, (tq, LANES), 1)
    n4 = 4 * tq
    parts = []
    for g in range(2):
        qh = q[:, g * LANES:(g + 1) * LANES]
        parts.append(jnp.where(lane < ATTN_DH, qh, 0.0))
        parts.append(jnp.where(lane >= ATTN_DH, qh, 0.0))
    q4 = jnp.concatenate(parts, axis=0).astype(BF16)
    acc_sc[...] = jnp.zeros_like(acc_sc)

    def chunk(c, stats, masked):
        m_old, l_old = stats
        start = pl.multiple_of(c * tk, tk)
        k = k_ref[pl.ds(start, tk), :].astype(BF16)
        v = v_ref[pl.ds(start, tk), :].astype(BF16)
        st = _nt(k, q4)
        if masked:
            key = start + lax.broadcasted_iota(I32, (tk, n4), 0)
            t = i * tq + jnp.bitwise_and(lax.broadcasted_iota(I32, (tk, n4), 1), tq - 1)
            st = jnp.where(key <= t, st, -jnp.inf)
        m_new = jnp.maximum(m_old, jnp.max(st, axis=0, keepdims=True))
        alpha = jnp.exp2(m_old - m_new)
        p = jnp.exp2(st - m_new)
        pv = lax.dot_general(v, p.astype(BF16), (((0,), (0,)), ((), ())), preferred_element_type=F32)
        acc_sc[...] = alpha * acc_sc[...] + pv
        return m_new, alpha * l_old + jnp.sum(p, axis=0, keepdims=True)

    last = (i * tq) // tk
    stats = (jnp.full((1, n4), -jnp.inf, F32), jnp.zeros((1, n4), F32))
    stats = lax.fori_loop(0, last, lambda c, s: chunk(c, s, False), stats)
    _, l = chunk(last, stats, True)
    ot = acc_sc[...] / l
    for g in range(2):
        og = ot[:, (2 * g) * tq:(2 * g + 1) * tq] - lam * ot[:, (2 * g + 1) * tq:(2 * g + 2) * tq]
        ms = jnp.mean(og * og, axis=0, keepdims=True)
        nrm = (og * lax.rsqrt(ms + EPS)) * sub_ref[...] * (1.0 - lam_init)
        o_ref[:, g * LANES:(g + 1) * LANES] = nrm.T


def _attn_prompt(u, lams, subln, bsz, seq, lam_init):
    tq = _tile(seq, (256, 128))
    tk = 2 * tq if seq % (2 * tq) == 0 else tq
    nq = seq // tq
    vec = pl.BlockSpec((1, ATTN_DH), lambda b, h, i: (0, 0))
    return pl.pallas_call(
        functools.partial(_attn_prompt_body, tq=tq, tk=tk, lam_init=lam_init),
        grid=(bsz, ATTN_KV_HEADS, nq),
        in_specs=[vec, vec, vec, vec,
                  pl.BlockSpec((ATTN_VD, 1), lambda b, h, i: (0, 0)),
                  pl.BlockSpec((tq, 2 * LANES), lambda b, h, i: (b * nq + i, h)),
                  pl.BlockSpec((seq, LANES), lambda b, h, i: (b, K0 // LANES + h)),
                  pl.BlockSpec((seq, LANES), lambda b, h, i: (b, V0 // LANES + h))],
        out_specs=[pl.BlockSpec((tq, 2 * LANES), lambda b, h, i: (b * nq + i, h)),
                   pl.BlockSpec((seq * ATTN_KV_HEADS, ATTN_VD), lambda b, h, i: (b, 0)),
                   pl.BlockSpec((seq * ATTN_KV_HEADS, ATTN_VD), lambda b, h, i: (b, 0))],
        out_shape=[jax.ShapeDtypeStruct((bsz * seq, ATTN_HEADS * ATTN_VD), F32),
                   jax.ShapeDtypeStruct((bsz * seq * ATTN_KV_HEADS, ATTN_VD), F32),
                   jax.ShapeDtypeStruct((bsz * seq * ATTN_KV_HEADS, ATTN_VD), F32)],
        scratch_shapes=[pltpu.VMEM((ATTN_VD, 4 * tq), F32)],
        compiler_params=_cparams(("arbitrary", "arbitrary", "arbitrary"), 56),
        name="attn_prompt",
    )(*lams, subln.reshape(ATTN_VD, 1), u, u, u)


def _attn_sample_body(pt_ref, lq1, lk1, lq2, lk2, sub_ref, q_ref, kn_ref, vn_ref, *rest, n_pg, lam_init, tdec):
    del pt_ref
    k_refs, v_refs = rest[:n_pg], rest[n_pg:2 * n_pg]
    o_ref, m_sc, l_sc, acc_sc, qr_sc = rest[2 * n_pg:]
    j = pl.program_id(1)
    nrow = ATTN_KV_HEADS * 2 * 2 * tdec
    page_tok = k_refs[0].shape[0] // ATTN_KV_HEADS

    @pl.when(j == 0)
    def _():
        q = q_ref[...] * (ATTN_DH ** -0.5)
        lane = lax.broadcasted_iota(I32, (tdec, LANES), 1)
        qr_sc[...] = jnp.zeros_like(qr_sc)
        for kv in range(ATTN_KV_HEADS):
            for c in range(2):
                for g in range(2):
                    hd = kv * 2 + g
                    qh = q[:, hd * LANES:(hd + 1) * LANES]
                    n0 = ((kv * 2 + c) * 2 + g) * tdec
                    qr_sc[n0:n0 + tdec, kv * LANES:(kv + 1) * LANES] = jnp.where(
                        (lane >= c * ATTN_DH) & (lane < (c + 1) * ATTN_DH), qh, 0.0)
        m_sc[...] = jnp.full_like(m_sc, -jnp.inf)
        l_sc[...] = jnp.zeros_like(l_sc)
        acc_sc[...] = jnp.zeros_like(acc_sc)

    qr = qr_sc[...].astype(BF16)

    def page(ref):
        return jnp.concatenate([ref[pl.ds(kv, page_tok, stride=ATTN_KV_HEADS), :] for kv in range(ATTN_KV_HEADS)],
                               axis=1).astype(BF16)

    def update(s_list, v_list):
        m_old = m_sc[...]
        m_new = m_old
        for s in s_list:
            m_new = jnp.maximum(m_new, jnp.max(s, axis=1, keepdims=True))
        alpha = jnp.exp(m_old - m_new)
        l = alpha * l_sc[...]
        pv = None
        for s, v in zip(s_list, v_list):
            p = jnp.exp(s - m_new)
            l = l + jnp.sum(p, axis=1, keepdims=True)
            d = _mm(p.astype(BF16), v)
            pv = d if pv is None else pv + d
        m_sc[...] = m_new
        l_sc[...] = l
        acc_sc[...] = alpha * acc_sc[...] + pv

    for g0 in range(0, n_pg, PAGE_GROUP):
        grp = range(g0, min(g0 + PAGE_GROUP, n_pg))
        update([_nt(qr, page(k_refs[i])) for i in grp], [page(v_refs[i]) for i in grp])

    @pl.when(j == pl.num_programs(1) - 1)
    def _():
        lam = _lambda(lq1, lk1, lq2, lk2, lam_init)
        pad = jnp.zeros((LANES - tdec, kn_ref.shape[1]), F32)
        kn = jnp.concatenate([kn_ref[...], pad], axis=0).astype(BF16)
        vn = jnp.concatenate([vn_ref[...], pad], axis=0).astype(BF16)
        s = _nt(qr, kn)
        row = jnp.bitwise_and(lax.broadcasted_iota(I32, (nrow, LANES), 0), tdec - 1)
        col = lax.broadcasted_iota(I32, (nrow, LANES), 1)
        update([jnp.where(col <= row, s, -jnp.inf)], [vn])
        o = acc_sc[...] / l_sc[...]
        for kv in range(ATTN_KV_HEADS):
            for g in range(2):
                n0 = ((kv * 2 + 0) * 2 + g) * tdec
                n1 = ((kv * 2 + 1) * 2 + g) * tdec
                og = (o[n0:n0 + tdec, kv * LANES:(kv + 1) * LANES]
                      - lam * o[n1:n1 + tdec, kv * LANES:(kv + 1) * LANES])
                hd = kv * 2 + g
                o_ref[:, hd * LANES:(hd + 1) * LANES] = _rms(og, sub_ref[...]) * (1.0 - lam_init)


def _attn_sample(u, cache_k, cache_v, page_flat, layer, n_pool, lams, subln, bsz, tdec, n_pages, lam_init):
    del n_pool
    n_pg = _tile(n_pages, (32, 16, 8, 4, 2, 1))
    nchunks = n_pages // n_pg
    prow = cache_k.shape[2]
    width = ATTN_KV_HEADS * LANES
    nrow = ATTN_KV_HEADS * 2 * 2 * tdec
    assert nrow == LANES and tdec == SUBLANES
    vec = pl.BlockSpec((1, ATTN_DH), lambda b, j, pt: (0, 0))

    def page_spec(i):
        return pl.BlockSpec((None, None, prow, ATTN_VD),
                            lambda b, j, pt: (layer, pt[b * n_pages + j * n_pg + i], 0, 0))

    in_specs = [vec, vec, vec, vec,
                pl.BlockSpec((1, ATTN_VD), lambda b, j, pt: (0, 0)),
                pl.BlockSpec((tdec, ATTN_HEADS * ATTN_VD), lambda b, j, pt: (b, 0)),
                pl.BlockSpec((tdec, width), lambda b, j, pt: (b, K0 // width)),
                pl.BlockSpec((tdec, width), lambda b, j, pt: (b, V0 // width))]
    in_specs += [page_spec(i) for i in range(n_pg)] * 2
    return pl.pallas_call(
        functools.partial(_attn_sample_body, n_pg=n_pg, lam_init=lam_init, tdec=tdec),
        grid_spec=pltpu.PrefetchScalarGridSpec(
            num_scalar_prefetch=1,
            grid=(bsz, nchunks),
            in_specs=in_specs,
            out_specs=pl.BlockSpec((tdec, ATTN_HEADS * ATTN_VD), lambda b, j, pt: (b, 0)),
            scratch_shapes=[pltpu.VMEM((nrow, 1), F32), pltpu.VMEM((nrow, 1), F32),
                            pltpu.VMEM((nrow, width), F32), pltpu.VMEM((nrow, width), F32)]),
        out_shape=jax.ShapeDtypeStruct((bsz * tdec, ATTN_HEADS * ATTN_VD), F32),
        compiler_params=_cparams(("arbitrary", "arbitrary"), 56),
        name="attn_sample",
    )(page_flat, *lams, subln, u, u, u, *([cache_k] * n_pg), *([cache_v] * n_pg))


def _causal_conv(x, prev, cw_ref, cb_ref):
    taps = cw_ref.shape[0]
    rowi = lax.broadcasted_iota(I32, x.shape, 0)
    acc = cb_ref[...] + x * cw_ref[taps - 1:taps, :]
    for s in range(1, taps):
        sh = jnp.where(rowi < s, pltpu.roll(prev, s, 0), pltpu.roll(x, s, 0))
        acc = acc + sh * cw_ref[taps - 1 - s:taps - s, :]
    return acc


def _pad_rows(x, rows):
    if x.shape[0] == rows:
        return x
    return jnp.concatenate([x, jnp.zeros((rows - x.shape[0], x.shape[1]), x.dtype)], axis=0)


def _ssd_body(xbc_ref, z_ref, dt_ref, st_ref, h0_ref, cw_ref, cb_ref, dtb_ref, alog_ref, dvec_ref, nw_ref,
              y_ref, hout_ref, prev_sc, h_sc, *, rows):
    c = pl.program_id(1)
    L = SSD_CHUNK
    width = SSD_HEADS * SSD_HEAD_DIM
    gw = SSD_GROUPS * SSD_STATE

    @pl.when(c == 0)
    def _():
        prev_sc[...] = jnp.concatenate([jnp.zeros((L - SUBLANES, prev_sc.shape[1]), F32), st_ref[...]], axis=0)
        h_sc[...] = h0_ref[...]

    x = _pad_rows(xbc_ref[...], L)
    conv = _causal_conv(x, prev_sc[...], cw_ref, cb_ref)
    prev_sc[...] = x
    xc = conv * jax.nn.sigmoid(conv)
    xs, bm, cm = xc[:, :width], xc[:, width:width + gw], xc[:, width + gw:]

    dt = jax.nn.softplus(_pad_rows(dt_ref[...], L) + dtb_ref[...])
    if rows < L:
        dt = jnp.where(lax.broadcasted_iota(I32, dt.shape, 0) < rows, dt, 0.0)
    da = dt * (-jnp.exp(alog_ref[...]))
    r0 = lax.broadcasted_iota(I32, (L, L), 0)
    c0 = lax.broadcasted_iota(I32, (L, L), 1)
    causal = r0 >= c0
    acum = jnp.dot(causal.astype(F32), da, precision=lax.Precision.HIGHEST, preferred_element_type=F32)
    acum_t, dt_t = acum.T, dt.T
    last = acum[L - 1:L, :]
    wend = jnp.exp(last - acum) * dt
    eac = jnp.exp(acum)
    elast = jnp.exp(last)
    lane = lax.broadcasted_iota(I32, (L, LANES), 1)
    first = lane < SSD_HEAD_DIM
    top = lax.broadcasted_iota(I32, (2 * SSD_HEAD_DIM, SSD_STATE), 0) < SSD_HEAD_DIM

    ys = []
    for pr in range(SSD_HEADS // 2):
        g = (2 * pr) // (SSD_HEADS // SSD_GROUPS)
        bg = bm[:, g * SSD_STATE:(g + 1) * SSD_STATE].astype(BF16)
        cg = cm[:, g * SSD_STATE:(g + 1) * SSD_STATE].astype(BF16)
        gmat = _nt(cg, bg)
        xp = xs[:, pr * LANES:(pr + 1) * LANES]
        xpb = xp.astype(BF16)
        hp = h_sc[pr * LANES:(pr + 1) * LANES, :]
        outs = []
        for hh in (2 * pr, 2 * pr + 1):
            seg = acum[:, hh:hh + 1] - acum_t[hh:hh + 1, :]
            dec = jnp.exp(jnp.where(causal, seg, -jnp.inf))
            w = gmat * dec * dt_t[hh:hh + 1, :]
            outs.append(_mm(w.astype(BF16), xpb))
        h_a, h_b = 2 * pr, 2 * pr + 1
        y_intra = jnp.where(first, outs[0], outs[1])
        e_pair = jnp.where(first, eac[:, h_a:h_a + 1], eac[:, h_b:h_b + 1])
        ys.append(y_intra + _nt(cg, hp.astype(BF16)) * e_pair)
        w_pair = jnp.where(first, wend[:, h_a:h_a + 1], wend[:, h_b:h_b + 1])
        upd = _mm((xp * w_pair).T.astype(BF16), bg)
        keep = jnp.where(top, elast[:, h_a:h_a + 1], elast[:, h_b:h_b + 1])
        h_sc[pr * LANES:(pr + 1) * LANES, :] = hp * keep + upd

    y = jnp.concatenate(ys, axis=1) + dvec_ref[...] * xs
    zz = _pad_rows(z_ref[...], L)
    gated = y * (zz * jax.nn.sigmoid(zz))
    y_ref[...] = _rms(gated, nw_ref[...])[:rows]

    @pl.when(c == pl.num_programs(1) - 1)
    def _():
        hout_ref[...] = h_sc[...]


def _ssd(u, st8, h0, layer, cw, cb, dtb, alog, dvec, nw, bsz, seq):
    rows = SSD_CHUNK if seq % SSD_CHUNK == 0 else seq
    assert rows == SSD_CHUNK or (rows == seq and rows % SUBLANES == 0 and rows <= SSD_CHUNK)
    nch = seq // rows
    cc = cw.shape[1]
    width = SSD_HEADS * SSD_HEAD_DIM
    hp = SSD_HEADS * SSD_HEAD_DIM
    const = lambda shape: pl.BlockSpec(shape, lambda b, c: (0,) * len(shape))
    return pl.pallas_call(
        functools.partial(_ssd_body, rows=rows),
        grid=(bsz, nch),
        in_specs=[pl.BlockSpec((rows, cc), lambda b, c: (b * nch + c, XBC0 // cc)),
                  pl.BlockSpec((rows, width), lambda b, c: (b * nch + c, Z0 // width)),
                  pl.BlockSpec((rows, LANES), lambda b, c: (b * nch + c, DT0 // LANES)),
                  pl.BlockSpec((None, SUBLANES, cc), lambda b, c: (b, 0, 0)),
                  pl.BlockSpec((None, None, hp, SSD_STATE), lambda b, c: (layer, b, 0, 0)),
                  const((SSD_CONV, cc)), const((1, cc)), const((1, LANES)), const((1, LANES)),
                  const((1, width)), const((1, width))],
        out_specs=[pl.BlockSpec((rows, width), lambda b, c: (b * nch + c, 0)),
                   pl.BlockSpec((None, hp, SSD_STATE), lambda b, c: (b, 0, 0))],
        out_shape=[jax.ShapeDtypeStruct((bsz * seq, width), F32),
                   jax.ShapeDtypeStruct((bsz, hp, SSD_STATE), F32)],
        scratch_shapes=[pltpu.VMEM((SSD_CHUNK, cc), F32), pltpu.VMEM((hp, SSD_STATE), F32)],
        compiler_params=_cparams(("arbitrary", "arbitrary"), 32),
        name="ssd",
    )(u, u, u, st8, h0, cw, cb, dtb, alog, dvec, nw)


def _expm1(t):
    u = jnp.exp(t)
    small = (u - 1.0) * t / jnp.log(u)
    return jnp.where(t < -1.0, u - 1.0, jnp.where(u == 1.0, t, small))


def _lru_body(xr_ref, gate_ref, st_ref, h0_ref, cw_ref, cb_ref, wa_ref, ba_ref, wx_ref, bx_ref, lam_ref,
              y_ref, hout_ref, prev_sc, h_sc, *, rows):
    c = pl.program_id(1)

    @pl.when(c == 0)
    def _():
        st = st_ref[...]
        if rows > SUBLANES:
            st = jnp.concatenate([jnp.zeros((rows - SUBLANES, st.shape[1]), F32), st], axis=0)
        prev_sc[...] = st
        h_sc[...] = h0_ref[...]

    x = xr_ref[...]
    xc = _causal_conv(x, prev_sc[...], cw_ref, cb_ref)
    prev_sc[...] = x
    xcb = xc.astype(BF16)
    r = jax.nn.sigmoid(_mm(xcb, wa_ref[...]) + ba_ref[...])
    ig = jax.nn.sigmoid(_mm(xcb, wx_ref[...]) + bx_ref[...])
    log_a = (-LRU_C) * r * jax.nn.softplus(-lam_ref[...])
    a = jnp.exp(log_a)
    b = jnp.sqrt(-_expm1(2.0 * log_a)) * (ig * xc)
    rowi = lax.broadcasted_iota(I32, a.shape, 0)
    d = 1
    while d < rows:
        ok = rowi >= d
        b = jnp.where(ok, a * pltpu.roll(b, d, 0) + b, b)
        a = jnp.where(ok, a * pltpu.roll(a, d, 0), a)
        d *= 2
    h = b + a * h_sc[...]
    h_sc[...] = h[rows - 1:rows, :]
    y_ref[...] = h * jax.nn.gelu(gate_ref[...])

    @pl.when(c == pl.num_programs(1) - 1)
    def _():
        hout_ref[...] = h[rows - 1:rows, :]


def _lru(u, st8, h0, cw, cb, wa, ba, wx, bx, lam, bsz, seq):
    rows = _tile(seq, (128, 64, 32, 16, 8))
    nch = seq // rows
    w = cw.shape[1]
    const = lambda shape: pl.BlockSpec(shape, lambda b, c: (0,) * len(shape))
    return pl.pallas_call(
        functools.partial(_lru_body, rows=rows),
        grid=(bsz, nch),
        in_specs=[pl.BlockSpec((rows, w), lambda b, c: (b * nch + c, XR0 // w)),
                  pl.BlockSpec((rows, w), lambda b, c: (b * nch + c, GATE0 // w)),
                  pl.BlockSpec((None, SUBLANES, w), lambda b, c: (b, 0, 0)),
                  pl.BlockSpec((None, 1, w), lambda b, c: (b, 0, 0)),
                  const((LRU_CONV, w)), const((1, w)), const((w, w)), const((1, w)), const((w, w)), const((1, w)),
                  const((1, w))],
        out_specs=[pl.BlockSpec((rows, w), lambda b, c: (b * nch + c, 0)),
                   pl.BlockSpec((None, 1, w), lambda b, c: (b, 0, 0))],
        out_shape=[jax.ShapeDtypeStruct((bsz * seq, w), F32), jax.ShapeDtypeStruct((bsz, 1, w), F32)],
        scratch_shapes=[pltpu.VMEM((rows, w), F32), pltpu.VMEM((1, w), F32)],
        compiler_params=_cparams(("arbitrary", "arbitrary"), 32),
        name="lru",
    )(u, u, st8, h0, cw, cb, wa, ba, wx, bx, lam)


def _out_proj_body(x_ref, a_ref, s_ref, l_ref, wo_ref, nw_ref, wq_ref, x1_ref, qm_ref):
    wa, ws = a_ref.shape[1], s_ref.shape[1]
    acc = x_ref[...] + _mm(a_ref[...].astype(BF16), wo_ref[0:wa, :])
    acc = acc + _mm(s_ref[...].astype(BF16), wo_ref[wa:wa + ws, :])
    acc = acc + _mm(l_ref[...].astype(BF16), wo_ref[wa + ws:, :])
    x1_ref[...] = acc
    qm_ref[...] = _mm(_rms(acc, nw_ref[...]).astype(BF16), wq_ref[...])


def _out_proj(x, attn, ssd, lru, wo, nw, wq):
    t, d = x.shape
    tm = _tile(t, (256, 128))
    row = lambda w: pl.BlockSpec((tm, w), lambda i: (i, 0))
    const = lambda shape: pl.BlockSpec(shape, lambda i: (0, 0))
    return pl.pallas_call(
        _out_proj_body,
        grid=(t // tm,),
        in_specs=[row(d), row(attn.shape[1]), row(ssd.shape[1]), row(lru.shape[1]),
                  const(wo.shape), const((1, d)), const(wq.shape)],
        out_specs=[row(d), row(wq.shape[1])],
        out_shape=[jax.ShapeDtypeStruct((t, d), F32), jax.ShapeDtypeStruct((t, wq.shape[1]), F32)],
        compiler_params=_cparams(("arbitrary",), 48),
        name="out_proj",
    )(x, attn, ssd, lru, wo, nw, wq)


def _mem_kv_body(m_ref, wk_ref, wv_ref, k_ref, v_ref):
    mb = m_ref[...].astype(BF16)
    k_ref[...] = _mm(mb, wk_ref[...])
    v_ref[...] = _mm(mb, wv_ref[...])


def _mem_kv(mem, wk, wv):
    t, d = mem.shape
    tm = _tile(t, (256, 128))
    w = wk.shape[1]
    return pl.pallas_call(
        _mem_kv_body,
        grid=(t // tm,),
        in_specs=[pl.BlockSpec((tm, d), lambda i: (i, 0)),
                  pl.BlockSpec((d, w), lambda i: (0, 0)), pl.BlockSpec((d, w), lambda i: (0, 0))],
        out_specs=[pl.BlockSpec((tm, w), lambda i: (i, 0)), pl.BlockSpec((tm, w), lambda i: (i, 0))],
        out_shape=[jax.ShapeDtypeStruct((t, w), F32), jax.ShapeDtypeStruct((t, w), F32)],
        compiler_params=_cparams(("arbitrary",), 32),
        name="mem_kv",
    )(mem, wk, wv)


def _xattn_body(q_ref, k_ref, v_ref, o_ref, *, head_rows):
    for h in range(MEM_HEADS):
        sl = slice(h * MEM_DH, (h + 1) * MEM_DH)
        if head_rows:
            rows = pl.ds(h, k_ref.shape[0] // MEM_HEADS, stride=MEM_HEADS)
            k, v = k_ref[rows, :], v_ref[rows, :]
        else:
            k, v = k_ref[:, sl], v_ref[:, sl]
        s = _nt(q_ref[:, sl].astype(BF16), k.astype(BF16)) * (MEM_DH ** -0.5)
        e = jnp.exp(s - jnp.max(s, axis=1, keepdims=True))
        o = _mm(e.astype(BF16), v.astype(BF16))
        o_ref[:, sl] = o / jnp.sum(e, axis=1, keepdims=True)


def _xattn(qm, mem_k, mem_v, bsz, seq, layer=None):
    tq = _tile(seq, (256, 128))
    nq = seq // tq
    w = qm.shape[1]
    if layer is None:
        mem_spec = pl.BlockSpec((None,) + mem_k.shape[1:], lambda b, i: (b, 0, 0))
    else:
        mem_spec = pl.BlockSpec((None, None) + mem_k.shape[2:], lambda b, i: (layer, b, 0, 0))
    return pl.pallas_call(
        functools.partial(_xattn_body, head_rows=layer is not None),
        grid=(bsz, nq),
        in_specs=[pl.BlockSpec((tq, w), lambda b, i: (b * nq + i, 0)), mem_spec, mem_spec],
        out_specs=pl.BlockSpec((tq, w), lambda b, i: (b * nq + i, 0)),
        out_shape=jax.ShapeDtypeStruct(qm.shape, F32),
        compiler_params=_cparams(("arbitrary", "arbitrary"), 32),
        name="xattn",
    )(qm, mem_k, mem_v)


def _route_body(*refs, n_own):
    hf_ref = refs[-3]
    i = pl.program_id(0)

    @pl.when(i < n_own)
    def _():
        _route_tile(*refs)

    @pl.when(i >= n_own)
    def _():
        hf_ref[...] = jnp.zeros_like(hf_ref)


def _route_tile(x1_ref, om_ref, wo_ref, nw_ref, rw_ref, rb_ref, *rest):
    x2_ref, hf_ref, eid_ref, gate_ref = rest[-4:]
    x2 = x1_ref[...] + _mm(om_ref[...].astype(BF16), wo_ref[...])
    x2_ref[...] = x2
    hf = _rms(x2, nw_ref[...])
    hf_ref[...] = hf
    logits = jnp.dot(hf, rw_ref[...], precision=lax.Precision.HIGHEST, preferred_element_type=F32) + rb_ref[...]
    lane = lax.broadcasted_iota(I32, logits.shape, 1).astype(F32)
    ninf = -jnp.inf
    big = float(LANES)

    def first_argmax(v, mx):
        return jnp.min(jnp.where(v == mx, lane, big), axis=1, keepdims=True)

    lg = jnp.where(lane < MOE_GROUPS, logits, ninf)
    mg = jnp.max(lg, axis=1, keepdims=True)
    p_grp = 1.0 / jnp.sum(jnp.exp(lg - mg), axis=1, keepdims=True)
    lo = MOE_GROUPS + MOE_EXPERTS_PER_GROUP * first_argmax(lg, mg)
    le = jnp.where((lane >= lo) & (lane < lo + MOE_EXPERTS_PER_GROUP), logits, ninf)
    v1 = jnp.max(le, axis=1, keepdims=True)
    i1 = first_argmax(le, v1)
    le2 = jnp.where(lane == i1, ninf, le)
    v2 = jnp.max(le2, axis=1, keepdims=True)
    i2 = first_argmax(le2, v2)
    e2 = jnp.exp(v2 - v1)
    den = 1.0 + e2
    g1 = p_grp * (1.0 / den)
    g2 = p_grp * (e2 / den)
    gate_ref[...] = jnp.where(lane == 0.0, g1, jnp.where(lane == 1.0, g2, 0.0))
    eid_ref[...] = jnp.where(lane == 0.0, i1 - MOE_GROUPS, jnp.where(lane == 1.0, i2 - MOE_GROUPS, 0.0)).astype(I32)


def _route(x1, om, wo, nw, rw, rb, hf_rows, hf_row0, hf_shared):
    t, d = x1.shape
    creating = hf_shared is None
    tail = hf_rows - hf_row0 - t if creating else 0
    tm = _tile(math.gcd(t, hf_row0, tail), (256, 128, 64, 32, 16, 8))
    blk0 = hf_row0 // tm
    n_own = t // tm
    n_steps = n_own + tail // tm
    row = lambda w: pl.BlockSpec((tm, w), lambda i: (jnp.minimum(i, n_own - 1), 0))
    const = lambda shape: pl.BlockSpec(shape, lambda i: (0, 0))
    in_specs = [row(d), row(om.shape[1]), const(wo.shape), const((1, d)), const(rw.shape), const((1, LANES))]
    args = [x1, om, wo, nw, rw, rb]
    aliases = {}
    if hf_shared is not None:
        in_specs.append(pl.BlockSpec(memory_space=pl.ANY))
        args.append(hf_shared)
        aliases = {len(args) - 1: 1}
    return pl.pallas_call(
        functools.partial(_route_body, n_own=n_own),
        grid=(n_steps,),
        in_specs=in_specs,
        out_specs=[row(d), pl.BlockSpec((tm, d), lambda i: (blk0 + i, 0)), row(LANES), row(LANES)],
        out_shape=[jax.ShapeDtypeStruct((t, d), F32), jax.ShapeDtypeStruct((hf_rows, d), F32),
                   jax.ShapeDtypeStruct((t, LANES), I32), jax.ShapeDtypeStruct((t, LANES), F32)],
        input_output_aliases=aliases,
        compiler_params=_cparams(("arbitrary",), 40),
        name="route",
    )(*args)


ROW_UNROLL = 8


def _moe_body(grow_ref, srow_ref, be_ref, base_ref, cnt_ref, hf_hbm, wg_ref, wu_ref, wd_ref, o_hbm,
              xb0, xb1, yb0, yb1, wgb, wub, wdb, gsem, ssem, *, rb, n_asg):
    g = pl.program_id(0)
    ng = pl.num_programs(0)
    slot = lax.rem(g, 2)
    last_row = n_asg - 1
    xbs, ybs = (xb0, xb1), (yb0, yb1)

    def gather_row(s, i, src):
        return pltpu.make_async_copy(hf_hbm.at[pl.ds(src, 1), :], xbs[s].at[pl.ds(i, 1), :], gsem.at[s])

    def scatter_row(s, i, dst):
        return pltpu.make_async_copy(ybs[s].at[pl.ds(i, 1), :], o_hbm.at[pl.ds(dst, 1), :], ssem.at[s])

    def wait_gathers(s):
        pltpu.make_async_copy(hf_hbm.at[pl.ds(0, rb), :], xbs[s], gsem.at[s]).wait()

    def wait_scatters(s):
        pltpu.make_async_copy(ybs[s], o_hbm.at[pl.ds(0, rb), :], ssem.at[s]).wait()

    def gather_src(blk, i):
        return grow_ref[jnp.minimum(base_ref[blk] + i, last_row)]

    def scatter_dst(blk, n, s, i):
        return jnp.where(i < n, srow_ref[jnp.minimum(base_ref[blk] + i, last_row)], n_asg + s * rb + i)

    def rolled(fn):
        def grp(q, c):
            for u in range(ROW_UNROLL):
                fn(q * ROW_UNROLL + u)
            return c

        lax.fori_loop(0, rb // ROW_UNROLL, grp, 0)

    @pl.when(g == 0)
    def _():
        for yb in ybs:
            yb[...] = jnp.zeros_like(yb)
        for s in range(2):
            pltpu.make_async_copy(ybs[s], o_hbm.at[pl.ds(n_asg + s * rb, rb), :], ssem.at[s]).start()
        for s in range(2):
            wait_scatters(s)
        rolled(lambda i: gather_row(0, i, gather_src(0, i)).start())

    prev_e = be_ref[jnp.maximum(g - 1, 0)]

    @pl.when((cnt_ref[g] > 0) & ((g == 0) | (prev_e != be_ref[g])))
    def _():
        wgb[...] = wg_ref[...].astype(BF16)
        wub[...] = wu_ref[...].astype(BF16)
        wdb[...] = wd_ref[...].astype(BF16)

    def step(s):
        o = 1 - s
        wait_gathers(s)

        @pl.when(g >= 1)
        def _():
            wait_scatters(s)

        nxt = jnp.minimum(g + 1, ng - 1)
        prv = jnp.maximum(g - 1, 0)
        n_prv = jnp.where(g >= 1, cnt_ref[prv], 0)
        for i in range(rb):
            gather_row(o, i, gather_src(nxt, i)).start()
        xb = xbs[s][...].astype(BF16)
        hg = _mm(xb, wgb[...])
        act = (hg * jax.nn.sigmoid(hg)) * _mm(xb, wub[...])
        ybs[s][...] = _mm(act.astype(BF16), wdb[...])
        for i in range(rb):
            scatter_row(o, i, scatter_dst(prv, n_prv, o, i)).start()

        @pl.when(g == ng - 1)
        def _():
            n_own = cnt_ref[g]
            rolled(lambda i: scatter_row(s, i, scatter_dst(g, n_own, s, i)).start())
            wait_gathers(o)
            wait_scatters(o)
            wait_scatters(s)

    for s in range(2):
        pl.when(slot == s)(functools.partial(step, s))


def _moe_ffn(hf, grow, srow, blk_e, blk_base, blk_cnt, wg, wu, wd, layer, rb):
    t, d = hf.shape
    n_asg = grow.shape[0]
    ff = wg.shape[-1]
    n_blk = blk_e.shape[0]
    wspec = lambda shape: pl.BlockSpec((None, None) + shape, lambda g, gr, sr, be, bb, bc: (layer, be[g], 0, 0))
    return pl.pallas_call(
        functools.partial(_moe_body, rb=rb, n_asg=n_asg),
        grid_spec=pltpu.PrefetchScalarGridSpec(
            num_scalar_prefetch=5,
            grid=(n_blk,),
            in_specs=[pl.BlockSpec(memory_space=pl.ANY), wspec((d, ff)), wspec((d, ff)), wspec((ff, d))],
            out_specs=pl.BlockSpec(memory_space=pl.ANY),
            scratch_shapes=[pltpu.VMEM((rb, d), F32), pltpu.VMEM((rb, d), F32),
                            pltpu.VMEM((rb, d), F32), pltpu.VMEM((rb, d), F32),
                            pltpu.VMEM((d, ff), BF16), pltpu.VMEM((d, ff), BF16), pltpu.VMEM((ff, d), BF16),
                            pltpu.SemaphoreType.DMA((2,)), pltpu.SemaphoreType.DMA((2,))]),
        out_shape=jax.ShapeDtypeStruct((n_asg + 2 * rb, d), F32),
        compiler_params=_cparams(("arbitrary",), 56),
        name="moe_ffn",
    )(grow, srow, blk_e, blk_base, blk_cnt, hf, wg, wu, wd)


def _moe_plan(eid, n_tok, rb):
    n_asg = n_tok * MOE_TOPK
    flat_e = eid.reshape(-1)
    order = jnp.argsort(flat_e, stable=True).astype(I32)
    grow = order // MOE_TOPK
    srow = (order % MOE_TOPK) * n_tok + grow
    counts = jnp.sum((flat_e[:, None] == jnp.arange(MOE_EXPERTS, dtype=I32)[None, :]).astype(I32), axis=0)
    starts = jnp.cumsum(counts) - counts
    nblk = (counts + rb - 1) // rb
    blk_end = jnp.cumsum(nblk)
    n_blk = -(-n_asg // rb) + MOE_EXPERTS
    gi = jnp.arange(n_blk, dtype=I32)
    be = jnp.minimum(jnp.sum((gi[:, None] >= blk_end[None, :]).astype(I32), axis=1), MOE_EXPERTS - 1)
    r = gi - (blk_end - nblk)[be]
    cnt = jnp.clip(counts[be] - r * rb, 0, rb).astype(I32)
    base = jnp.minimum(starts[be] + r * rb, n_asg - 1).astype(I32)
    return grow.astype(I32), srow.astype(I32), be, base, cnt


def _combine_body(x2_ref, o0_ref, o1_ref, g_ref, fw_ref, x3_ref, *, final):
    g = g_ref[...]
    x3 = x2_ref[...] + (g[:, 0:1] * o0_ref[...] + g[:, 1:2] * o1_ref[...])
    x3_ref[...] = _rms(x3, fw_ref[...]) if final else x3


def _combine(x2, o2, gates, fw, final, row0, n_all):
    t, d = x2.shape
    tm = _tile(math.gcd(t, row0, n_all), (256, 128, 64, 32, 16, 8))
    assert t % tm == 0 and row0 % tm == 0 and n_all % tm == 0
    nt = t // tm
    b0, b1 = row0 // tm, (n_all + row0) // tm
    return pl.pallas_call(
        functools.partial(_combine_body, final=final),
        grid=(nt,),
        in_specs=[pl.BlockSpec((tm, d), lambda i: (i, 0)),
                  pl.BlockSpec((tm, d), lambda i: (b0 + i, 0)),
                  pl.BlockSpec((tm, d), lambda i: (b1 + i, 0)),
                  pl.BlockSpec((tm, LANES), lambda i: (i, 0)),
                  pl.BlockSpec((1, d), lambda i: (0, 0))],
        out_specs=pl.BlockSpec((tm, d), lambda i: (i, 0)),
        out_shape=jax.ShapeDtypeStruct((t, d), F32),
        compiler_params=_cparams(("arbitrary",), 32),
        name="combine",
    )(x2, o2, o2, gates, fw)


def _rope_tables(pos):
    half = ATTN_DH // 2
    inv = ROPE_THETA ** (-jnp.arange(half, dtype=F32) / half)
    ang = pos.astype(F32)[:, None] * inv[None, :]
    cos, sin = jnp.cos(ang), jnp.sin(ang)
    reps = LANES // ATTN_DH
    return jnp.tile(jnp.concatenate([cos, cos], axis=1), (1, reps)), jnp.tile(jnp.concatenate([-sin, sin], axis=1), (1, reps))


def _block_diag(w):
    g, a, b = w.shape
    eye = jnp.eye(g, dtype=w.dtype)
    return (w[:, :, None, :] * eye[:, None, :, None]).reshape(g * a, g * b)


def _pad_lanes(v, n):
    return jnp.pad(v.reshape(1, -1), ((0, 0), (0, n - v.size)))


def _layer_front(x, grp, lw, layer, lam_init, hf_rows, hf_row0, hf_shared):
    bsz, seq = grp["bsz"], grp["seq"]
    u = _in_proj(x, lw["norm_mix"], lw["w_in"], grp["cos"], grp["sin"])
    lams = lw["lams"]
    if grp["paged"]:
        attn = _attn_sample(u, grp["cache_k"], grp["cache_v"], grp["page_flat"], layer, grp["n_pool"], lams,
                            lw["subln"], bsz, seq, grp["n_pages"], lam_init)
        k_new = v_new = None
    else:
        attn, k_new, v_new = _attn_prompt(u, lams, lw["subln"], bsz, seq, lam_init)
    ssd, h_ssd = _ssd(u, grp["ssd_conv"][layer], grp["ssd_state"], layer, lw["ssd_conv_w"], lw["ssd_conv_b"],
                      lw["ssd_dt_bias"], lw["ssd_a_log"], lw["ssd_d"], lw["ssd_norm"], bsz, seq)
    lru, h_lru = _lru(u, grp["lru_conv"][layer], grp["lru_state"][layer], lw["lru_conv_w"], lw["lru_conv_b"],
                      lw["lru_wa"], lw["lru_ba"], lw["lru_wx"], lw["lru_bx"], lw["lru_lambda"], bsz, seq)
    x1, qm = _out_proj(x, attn, ssd, lru, lw["w_out"], lw["norm_mem"], lw["wq_mem"])
    if grp["paged"]:
        om = _xattn(qm, grp["mem_k"], grp["mem_v"], bsz, seq, layer)
    else:
        om = _xattn(qm, grp["mem_k"], grp["mem_v"], bsz, seq)
    routed = _route(x1, om, lw["wo_mem"], lw["norm_ffn"], lw["router_w"], lw["router_b"],
                    hf_rows, hf_row0, hf_shared)
    ur = u.reshape(bsz, seq, UW)
    width_kv = ATTN_KV_HEADS * ATTN_VD
    kv_shape = (bsz, seq, ATTN_KV_HEADS, ATTN_VD)
    outs = dict(
        k=(ur[:, :, K0:K0 + width_kv] if k_new is None else k_new).reshape(kv_shape),
        v=(ur[:, :, V0:V0 + width_kv] if v_new is None else v_new).reshape(kv_shape),
        ssd_conv=ur[:, seq - (SSD_CONV - 1):, XBC0:Z0],
        ssd_state=h_ssd.reshape(bsz, SSD_HEADS, SSD_HEAD_DIM, SSD_STATE),
        lru_conv=ur[:, seq - (LRU_CONV - 1):, XR0:GATE0],
        lru_state=h_lru.reshape(bsz, -1),
    )
    return routed, outs


def _moe_groups(routed, lw, layer, final, final_norm):
    hf = routed[-1][1]
    eid = jnp.concatenate([r[2][:, :MOE_TOPK] for r in routed], axis=0)
    n_tok = hf.shape[0]
    rb = 256 if n_tok * MOE_TOPK >= 256 * MOE_EXPERTS else 128
    plan = _moe_plan(eid, n_tok, rb)
    o2 = _moe_ffn(hf, *plan, lw["moe_w_gate"], lw["moe_w_up"], lw["moe_w_down"], layer, rb)
    outs, row0 = [], 0
    for x2, _, _, gate in routed:
        outs.append(_combine(x2, o2, gate, final_norm, final, row0, n_tok))
        row0 += x2.shape[0]
    return outs


def _conv_state8(st):
    return jnp.pad(st, ((0, 0), (0, 0), (SUBLANES - st.shape[2], 0), (0, 0)))


def kernel(x_prompt, x_sample, cache_k, cache_v, cache_mem_k, cache_mem_v, state_ssd_conv, state_ssd, state_lru_conv, state_lru, page_table, mem_prompt, norm_mix, w_in, attn_lambda_q1, attn_lambda_k1, attn_lambda_q2, attn_lambda_k2, attn_subln, ssd_conv_w, ssd_conv_b, ssd_dt_bias, ssd_a_log, ssd_d, ssd_norm, lru_conv_w, lru_conv_b, lru_wa, lru_ba, lru_wx, lru_bx, lru_lambda, w_out, norm_mem, wq_mem, wk_mem, wv_mem, wo_mem, norm_ffn, router_group_w, router_group_b, router_expert_w, router_expert_b, moe_w_gate, moe_w_up, moe_w_down, final_norm):
    depth = w_in.shape[0]
    bp, tp, d = x_prompt.shape
    bs, ts, _ = x_sample.shape
    n_pool, page = cache_k.shape[1], cache_k.shape[2]
    n_pages = page_table.shape[1]
    past_len = n_pages * page
    n_mem = mem_prompt.shape[1]
    ssd_cc = state_ssd_conv.shape[-1]
    lru_w = state_lru.shape[-1]
    kvw = ATTN_KV_HEADS * ATTN_VD
    memw = MEM_HEADS * MEM_DH

    cos_p, sin_p = _rope_tables(jnp.tile(jnp.arange(tp, dtype=I32), bp))
    cos_s, sin_s = _rope_tables(jnp.tile(past_len + jnp.arange(ts, dtype=I32), bs))

    prompt = dict(bsz=bp, seq=tp, paged=False, cos=cos_p, sin=sin_p,
                  ssd_conv=jnp.zeros((depth, bp, SUBLANES, ssd_cc), F32),
                  ssd_state=jnp.zeros((depth, bp, SSD_HEADS * SSD_HEAD_DIM, SSD_STATE), F32),
                  lru_conv=jnp.zeros((depth, bp, SUBLANES, lru_w), F32),
                  lru_state=jnp.zeros((depth, bp, 1, lru_w), F32))
    sample = dict(bsz=bs, seq=ts, paged=True, cos=cos_s, sin=sin_s,
                  cache_k=cache_k.reshape(depth, n_pool, page * ATTN_KV_HEADS, ATTN_VD),
                  cache_v=cache_v.reshape(depth, n_pool, page * ATTN_KV_HEADS, ATTN_VD),
                  page_flat=page_table.reshape(-1), n_pool=n_pool, n_pages=n_pages,
                  ssd_conv=_conv_state8(state_ssd_conv),
                  ssd_state=state_ssd.reshape(depth, bs, SSD_HEADS * SSD_HEAD_DIM, SSD_STATE),
                  lru_conv=_conv_state8(state_lru_conv),
                  lru_state=state_lru.reshape(depth, bs, 1, lru_w),
                  mem_k=cache_mem_k.reshape(depth, bs, n_mem * MEM_HEADS, MEM_DH),
                  mem_v=cache_mem_v.reshape(depth, bs, n_mem * MEM_HEADS, MEM_DH))

    xp = x_prompt.reshape(bp * tp, d)
    xs = x_sample.reshape(bs * ts, d)
    po, so, mks, mvs = [], [], [], []
    for l in range(depth):
        w = w_in[l]
        zc = V0 + kvw
        xc0 = zc + SSD_HEADS * SSD_HEAD_DIM
        dc = xc0 + ssd_cc
        rc = dc + SSD_HEADS
        w_pad = jnp.concatenate([w[:, :zc], w[:, xc0:dc], w[:, zc:xc0], w[:, rc:rc + 2 * lru_w], w[:, dc:rc],
                                 jnp.zeros((d, UW - DT0 - SSD_HEADS), F32)], axis=1).astype(BF16)
        router_w = jnp.pad(jnp.concatenate([router_group_w[l], router_expert_w[l]], axis=1),
                           ((0, 0), (0, LANES - MOE_GROUPS - MOE_EXPERTS)))
        router_b = _pad_lanes(jnp.concatenate([router_group_b[l], router_expert_b[l]]), LANES)
        lw = dict(
            norm_mix=norm_mix[l].reshape(1, d), w_in=w_pad,
            lams=[v[l].reshape(1, ATTN_DH) for v in (attn_lambda_q1, attn_lambda_k1, attn_lambda_q2, attn_lambda_k2)],
            subln=attn_subln[l].reshape(1, ATTN_VD),
            ssd_conv_w=ssd_conv_w[l], ssd_conv_b=ssd_conv_b[l].reshape(1, -1),
            ssd_dt_bias=_pad_lanes(ssd_dt_bias[l], LANES), ssd_a_log=_pad_lanes(ssd_a_log[l], LANES),
            ssd_d=jnp.repeat(ssd_d[l], SSD_HEAD_DIM).reshape(1, -1), ssd_norm=ssd_norm[l].reshape(1, -1),
            lru_conv_w=lru_conv_w[l], lru_conv_b=lru_conv_b[l].reshape(1, -1),
            lru_wa=_block_diag(lru_wa[l]).astype(BF16), lru_ba=lru_ba[l].reshape(1, -1),
            lru_wx=_block_diag(lru_wx[l]).astype(BF16), lru_bx=lru_bx[l].reshape(1, -1),
            lru_lambda=lru_lambda[l].reshape(1, -1),
            w_out=w_out[l].astype(BF16), norm_mem=norm_mem[l].reshape(1, d), wq_mem=wq_mem[l].astype(BF16),
            wo_mem=wo_mem[l].astype(BF16), norm_ffn=norm_ffn[l].reshape(1, d),
            router_w=router_w, router_b=router_b,
            moe_w_gate=moe_w_gate, moe_w_up=moe_w_up, moe_w_down=moe_w_down,
        )
        lam_init = 0.8 - 0.6 * math.exp(-0.3 * l)
        mk, mv = _mem_kv(mem_prompt.reshape(bp * n_mem, d), wk_mem[l].astype(BF16), wv_mem[l].astype(BF16))
        mks.append(mk.reshape(bp, n_mem, MEM_HEADS, MEM_DH))
        mvs.append(mv.reshape(bp, n_mem, MEM_HEADS, MEM_DH))
        pg = dict(prompt, mem_k=mk.reshape(bp, n_mem, memw), mem_v=mv.reshape(bp, n_mem, memw))
        n_all = xp.shape[0] + xs.shape[0]
        routed_p, o = _layer_front(xp, pg, lw, l, lam_init, n_all, 0, None)
        po.append(o)
        routed_s, o = _layer_front(xs, sample, lw, l, lam_init, n_all, xp.shape[0], routed_p[1])
        so.append(o)
        xp, xs = _moe_groups([routed_p, routed_s], lw, l, l == depth - 1, final_norm.reshape(1, d))

    st = lambda outs, key: jnp.stack([o[key] for o in outs])
    return (xp.reshape(bp, tp, d), xs.reshape(bs, ts, d),
            st(po, "k"), st(po, "v"), jnp.stack(mks), jnp.stack(mvs),
            st(po, "ssd_conv"), st(po, "ssd_state"), st(po, "lru_conv"), st(po, "lru_state"),
            st(so, "k"), st(so, "v"), st(so, "ssd_conv"), st(so, "ssd_state"), st(so, "lru_conv"), st(so, "lru_state"))
```

```python
import functools
import math

import jax
import jax.numpy as jnp
from jax import lax
from jax.experimental import pallas as pl
from jax.experimental.pallas import tpu as pltpu

F32, BF16, I32 = jnp.float32, jnp.bfloat16, jnp.int32
EPS = 1e-6
LANES = 128
SUBLANES = 8
MIB = 1024 * 1024

ATTN_HEADS, ATTN_KV_HEADS, ATTN_DH = 8, 4, 64
ATTN_VD = 2 * ATTN_DH
ROPE_THETA = 10000.0
SSD_HEADS, SSD_HEAD_DIM, SSD_GROUPS, SSD_STATE, SSD_CONV = 8, 64, 2, 128, 4
SSD_CHUNK = 128
LRU_BLOCKS, LRU_CONV, LRU_C = 8, 4, 8.0
MEM_HEADS, MEM_DH = 4, 128
MOE_GROUPS, MOE_EXPERTS_PER_GROUP, MOE_TOPK = 4, 8, 2
MOE_EXPERTS = MOE_GROUPS * MOE_EXPERTS_PER_GROUP

Q0, K0, V0, XBC0, Z0, XR0, GATE0, DT0, UW = 0, 1024, 1536, 2048, 3072, 3584, 4096, 4608, 5120
PROJ_TN = 1024
PAGE_GROUP = 16


def _cparams(sem, vmem_mib):
    return pltpu.CompilerParams(dimension_semantics=sem, vmem_limit_bytes=vmem_mib * MIB)


def _tile(n, prefs):
    for p in prefs:
        if n % p == 0:
            return p
    return n


def _rms(x, w):
    return (x * lax.rsqrt(jnp.mean(x * x, axis=-1, keepdims=True) + EPS)) * w


def _load_token_rows(ref, n):
    c = ref.shape[0] // n
    return jnp.concatenate([ref[pl.ds(j, n, stride=c), :] for j in range(c)], axis=1)


def _store_token_rows(ref, x):
    n = x.shape[0]
    c = ref.shape[0] // n
    for j in range(c):
        ref[pl.ds(j, n, stride=c), :] = x[:, j * LANES:(j + 1) * LANES]


def _nt(a, b):
    return lax.dot_general(a, b, (((1,), (1,)), ((), ())), preferred_element_type=F32)


def _mm(a, b):
    return jnp.dot(a, b, preferred_element_type=F32)


def _rope128(yc, cos, sin):
    lane = lax.broadcasted_iota(I32, yc.shape, 1)
    sw = jnp.where((lane % 64) < 32, pltpu.roll(yc, 96, 1), pltpu.roll(yc, 32, 1))
    return yc * cos + sw * sin


def _in_proj_body(x_ref, nw_ref, w_ref, cos_ref, sin_ref, o_ref):
    hb = _rms(x_ref[...], nw_ref[...]).astype(BF16)
    cos, sin = cos_ref[...], sin_ref[...]
    rope_end = V0
    for c0 in range(0, UW, PROJ_TN):
        y = _mm(hb, w_ref[:, c0:c0 + PROJ_TN])
        for c in range(c0, c0 + PROJ_TN, LANES):
            yc = y[:, c - c0:c - c0 + LANES]
            o_ref[:, c:c + LANES] = _rope128(yc, cos, sin) if c < rope_end else yc


def _in_proj(x, nw, w_pad, cos, sin):
    t, d = x.shape
    tm = _tile(t, (256, 128))
    return pl.pallas_call(
        _in_proj_body,
        grid=(t // tm,),
        in_specs=[pl.BlockSpec((tm, d), lambda i: (i, 0)),
                  pl.BlockSpec((1, d), lambda i: (0, 0)),
                  pl.BlockSpec((d, UW), lambda i: (0, 0), pipeline_mode=pl.Buffered(1)),
                  pl.BlockSpec((tm, LANES), lambda i: (i, 0)),
                  pl.BlockSpec((tm, LANES), lambda i: (i, 0))],
        out_specs=pl.BlockSpec((tm, UW), lambda i: (i, 0)),
        out_shape=jax.ShapeDtypeStruct((t, UW), F32),
        compiler_params=_cparams(("arbitrary",), 52),
        name="in_proj",
    )(x, nw, w_pad, cos, sin)


def _lambda(lq1, lk1, lq2, lk2, lam_init):
    s1 = jnp.sum(lq1[...] * lk1[...], axis=-1, keepdims=True)
    s2 = jnp.sum(lq2[...] * lk2[...], axis=-1, keepdims=True)
    return jnp.exp(s1) - jnp.exp(s2) + lam_init


def _attn_prompt_body(lq1, lk1, lq2, lk2, sub_ref, q_ref, k_ref, v_ref, o_ref, ko_ref, vo_ref, acc_sc,
                      *, tq, tk, lam_init):
    i = pl.program_id(2)
    lam = _lambda(lq1, lk1, lq2, lk2, lam_init)

    @pl.when(i == 0)
    def _():
        h = pl.program_id(1)
        rows = pl.ds(h, k_ref.shape[0], stride=ATTN_KV_HEADS)
        ko_ref[rows, :] = k_ref[...]
        vo_ref[rows, :] = v_ref[...]

    q = q_ref[...] * (ATTN_DH ** -0.5 * math.log2(math.e))
    lane = lax.broadcasted_iota(I32, (tq, LANES), 1)
    n4 = 4 * tq
    parts = []
    for g in range(2):
        qh = q[:, g * LANES:(g + 1) * LANES]
        parts.append(jnp.where(lane < ATTN_DH, qh, 0.0))
        parts.append(jnp.where(lane >= ATTN_DH, qh, 0.0))
    q4 = jnp.concatenate(parts, axis=0).astype(BF16)
    acc_sc[...] = jnp.zeros_like(acc_sc)

    def chunk(c, stats, masked):
        m_old, l_old = stats
        start = pl.multiple_of(c * tk, tk)
        k = k_ref[pl.ds(start, tk), :].astype(BF16)
        v = v_ref[pl.ds(start, tk), :].astype(BF16)
        st = _nt(k, q4)
        if masked:
            key = start + lax.broadcasted_iota(I32, (tk, n4), 0)
            t = i * tq + jnp.bitwise_and(lax.broadcasted_iota(I32, (tk, n4), 1), tq - 1)
            st = jnp.where(key <= t, st, -jnp.inf)
        m_new = jnp.maximum(m_old, jnp.max(st, axis=0, keepdims=True))
        alpha = jnp.exp2(m_old - m_new)
        p = jnp.exp2(st - m_new)
        pv = lax.dot_general(v, p.astype(BF16), (((0,), (0,)), ((), ())), preferred_element_type=F32)
        acc_sc[...] = alpha * acc_sc[...] + pv
        return m_new, alpha * l_old + jnp.sum(p, axis=0, keepdims=True)

    last = (i * tq) // tk
    stats = (jnp.full((1, n4), -jnp.inf, F32), jnp.zeros((1, n4), F32))
    stats = lax.fori_loop(0, last, lambda c, s: chunk(c, s, False), stats)
    _, l = chunk(last, stats, True)
    ot = acc_sc[...] / l
    for g in range(2):
        og = ot[:, (2 * g) * tq:(2 * g + 1) * tq] - lam * ot[:, (2 * g + 1) * tq:(2 * g + 2) * tq]
        ms = jnp.mean(og * og, axis=0, keepdims=True)
        nrm = (og * lax.rsqrt(ms + EPS)) * sub_ref[...] * (1.0 - lam_init)
        o_ref[:, g * LANES:(g + 1) * LANES] = nrm.T


def _attn_prompt(u, lams, subln, bsz, seq, lam_init):
    tq = _tile(seq, (256, 128))
    tk = 2 * tq if seq % (2 * tq) == 0 else tq
    nq = seq // tq
    vec = pl.BlockSpec((1, ATTN_DH), lambda b, h, i: (0, 0))
    return pl.pallas_call(
        functools.partial(_attn_prompt_body, tq=tq, tk=tk, lam_init=lam_init),
        grid=(bsz, ATTN_KV_HEADS, nq),
        in_specs=[vec, vec, vec, vec,
                  pl.BlockSpec((ATTN_VD, 1), lambda b, h, i: (0, 0)),
                  pl.BlockSpec((tq, 2 * LANES), lambda b, h, i: (b * nq + i, h)),
                  pl.BlockSpec((seq, LANES), lambda b, h, i: (b, K0 // LANES + h)),
                  pl.BlockSpec((seq, LANES), lambda b, h, i: (b, V0 // LANES + h))],
        out_specs=[pl.BlockSpec((tq, 2 * LANES), lambda b, h, i: (b * nq + i, h)),
                   pl.BlockSpec((seq * ATTN_KV_HEADS, ATTN_VD), lambda b, h, i: (b, 0)),
                   pl.BlockSpec((seq * ATTN_KV_HEADS, ATTN_VD), lambda b, h, i: (b, 0))],
        out_shape=[jax.ShapeDtypeStruct((bsz * seq, ATTN_HEADS * ATTN_VD), F32),
                   jax.ShapeDtypeStruct((bsz * seq * ATTN_KV_HEADS, ATTN_VD), F32),
                   jax.ShapeDtypeStruct((bsz * seq * ATTN_KV_HEADS, ATTN_VD), F32)],
        scratch_shapes=[pltpu.VMEM((ATTN_VD, 4 * tq), F32)],
        compiler_params=_cparams(("arbitrary", "arbitrary", "arbitrary"), 56),
        name="attn_prompt",
    )(*lams, subln.reshape(ATTN_VD, 1), u, u, u)


def _attn_sample_body(pt_ref, lq1, lk1, lq2, lk2, sub_ref, q_ref, kn_ref, vn_ref, *rest, n_pg, lam_init, tdec):
    del pt_ref
    k_refs, v_refs = rest[:n_pg], rest[n_pg:2 * n_pg]
    o_ref, m_sc, l_sc, acc_sc, qr_sc = rest[2 * n_pg:]
    j = pl.program_id(1)
    nrow = ATTN_KV_HEADS * 2 * 2 * tdec
    page_tok = k_refs[0].shape[0] // ATTN_KV_HEADS

    @pl.when(j == 0)
    def _():
        q = q_ref[...] * (ATTN_DH ** -0.5)
        lane = lax.broadcasted_iota(I32, (tdec, LANES), 1)
        qr_sc[...] = jnp.zeros_like(qr_sc)
        for kv in range(ATTN_KV_HEADS):
            for c in range(2):
                for g in range(2):
                    hd = kv * 2 + g
                    qh = q[:, hd * LANES:(hd + 1) * LANES]
                    n0 = ((kv * 2 + c) * 2 + g) * tdec
                    qr_sc[n0:n0 + tdec, kv * LANES:(kv + 1) * LANES] = jnp.where(
                        (lane >= c * ATTN_DH) & (lane < (c + 1) * ATTN_DH), qh, 0.0)
        m_sc[...] = jnp.full_like(m_sc, -jnp.inf)
        l_sc[...] = jnp.zeros_like(l_sc)
        acc_sc[...] = jnp.zeros_like(acc_sc)

    qr = qr_sc[...].astype(BF16)

    def page(ref):
        return jnp.concatenate([ref[pl.ds(kv, page_tok, stride=ATTN_KV_HEADS), :] for kv in range(ATTN_KV_HEADS)],
                               axis=1).astype(BF16)

    def update(s_list, v_list):
        m_old = m_sc[...]
        m_new = m_old
        for s in s_list:
            m_new = jnp.maximum(m_new, jnp.max(s, axis=1, keepdims=True))
        alpha = jnp.exp(m_old - m_new)
        l = alpha * l_sc[...]
        pv = None
        for s, v in zip(s_list, v_list):
            p = jnp.exp(s - m_new)
            l = l + jnp.sum(p, axis=1, keepdims=True)
            d = _mm(p.astype(BF16), v)
            pv = d if pv is None else pv + d
        m_sc[...] = m_new
        l_sc[...] = l
        acc_sc[...] = alpha * acc_sc[...] + pv

    for g0 in range(0, n_pg, PAGE_GROUP):
        grp = range(g0, min(g0 + PAGE_GROUP, n_pg))
        update([_nt(qr, page(k_refs[i])) for i in grp], [page(v_refs[i]) for i in grp])

    @pl.when(j == pl.num_programs(1) - 1)
    def _():
        lam = _lambda(lq1, lk1, lq2, lk2, lam_init)
        pad = jnp.zeros((LANES - tdec, kn_ref.shape[1]), F32)
        kn = jnp.concatenate([kn_ref[...], pad], axis=0).astype(BF16)
        vn = jnp.concatenate([vn_ref[...], pad], axis=0).astype(BF16)
        s = _nt(qr, kn)
        row = jnp.bitwise_and(lax.broadcasted_iota(I32, (nrow, LANES), 0), tdec - 1)
        col = lax.broadcasted_iota(I32, (nrow, LANES), 1)
        update([jnp.where(col <= row, s, -jnp.inf)], [vn])
        o = acc_sc[...] / l_sc[...]
        for kv in range(ATTN_KV_HEADS):
            for g in range(2):
                n0 = ((kv * 2 + 0) * 2 + g) * tdec
                n1 = ((kv * 2 + 1) * 2 + g) * tdec
                og = (o[n0:n0 + tdec, kv * LANES:(kv + 1) * LANES]
                      - lam * o[n1:n1 + tdec, kv * LANES:(kv + 1) * LANES])
                hd = kv * 2 + g
                o_ref[:, hd * LANES:(hd + 1) * LANES] = _rms(og, sub_ref[...]) * (1.0 - lam_init)


def _attn_sample(u, cache_k, cache_v, page_flat, layer, n_pool, lams, subln, bsz, tdec, n_pages, lam_init):
    del n_pool
    n_pg = _tile(n_pages, (32, 16, 8, 4, 2, 1))
    nchunks = n_pages // n_pg
    prow = cache_k.shape[2]
    width = ATTN_KV_HEADS * LANES
    nrow = ATTN_KV_HEADS * 2 * 2 * tdec
    assert nrow == LANES and tdec == SUBLANES
    vec = pl.BlockSpec((1, ATTN_DH), lambda b, j, pt: (0, 0))

    def page_spec(i):
        return pl.BlockSpec((None, None, prow, ATTN_VD),
                            lambda b, j, pt: (layer, pt[b * n_pages + j * n_pg + i], 0, 0))

    in_specs = [vec, vec, vec, vec,
                pl.BlockSpec((1, ATTN_VD), lambda b, j, pt: (0, 0)),
                pl.BlockSpec((tdec, ATTN_HEADS * ATTN_VD), lambda b, j, pt: (b, 0)),
                pl.BlockSpec((tdec, width), lambda b, j, pt: (b, K0 // width)),
                pl.BlockSpec((tdec, width), lambda b, j, pt: (b, V0 // width))]
    in_specs += [page_spec(i) for i in range(n_pg)] * 2
    return pl.pallas_call(
        functools.partial(_attn_sample_body, n_pg=n_pg, lam_init=lam_init, tdec=tdec),
        grid_spec=pltpu.PrefetchScalarGridSpec(
            num_scalar_prefetch=1,
            grid=(bsz, nchunks),
            in_specs=in_specs,
            out_specs=pl.BlockSpec((tdec, ATTN_HEADS * ATTN_VD), lambda b, j, pt: (b, 0)),
            scratch_shapes=[pltpu.VMEM((nrow, 1), F32), pltpu.VMEM((nrow, 1), F32),
                            pltpu.VMEM((nrow, width), F32), pltpu.VMEM((nrow, width), F32)]),
        out_shape=jax.ShapeDtypeStruct((bsz * tdec, ATTN_HEADS * ATTN_VD), F32),
        compiler_params=_cparams(("arbitrary", "arbitrary"), 56),
        name="attn_sample",
    )(page_flat, *lams, subln, u, u, u, *([cache_k] * n_pg), *([cache_v] * n_pg))


def _causal_conv(x, prev, cw_ref, cb_ref):
    taps = cw_ref.shape[0]
    rowi = lax.broadcasted_iota(I32, x.shape, 0)
    acc = cb_ref[...] + x * cw_ref[taps - 1:taps, :]
    for s in range(1, taps):
        sh = jnp.where(rowi < s, pltpu.roll(prev, s, 0), pltpu.roll(x, s, 0))
        acc = acc + sh * cw_ref[taps - 1 - s:taps - s, :]
    return acc


def _pad_rows(x, rows):
    if x.shape[0] == rows:
        return x
    return jnp.concatenate([x, jnp.zeros((rows - x.shape[0], x.shape[1]), x.dtype)], axis=0)


def _ssd_body(xbc_ref, z_ref, dt_ref, st_ref, h0_ref, cw_ref, cb_ref, dtb_ref, alog_ref, dvec_ref, nw_ref,
              y_ref, hout_ref, prev_sc, h_sc, *, rows):
    c = pl.program_id(1)
    L = SSD_CHUNK
    width = SSD_HEADS * SSD_HEAD_DIM
    gw = SSD_GROUPS * SSD_STATE

    @pl.when(c == 0)
    def _():
        prev_sc[...] = jnp.concatenate([jnp.zeros((L - SUBLANES, prev_sc.shape[1]), F32), st_ref[...]], axis=0)
        h_sc[...] = h0_ref[...]

    x = _pad_rows(xbc_ref[...], L)
    conv = _causal_conv(x, prev_sc[...], cw_ref, cb_ref)
    prev_sc[...] = x
    xc = conv * jax.nn.sigmoid(conv)
    xs, bm, cm = xc[:, :width], xc[:, width:width + gw], xc[:, width + gw:]

    dt = jax.nn.softplus(_pad_rows(dt_ref[...], L) + dtb_ref[...])
    if rows < L:
        dt = jnp.where(lax.broadcasted_iota(I32, dt.shape, 0) < rows, dt, 0.0)
    da = dt * (-jnp.exp(alog_ref[...]))
    r0 = lax.broadcasted_iota(I32, (L, L), 0)
    c0 = lax.broadcasted_iota(I32, (L, L), 1)
    causal = r0 >= c0
    acum = jnp.dot(causal.astype(F32), da, precision=lax.Precision.HIGHEST, preferred_element_type=F32)
    acum_t, dt_t = acum.T, dt.T
    last = acum[L - 1:L, :]
    wend = jnp.exp(last - acum) * dt
    eac = jnp.exp(acum)
    elast = jnp.exp(last)
    lane = lax.broadcasted_iota(I32, (L, LANES), 1)
    first = lane < SSD_HEAD_DIM
    top = lax.broadcasted_iota(I32, (2 * SSD_HEAD_DIM, SSD_STATE), 0) < SSD_HEAD_DIM

    ys = []
    for pr in range(SSD_HEADS // 2):
        g = (2 * pr) // (SSD_HEADS // SSD_GROUPS)
        bg = bm[:, g * SSD_STATE:(g + 1) * SSD_STATE].astype(BF16)
        cg = cm[:, g * SSD_STATE:(g + 1) * SSD_STATE].astype(BF16)
        gmat = _nt(cg, bg)
        xp = xs[:, pr * LANES:(pr + 1) * LANES]
        xpb = xp.astype(BF16)
        hp = h_sc[pr * LANES:(pr + 1) * LANES, :]
        outs = []
        for hh in (2 * pr, 2 * pr + 1):
            seg = acum[:, hh:hh + 1] - acum_t[hh:hh + 1, :]
            dec = jnp.exp(jnp.where(causal, seg, -jnp.inf))
            w = gmat * dec * dt_t[hh:hh + 1, :]
            outs.append(_mm(w.astype(BF16), xpb))
        h_a, h_b = 2 * pr, 2 * pr + 1
        y_intra = jnp.where(first, outs[0], outs[1])
        e_pair = jnp.where(first, eac[:, h_a:h_a + 1], eac[:, h_b:h_b + 1])
        ys.append(y_intra + _nt(cg, hp.astype(BF16)) * e_pair)
        w_pair = jnp.where(first, wend[:, h_a:h_a + 1], wend[:, h_b:h_b + 1])
        upd = _mm((xp * w_pair).T.astype(BF16), bg)
        keep = jnp.where(top, elast[:, h_a:h_a + 1], elast[:, h_b:h_b + 1])
        h_sc[pr * LANES:(pr + 1) * LANES, :] = hp * keep + upd

    y = jnp.concatenate(ys, axis=1) + dvec_ref[...] * xs
    zz = _pad_rows(z_ref[...], L)
    gated = y * (zz * jax.nn.sigmoid(zz))
    y_ref[...] = _rms(gated, nw_ref[...])[:rows]

    @pl.when(c == pl.num_programs(1) - 1)
    def _():
        hout_ref[...] = h_sc[...]


def _ssd(u, st8, h0, layer, cw, cb, dtb, alog, dvec, nw, bsz, seq):
    rows = SSD_CHUNK if seq % SSD_CHUNK == 0 else seq
    assert rows == SSD_CHUNK or (rows == seq and rows % SUBLANES == 0 and rows <= SSD_CHUNK)
    nch = seq // rows
    cc = cw.shape[1]
    width = SSD_HEADS * SSD_HEAD_DIM
    hp = SSD_HEADS * SSD_HEAD_DIM
    const = lambda shape: pl.BlockSpec(shape, lambda b, c: (0,) * len(shape))
    return pl.pallas_call(
        functools.partial(_ssd_body, rows=rows),
        grid=(bsz, nch),
        in_specs=[pl.BlockSpec((rows, cc), lambda b, c: (b * nch + c, XBC0 // cc)),
                  pl.BlockSpec((rows, width), lambda b, c: (b * nch + c, Z0 // width)),
                  pl.BlockSpec((rows, LANES), lambda b, c: (b * nch + c, DT0 // LANES)),
                  pl.BlockSpec((None, SUBLANES, cc), lambda b, c: (b, 0, 0)),
                  pl.BlockSpec((None, None, hp, SSD_STATE), lambda b, c: (layer, b, 0, 0)),
                  const((SSD_CONV, cc)), const((1, cc)), const((1, LANES)), const((1, LANES)),
                  const((1, width)), const((1, width))],
        out_specs=[pl.BlockSpec((rows, width), lambda b, c: (b * nch + c, 0)),
                   pl.BlockSpec((None, hp, SSD_STATE), lambda b, c: (b, 0, 0))],
        out_shape=[jax.ShapeDtypeStruct((bsz * seq, width), F32),
                   jax.ShapeDtypeStruct((bsz, hp, SSD_STATE), F32)],
        scratch_shapes=[pltpu.VMEM((SSD_CHUNK, cc), F32), pltpu.VMEM((hp, SSD_STATE), F32)],
        compiler_params=_cparams(("arbitrary", "arbitrary"), 32),
        name="ssd",
    )(u, u, u, st8, h0, cw, cb, dtb, alog, dvec, nw)


def _expm1(t):
    u = jnp.exp(t)
    small = (u - 1.0) * t / jnp.log(u)
    return jnp.where(t < -1.0, u - 1.0, jnp.where(u == 1.0, t, small))


def _lru_body(xr_ref, gate_ref, st_ref, h0_ref, cw_ref, cb_ref, wa_ref, ba_ref, wx_ref, bx_ref, lam_ref,
              y_ref, hout_ref, prev_sc, h_sc, *, rows):
    c = pl.program_id(1)

    @pl.when(c == 0)
    def _():
        st = st_ref[...]
        if rows > SUBLANES:
            st = jnp.concatenate([jnp.zeros((rows - SUBLANES, st.shape[1]), F32), st], axis=0)
        prev_sc[...] = st
        h_sc[...] = h0_ref[...]

    x = xr_ref[...]
    xc = _causal_conv(x, prev_sc[...], cw_ref, cb_ref)
    prev_sc[...] = x
    xcb = xc.astype(BF16)
    r = jax.nn.sigmoid(_mm(xcb, wa_ref[...]) + ba_ref[...])
    ig = jax.nn.sigmoid(_mm(xcb, wx_ref[...]) + bx_ref[...])
    log_a = (-LRU_C) * r * jax.nn.softplus(-lam_ref[...])
    a = jnp.exp(log_a)
    b = jnp.sqrt(-_expm1(2.0 * log_a)) * (ig * xc)
    rowi = lax.broadcasted_iota(I32, a.shape, 0)
    d = 1
    while d < rows:
        ok = rowi >= d
        b = jnp.where(ok, a * pltpu.roll(b, d, 0) + b, b)
        a = jnp.where(ok, a * pltpu.roll(a, d, 0), a)
        d *= 2
    h = b + a * h_sc[...]
    h_sc[...] = h[rows - 1:rows, :]
    y_ref[...] = h * jax.nn.gelu(gate_ref[...])

    @pl.when(c == pl.num_programs(1) - 1)
    def _():
        hout_ref[...] = h[rows - 1:rows, :]


def _lru(u, st8, h0, cw, cb, wa, ba, wx, bx, lam, bsz, seq):
    rows = _tile(seq, (128, 64, 32, 16, 8))
    nch = seq // rows
    w = cw.shape[1]
    const = lambda shape: pl.BlockSpec(shape, lambda b, c: (0,) * len(shape))
    return pl.pallas_call(
        functools.partial(_lru_body, rows=rows),
        grid=(bsz, nch),
        in_specs=[pl.BlockSpec((rows, w), lambda b, c: (b * nch + c, XR0 // w)),
                  pl.BlockSpec((rows, w), lambda b, c: (b * nch + c, GATE0 // w)),
                  pl.BlockSpec((None, SUBLANES, w), lambda b, c: (b, 0, 0)),
                  pl.BlockSpec((None, 1, w), lambda b, c: (b, 0, 0)),
                  const((LRU_CONV, w)), const((1, w)), const((w, w)), const((1, w)), const((w, w)), const((1, w)),
                  const((1, w))],
        out_specs=[pl.BlockSpec((rows, w), lambda b, c: (b * nch + c, 0)),
                   pl.BlockSpec((None, 1, w), lambda b, c: (b, 0, 0))],
        out_shape=[jax.ShapeDtypeStruct((bsz * seq, w), F32), jax.ShapeDtypeStruct((bsz, 1, w), F32)],
        scratch_shapes=[pltpu.VMEM((rows, w), F32), pltpu.VMEM((1, w), F32)],
        compiler_params=_cparams(("arbitrary", "arbitrary"), 32),
        name="lru",
    )(u, u, st8, h0, cw, cb, wa, ba, wx, bx, lam)


def _out_proj_body(x_ref, a_ref, s_ref, l_ref, wo_ref, nw_ref, wq_ref, x1_ref, qm_ref):
    wa, ws = a_ref.shape[1], s_ref.shape[1]
    acc = x_ref[...] + _mm(a_ref[...].astype(BF16), wo_ref[0:wa, :])
    acc = acc + _mm(s_ref[...].astype(BF16), wo_ref[wa:wa + ws, :])
    acc = acc + _mm(l_ref[...].astype(BF16), wo_ref[wa + ws:, :])
    x1_ref[...] = acc
    qm_ref[...] = _mm(_rms(acc, nw_ref[...]).astype(BF16), wq_ref[...])


def _out_proj(x, attn, ssd, lru, wo, nw, wq):
    t, d = x.shape
    tm = _tile(t, (256, 128))
    row = lambda w: pl.BlockSpec((tm, w), lambda i: (i, 0))
    const = lambda shape: pl.BlockSpec(shape, lambda i: (0, 0))
    return pl.pallas_call(
        _out_proj_body,
        grid=(t // tm,),
        in_specs=[row(d), row(attn.shape[1]), row(ssd.shape[1]), row(lru.shape[1]),
                  const(wo.shape), const((1, d)), const(wq.shape)],
        out_specs=[row(d), row(wq.shape[1])],
        out_shape=[jax.ShapeDtypeStruct((t, d), F32), jax.ShapeDtypeStruct((t, wq.shape[1]), F32)],
        compiler_params=_cparams(("arbitrary",), 48),
        name="out_proj",
    )(x, attn, ssd, lru, wo, nw, wq)


def _mem_kv_body(m_ref, wk_ref, wv_ref, k_ref, v_ref):
    mb = m_ref[...].astype(BF16)
    k_ref[...] = _mm(mb, wk_ref[...])
    v_ref[...] = _mm(mb, wv_ref[...])


def _mem_kv(mem, wk, wv):
    t, d = mem.shape
    tm = _tile(t, (256, 128))
    w = wk.shape[1]
    return pl.pallas_call(
        _mem_kv_body,
        grid=(t // tm,),
        in_specs=[pl.BlockSpec((tm, d), lambda i: (i, 0)),
                  pl.BlockSpec((d, w), lambda i: (0, 0)), pl.BlockSpec((d, w), lambda i: (0, 0))],
        out_specs=[pl.BlockSpec((tm, w), lambda i: (i, 0)), pl.BlockSpec((tm, w), lambda i: (i, 0))],
        out_shape=[jax.ShapeDtypeStruct((t, w), F32), jax.ShapeDtypeStruct((t, w), F32)],
        compiler_params=_cparams(("arbitrary",), 32),
        name="mem_kv",
    )(mem, wk, wv)


def _xattn_body(q_ref, k_ref, v_ref, o_ref, *, head_rows):
    for h in range(MEM_HEADS):
        sl = slice(h * MEM_DH, (h + 1) * MEM_DH)
        if head_rows:
            rows = pl.ds(h, k_ref.shape[0] // MEM_HEADS, stride=MEM_HEADS)
            k, v = k_ref[rows, :], v_ref[rows, :]
        else:
            k, v = k_ref[:, sl], v_ref[:, sl]
        s = _nt(q_ref[:, sl].astype(BF16), k.astype(BF16)) * (MEM_DH ** -0.5)
        e = jnp.exp(s - jnp.max(s, axis=1, keepdims=True))
        o = _mm(e.astype(BF16), v.astype(BF16))
        o_ref[:, sl] = o / jnp.sum(e, axis=1, keepdims=True)


def _xattn(qm, mem_k, mem_v, bsz, seq, layer=None):
    tq = _tile(seq, (256, 128))
    nq = seq // tq
    w = qm.shape[1]
    if layer is None:
        mem_spec = pl.BlockSpec((None,) + mem_k.shape[1:], lambda b, i: (b, 0, 0))
    else:
        mem_spec = pl.BlockSpec((None, None) + mem_k.shape[2:], lambda b, i: (layer, b, 0, 0))
    return pl.pallas_call(
        functools.partial(_xattn_body, head_rows=layer is not None),
        grid=(bsz, nq),
        in_specs=[pl.BlockSpec((tq, w), lambda b, i: (b * nq + i, 0)), mem_spec, mem_spec],
        out_specs=pl.BlockSpec((tq, w), lambda b, i: (b * nq + i, 0)),
        out_shape=jax.ShapeDtypeStruct(qm.shape, F32),
        compiler_params=_cparams(("arbitrary", "arbitrary"), 32),
        name="xattn",
    )(qm, mem_k, mem_v)


def _route_body(*refs, n_own):
    hf_ref = refs[-3]
    i = pl.program_id(0)

    @pl.when(i < n_own)
    def _():
        _route_tile(*refs)

    @pl.when(i >= n_own)
    def _():
        hf_ref[...] = jnp.zeros_like(hf_ref)


def _route_tile(x1_ref, om_ref, wo_ref, nw_ref, rw_ref, rb_ref, *rest):
    x2_ref, hf_ref, eid_ref, gate_ref = rest[-4:]
    x2 = x1_ref[...] + _mm(om_ref[...].astype(BF16), wo_ref[...])
    x2_ref[...] = x2
    hf = _rms(x2, nw_ref[...])
    _store_token_rows(hf_ref, hf)
    logits = jnp.dot(hf, rw_ref[...], precision=lax.Precision.HIGHEST, preferred_element_type=F32) + rb_ref[...]
    lane = lax.broadcasted_iota(I32, logits.shape, 1).astype(F32)
    ninf = -jnp.inf
    big = float(LANES)

    def first_argmax(v, mx):
        return jnp.min(jnp.where(v == mx, lane, big), axis=1, keepdims=True)

    lg = jnp.where(lane < MOE_GROUPS, logits, ninf)
    mg = jnp.max(lg, axis=1, keepdims=True)
    p_grp = 1.0 / jnp.sum(jnp.exp(lg - mg), axis=1, keepdims=True)
    lo = MOE_GROUPS + MOE_EXPERTS_PER_GROUP * first_argmax(lg, mg)
    le = jnp.where((lane >= lo) & (lane < lo + MOE_EXPERTS_PER_GROUP), logits, ninf)
    v1 = jnp.max(le, axis=1, keepdims=True)
    i1 = first_argmax(le, v1)
    le2 = jnp.where(lane == i1, ninf, le)
    v2 = jnp.max(le2, axis=1, keepdims=True)
    i2 = first_argmax(le2, v2)
    e2 = jnp.exp(v2 - v1)
    den = 1.0 + e2
    g1 = p_grp * (1.0 / den)
    g2 = p_grp * (e2 / den)
    gate_ref[...] = jnp.where(lane == 0.0, g1, jnp.where(lane == 1.0, g2, 0.0))
    eid_ref[...] = jnp.where(lane == 0.0, i1 - MOE_GROUPS, jnp.where(lane == 1.0, i2 - MOE_GROUPS, 0.0)).astype(I32)


def _route(x1, om, wo, nw, rw, rb, hf_rows, hf_row0, hf_shared):
    t, d = x1.shape
    creating = hf_shared is None
    tail = hf_rows - hf_row0 - t if creating else 0
    tm = _tile(math.gcd(t, hf_row0, tail), (256, 128, 64, 32, 16, 8))
    blk0 = hf_row0 // tm
    n_own = t // tm
    n_steps = n_own + tail // tm
    row = lambda w: pl.BlockSpec((tm, w), lambda i: (jnp.minimum(i, n_own - 1), 0))
    const = lambda shape: pl.BlockSpec(shape, lambda i: (0, 0))
    in_specs = [row(d), row(om.shape[1]), const(wo.shape), const((1, d)), const(rw.shape), const((1, LANES))]
    args = [x1, om, wo, nw, rw, rb]
    aliases = {}
    if hf_shared is not None:
        in_specs.append(pl.BlockSpec(memory_space=pl.ANY))
        args.append(hf_shared)
        aliases = {len(args) - 1: 1}
    return pl.pallas_call(
        functools.partial(_route_body, n_own=n_own),
        grid=(n_steps,),
        in_specs=in_specs,
        out_specs=[row(d), pl.BlockSpec((tm * (d // LANES), LANES), lambda i: (blk0 + i, 0)), row(LANES), row(LANES)],
        out_shape=[jax.ShapeDtypeStruct((t, d), F32), jax.ShapeDtypeStruct((hf_rows * (d // LANES), LANES), F32),
                   jax.ShapeDtypeStruct((t, LANES), I32), jax.ShapeDtypeStruct((t, LANES), F32)],
        input_output_aliases=aliases,
        compiler_params=_cparams(("arbitrary",), 40),
        name="route",
    )(*args)


ROW_UNROLL = 8


def _moe_body(grow_ref, srow_ref, be_ref, base_ref, cnt_ref, hf_hbm, wg_ref, wu_ref, wd_ref, o_hbm,
              xb0, xb1, yb0, yb1, wgb, wub, wdb, gsem, ssem, *, rb, n_asg):
    g = pl.program_id(0)
    ng = pl.num_programs(0)
    slot = lax.rem(g, 2)
    last_row = n_asg - 1
    xbs, ybs = (xb0, xb1), (yb0, yb1)
    tr = xb0.shape[0] // rb

    def tok(ref, i, n=1):
        return ref.at[pl.ds(i * tr, n * tr), :]

    def gather_row(s, i, src):
        return pltpu.make_async_copy(tok(hf_hbm, src), tok(xbs[s], i), gsem.at[s])

    def scatter_row(s, i, dst):
        return pltpu.make_async_copy(tok(ybs[s], i), tok(o_hbm, dst), ssem.at[s])

    def wait_gathers(s):
        pltpu.make_async_copy(tok(hf_hbm, 0, rb), xbs[s], gsem.at[s]).wait()

    def wait_scatters(s):
        pltpu.make_async_copy(ybs[s], tok(o_hbm, 0, rb), ssem.at[s]).wait()

    def gather_src(blk, i):
        return grow_ref[jnp.minimum(base_ref[blk] + i, last_row)]

    def scatter_dst(blk, n, s, i):
        return jnp.where(i < n, srow_ref[jnp.minimum(base_ref[blk] + i, last_row)], n_asg + s * rb + i)

    def rolled(fn):
        def grp(q, c):
            for u in range(ROW_UNROLL):
                fn(q * ROW_UNROLL + u)
            return c

        lax.fori_loop(0, rb // ROW_UNROLL, grp, 0)

    @pl.when(g == 0)
    def _():
        for yb in ybs:
            yb[...] = jnp.zeros_like(yb)
        for s in range(2):
            pltpu.make_async_copy(ybs[s], tok(o_hbm, n_asg + s * rb, rb), ssem.at[s]).start()
        for s in range(2):
            wait_scatters(s)
        rolled(lambda i: gather_row(0, i, gather_src(0, i)).start())

    prev_e = be_ref[jnp.maximum(g - 1, 0)]

    @pl.when((cnt_ref[g] > 0) & ((g == 0) | (prev_e != be_ref[g])))
    def _():
        wgb[...] = wg_ref[...].astype(BF16)
        wub[...] = wu_ref[...].astype(BF16)
        wdb[...] = wd_ref[...].astype(BF16)

    def step(s):
        o = 1 - s
        wait_gathers(s)

        @pl.when(g >= 1)
        def _():
            wait_scatters(s)

        nxt = jnp.minimum(g + 1, ng - 1)
        prv = jnp.maximum(g - 1, 0)
        n_prv = jnp.where(g >= 1, cnt_ref[prv], 0)
        for i in range(rb):
            gather_row(o, i, gather_src(nxt, i)).start()
        xb = _load_token_rows(xbs[s], rb).astype(BF16)
        hg = _mm(xb, wgb[...])
        act = (hg * jax.nn.sigmoid(hg)) * _mm(xb, wub[...])
        _store_token_rows(ybs[s], _mm(act.astype(BF16), wdb[...]))
        for i in range(rb):
            scatter_row(o, i, scatter_dst(prv, n_prv, o, i)).start()

        @pl.when(g == ng - 1)
        def _():
            n_own = cnt_ref[g]
            rolled(lambda i: scatter_row(s, i, scatter_dst(g, n_own, s, i)).start())
            wait_gathers(o)
            wait_scatters(o)
            wait_scatters(s)

    for s in range(2):
        pl.when(slot == s)(functools.partial(step, s))


def _moe_ffn(hf, grow, srow, blk_e, blk_base, blk_cnt, wg, wu, wd, layer, rb):
    d, ff = wg.shape[-2:]
    tr = d // LANES
    n_asg = grow.shape[0]
    n_blk = blk_e.shape[0]
    wspec = lambda shape: pl.BlockSpec((None, None) + shape, lambda g, gr, sr, be, bb, bc: (layer, be[g], 0, 0))
    return pl.pallas_call(
        functools.partial(_moe_body, rb=rb, n_asg=n_asg),
        grid_spec=pltpu.PrefetchScalarGridSpec(
            num_scalar_prefetch=5,
            grid=(n_blk,),
            in_specs=[pl.BlockSpec(memory_space=pl.ANY), wspec((d, ff)), wspec((d, ff)), wspec((ff, d))],
            out_specs=pl.BlockSpec(memory_space=pl.ANY),
            scratch_shapes=[pltpu.VMEM((rb * tr, LANES), F32), pltpu.VMEM((rb * tr, LANES), F32),
                            pltpu.VMEM((rb * tr, LANES), F32), pltpu.VMEM((rb * tr, LANES), F32),
                            pltpu.VMEM((d, ff), BF16), pltpu.VMEM((d, ff), BF16), pltpu.VMEM((ff, d), BF16),
                            pltpu.SemaphoreType.DMA((2,)), pltpu.SemaphoreType.DMA((2,))]),
        out_shape=jax.ShapeDtypeStruct(((n_asg + 2 * rb) * tr, LANES), F32),
        compiler_params=_cparams(("arbitrary",), 56),
        name="moe_ffn",
    )(grow, srow, blk_e, blk_base, blk_cnt, hf, wg, wu, wd)


def _moe_plan(eid, n_tok, rb):
    n_asg = n_tok * MOE_TOPK
    flat_e = eid.reshape(-1)
    order = jnp.argsort(flat_e, stable=True).astype(I32)
    grow = order // MOE_TOPK
    srow = (order % MOE_TOPK) * n_tok + grow
    counts = jnp.sum((flat_e[:, None] == jnp.arange(MOE_EXPERTS, dtype=I32)[None, :]).astype(I32), axis=0)
    starts = jnp.cumsum(counts) - counts
    nblk = (counts + rb - 1) // rb
    blk_end = jnp.cumsum(nblk)
    n_blk = -(-n_asg // rb) + MOE_EXPERTS
    gi = jnp.arange(n_blk, dtype=I32)
    be = jnp.minimum(jnp.sum((gi[:, None] >= blk_end[None, :]).astype(I32), axis=1), MOE_EXPERTS - 1)
    r = gi - (blk_end - nblk)[be]
    cnt = jnp.clip(counts[be] - r * rb, 0, rb).astype(I32)
    base = jnp.minimum(starts[be] + r * rb, n_asg - 1).astype(I32)
    return grow.astype(I32), srow.astype(I32), be, base, cnt


def _combine_body(x2_ref, o0_ref, o1_ref, g_ref, fw_ref, x3_ref, *, final):
    g = g_ref[...]
    n = x2_ref.shape[0]
    x3 = x2_ref[...] + (g[:, 0:1] * _load_token_rows(o0_ref, n) + g[:, 1:2] * _load_token_rows(o1_ref, n))
    x3_ref[...] = _rms(x3, fw_ref[...]) if final else x3


def _combine(x2, o2, gates, fw, final, row0, n_all):
    t, d = x2.shape
    tm = _tile(math.gcd(t, row0, n_all), (256, 128, 64, 32, 16, 8))
    assert t % tm == 0 and row0 % tm == 0 and n_all % tm == 0
    nt = t // tm
    b0, b1 = row0 // tm, (n_all + row0) // tm
    return pl.pallas_call(
        functools.partial(_combine_body, final=final),
        grid=(nt,),
        in_specs=[pl.BlockSpec((tm, d), lambda i: (i, 0)),
                  pl.BlockSpec((tm * (d // LANES), LANES), lambda i: (b0 + i, 0)),
                  pl.BlockSpec((tm * (d // LANES), LANES), lambda i: (b1 + i, 0)),
                  pl.BlockSpec((tm, LANES), lambda i: (i, 0)),
                  pl.BlockSpec((1, d), lambda i: (0, 0))],
        out_specs=pl.BlockSpec((tm, d), lambda i: (i, 0)),
        out_shape=jax.ShapeDtypeStruct((t, d), F32),
        compiler_params=_cparams(("arbitrary",), 32),
        name="combine",
    )(x2, o2, o2, gates, fw)


def _rope_tables(pos):
    half = ATTN_DH // 2
    inv = ROPE_THETA ** (-jnp.arange(half, dtype=F32) / half)
    ang = pos.astype(F32)[:, None] * inv[None, :]
    cos, sin = jnp.cos(ang), jnp.sin(ang)
    reps = LANES // ATTN_DH
    return jnp.tile(jnp.concatenate([cos, cos], axis=1), (1, reps)), jnp.tile(jnp.concatenate([-sin, sin], axis=1), (1, reps))


def _block_diag(w):
    g, a, b = w.shape
    eye = jnp.eye(g, dtype=w.dtype)
    return (w[:, :, None, :] * eye[:, None, :, None]).reshape(g * a, g * b)


def _pad_lanes(v, n):
    return jnp.pad(v.reshape(1, -1), ((0, 0), (0, n - v.size)))


def _layer_front(x, grp, lw, layer, lam_init, hf_rows, hf_row0, hf_shared):
    bsz, seq = grp["bsz"], grp["seq"]
    u = _in_proj(x, lw["norm_mix"], lw["w_in"], grp["cos"], grp["sin"])
    lams = lw["lams"]
    if grp["paged"]:
        attn = _attn_sample(u, grp["cache_k"], grp["cache_v"], grp["page_flat"], layer, grp["n_pool"], lams,
                            lw["subln"], bsz, seq, grp["n_pages"], lam_init)
        k_new = v_new = None
    else:
        attn, k_new, v_new = _attn_prompt(u, lams, lw["subln"], bsz, seq, lam_init)
    ssd, h_ssd = _ssd(u, grp["ssd_conv"][layer], grp["ssd_state"], layer, lw["ssd_conv_w"], lw["ssd_conv_b"],
                      lw["ssd_dt_bias"], lw["ssd_a_log"], lw["ssd_d"], lw["ssd_norm"], bsz, seq)
    lru, h_lru = _lru(u, grp["lru_conv"][layer], grp["lru_state"][layer], lw["lru_conv_w"], lw["lru_conv_b"],
                      lw["lru_wa"], lw["lru_ba"], lw["lru_wx"], lw["lru_bx"], lw["lru_lambda"], bsz, seq)
    x1, qm = _out_proj(x, attn, ssd, lru, lw["w_out"], lw["norm_mem"], lw["wq_mem"])
    if grp["paged"]:
        om = _xattn(qm, grp["mem_k"], grp["mem_v"], bsz, seq, layer)
    else:
        om = _xattn(qm, grp["mem_k"], grp["mem_v"], bsz, seq)
    routed = _route(x1, om, lw["wo_mem"], lw["norm_ffn"], lw["router_w"], lw["router_b"],
                    hf_rows, hf_row0, hf_shared)
    ur = u.reshape(bsz, seq, UW)
    width_kv = ATTN_KV_HEADS * ATTN_VD
    kv_shape = (bsz, seq, ATTN_KV_HEADS, ATTN_VD)
    outs = dict(
        k=(ur[:, :, K0:K0 + width_kv] if k_new is None else k_new).reshape(kv_shape),
        v=(ur[:, :, V0:V0 + width_kv] if v_new is None else v_new).reshape(kv_shape),
        ssd_conv=ur[:, seq - (SSD_CONV - 1):, XBC0:Z0],
        ssd_state=h_ssd.reshape(bsz, SSD_HEADS, SSD_HEAD_DIM, SSD_STATE),
        lru_conv=ur[:, seq - (LRU_CONV - 1):, XR0:GATE0],
        lru_state=h_lru.reshape(bsz, -1),
    )
    return routed, outs


def _moe_groups(routed, lw, layer, final, final_norm):
    hf = routed[-1][1]
    eid = jnp.concatenate([r[2][:, :MOE_TOPK] for r in routed], axis=0)
    n_tok = eid.shape[0]
    rb = 256 if n_tok * MOE_TOPK >= 256 * MOE_EXPERTS else 128
    plan = _moe_plan(eid, n_tok, rb)
    o2 = _moe_ffn(hf, *plan, lw["moe_w_gate"], lw["moe_w_up"], lw["moe_w_down"], layer, rb)
    outs, row0 = [], 0
    for x2, _, _, gate in routed:
        outs.append(_combine(x2, o2, gate, final_norm, final, row0, n_tok))
        row0 += x2.shape[0]
    return outs


def _conv_state8(st):
    return jnp.pad(st, ((0, 0), (0, 0), (SUBLANES - st.shape[2], 0), (0, 0)))


def kernel(x_prompt, x_sample, cache_k, cache_v, cache_mem_k, cache_mem_v, state_ssd_conv, state_ssd, state_lru_conv, state_lru, page_table, mem_prompt, norm_mix, w_in, attn_lambda_q1, attn_lambda_k1, attn_lambda_q2, attn_lambda_k2, attn_subln, ssd_conv_w, ssd_conv_b, ssd_dt_bias, ssd_a_log, ssd_d, ssd_norm, lru_conv_w, lru_conv_b, lru_wa, lru_ba, lru_wx, lru_bx, lru_lambda, w_out, norm_mem, wq_mem, wk_mem, wv_mem, wo_mem, norm_ffn, router_group_w, router_group_b, router_expert_w, router_expert_b, moe_w_gate, moe_w_up, moe_w_down, final_norm):
    depth = w_in.shape[0]
    bp, tp, d = x_prompt.shape
    bs, ts, _ = x_sample.shape
    n_pool, page = cache_k.shape[1], cache_k.shape[2]
    n_pages = page_table.shape[1]
    past_len = n_pages * page
    n_mem = mem_prompt.shape[1]
    ssd_cc = state_ssd_conv.shape[-1]
    lru_w = state_lru.shape[-1]
    kvw = ATTN_KV_HEADS * ATTN_VD
    memw = MEM_HEADS * MEM_DH

    cos_p, sin_p = _rope_tables(jnp.tile(jnp.arange(tp, dtype=I32), bp))
    cos_s, sin_s = _rope_tables(jnp.tile(past_len + jnp.arange(ts, dtype=I32), bs))

    prompt = dict(bsz=bp, seq=tp, paged=False, cos=cos_p, sin=sin_p,
                  ssd_conv=jnp.zeros((depth, bp, SUBLANES, ssd_cc), F32),
                  ssd_state=jnp.zeros((depth, bp, SSD_HEADS * SSD_HEAD_DIM, SSD_STATE), F32),
                  lru_conv=jnp.zeros((depth, bp, SUBLANES, lru_w), F32),
                  lru_state=jnp.zeros((depth, bp, 1, lru_w), F32))
    sample = dict(bsz=bs, seq=ts, paged=True, cos=cos_s, sin=sin_s,
                  cache_k=cache_k.reshape(depth, n_pool, page * ATTN_KV_HEADS, ATTN_VD),
                  cache_v=cache_v.reshape(depth, n_pool, page * ATTN_KV_HEADS, ATTN_VD),
                  page_flat=page_table.reshape(-1), n_pool=n_pool, n_pages=n_pages,
                  ssd_conv=_conv_state8(state_ssd_conv),
                  ssd_state=state_ssd.reshape(depth, bs, SSD_HEADS * SSD_HEAD_DIM, SSD_STATE),
                  lru_conv=_conv_state8(state_lru_conv),
                  lru_state=state_lru.reshape(depth, bs, 1, lru_w),
                  mem_k=cache_mem_k.reshape(depth, bs, n_mem * MEM_HEADS, MEM_DH),
                  mem_v=cache_mem_v.reshape(depth, bs, n_mem * MEM_HEADS, MEM_DH))

    xp = x_prompt.reshape(bp * tp, d)
    xs = x_sample.reshape(bs * ts, d)
    po, so, mks, mvs = [], [], [], []
    for l in range(depth):
        w = w_in[l]
        zc = V0 + kvw
        xc0 = zc + SSD_HEADS * SSD_HEAD_DIM
        dc = xc0 + ssd_cc
        rc = dc + SSD_HEADS
        w_pad = jnp.concatenate([w[:, :zc], w[:, xc0:dc], w[:, zc:xc0], w[:, rc:rc + 2 * lru_w], w[:, dc:rc],
                                 jnp.zeros((d, UW - DT0 - SSD_HEADS), F32)], axis=1).astype(BF16)
        router_w = jnp.pad(jnp.concatenate([router_group_w[l], router_expert_w[l]], axis=1),
                           ((0, 0), (0, LANES - MOE_GROUPS - MOE_EXPERTS)))
        router_b = _pad_lanes(jnp.concatenate([router_group_b[l], router_expert_b[l]]), LANES)
        lw = dict(
            norm_mix=norm_mix[l].reshape(1, d), w_in=w_pad,
            lams=[v[l].reshape(1, ATTN_DH) for v in (attn_lambda_q1, attn_lambda_k1, attn_lambda_q2, attn_lambda_k2)],
            subln=attn_subln[l].reshape(1, ATTN_VD),
            ssd_conv_w=ssd_conv_w[l], ssd_conv_b=ssd_conv_b[l].reshape(1, -1),
            ssd_dt_bias=_pad_lanes(ssd_dt_bias[l], LANES), ssd_a_log=_pad_lanes(ssd_a_log[l], LANES),
            ssd_d=jnp.repeat(ssd_d[l], SSD_HEAD_DIM).reshape(1, -1), ssd_norm=ssd_norm[l].reshape(1, -1),
            lru_conv_w=lru_conv_w[l], lru_conv_b=lru_conv_b[l].reshape(1, -1),
            lru_wa=_block_diag(lru_wa[l]).astype(BF16), lru_ba=lru_ba[l].reshape(1, -1),
            lru_wx=_block_diag(lru_wx[l]).astype(BF16), lru_bx=lru_bx[l].reshape(1, -1),
            lru_lambda=lru_lambda[l].reshape(1, -1),
            w_out=w_out[l].astype(BF16), norm_mem=norm_mem[l].reshape(1, d), wq_mem=wq_mem[l].astype(BF16),
            wo_mem=wo_mem[l].astype(BF16), norm_ffn=norm_ffn[l].reshape(1, d),
            router_w=router_w, router_b=router_b,
            moe_w_gate=moe_w_gate, moe_w_up=moe_w_up, moe_w_down=moe_w_down,
        )
        lam_init = 0.8 - 0.6 * math.exp(-0.3 * l)
        mk, mv = _mem_kv(mem_prompt.reshape(bp * n_mem, d), wk_mem[l].astype(BF16), wv_mem[l].astype(BF16))
        mks.append(mk.reshape(bp, n_mem, MEM_HEADS, MEM_DH))
        mvs.append(mv.reshape(bp, n_mem, MEM_HEADS, MEM_DH))
        pg = dict(prompt, mem_k=mk.reshape(bp, n_mem, memw), mem_v=mv.reshape(bp, n_mem, memw))
        n_all = xp.shape[0] + xs.shape[0]
        routed_p, o = _layer_front(xp, pg, lw, l, lam_init, n_all, 0, None)
        po.append(o)
        routed_s, o = _layer_front(xs, sample, lw, l, lam_init, n_all, xp.shape[0], routed_p[1])
        so.append(o)
        xp, xs = _moe_groups([routed_p, routed_s], lw, l, l == depth - 1, final_norm.reshape(1, d))

    st = lambda outs, key: jnp.stack([o[key] for o in outs])
    return (xp.reshape(bp, tp, d), xs.reshape(bs, ts, d),
            st(po, "k"), st(po, "v"), jnp.stack(mks), jnp.stack(mvs),
            st(po, "ssd_conv"), st(po, "ssd_state"), st(po, "lru_conv"), st(po, "lru_state"),
            st(so, "k"), st(so, "v"), st(so, "ssd_conv"), st(so, "ssd_state"), st(so, "lru_conv"), st(so, "lru_state"))
```

```python
import functools
import math

import jax
import jax.numpy as jnp
from jax import lax
from jax.experimental import pallas as pl
from jax.experimental.pallas import tpu as pltpu

F32, BF16, I32 = jnp.float32, jnp.bfloat16, jnp.int32
EPS = 1e-6
LANES = 128
SUBLANES = 8
MIB = 1024 * 1024

ATTN_HEADS, ATTN_KV_HEADS, ATTN_DH = 8, 4, 64
ATTN_VD = 2 * ATTN_DH
ROPE_THETA = 10000.0
SSD_HEADS, SSD_HEAD_DIM, SSD_GROUPS, SSD_STATE, SSD_CONV = 8, 64, 2, 128, 4
SSD_CHUNK = 128
LRU_BLOCKS, LRU_CONV, LRU_C = 8, 4, 8.0
MEM_HEADS, MEM_DH = 4, 128
MOE_GROUPS, MOE_EXPERTS_PER_GROUP, MOE_TOPK = 4, 8, 2
MOE_EXPERTS = MOE_GROUPS * MOE_EXPERTS_PER_GROUP

Q0, K0, V0, XBC0, Z0, XR0, GATE0, DT0, UW = 0, 1024, 1536, 2048, 3072, 3584, 4096, 4608, 5120
PROJ_TN = 1024
PAGE_GROUP = 16


def _cparams(sem, vmem_mib):
    return pltpu.CompilerParams(dimension_semantics=sem, vmem_limit_bytes=vmem_mib * MIB)


def _tile(n, prefs):
    for p in prefs:
        if n % p == 0:
            return p
    return n


def _rms(x, w):
    return (x * lax.rsqrt(jnp.mean(x * x, axis=-1, keepdims=True) + EPS)) * w


def _nt(a, b):
    return lax.dot_general(a, b, (((1,), (1,)), ((), ())), preferred_element_type=F32)


def _mm(a, b):
    return jnp.dot(a, b, preferred_element_type=F32)


def _rope128(yc, cos, sin):
    lane = lax.broadcasted_iota(I32, yc.shape, 1)
    sw = jnp.where((lane % 64) < 32, pltpu.roll(yc, 96, 1), pltpu.roll(yc, 32, 1))
    return yc * cos + sw * sin


def _in_proj_body(x_ref, nw_ref, w_ref, cos_ref, sin_ref, o_ref):
    hb = _rms(x_ref[...], nw_ref[...]).astype(BF16)
    cos, sin = cos_ref[...], sin_ref[...]
    rope_end = V0
    for c0 in range(0, UW, PROJ_TN):
        y = _mm(hb, w_ref[:, c0:c0 + PROJ_TN])
        for c in range(c0, c0 + PROJ_TN, LANES):
            yc = y[:, c - c0:c - c0 + LANES]
            o_ref[:, c:c + LANES] = _rope128(yc, cos, sin) if c < rope_end else yc


def _in_proj(x, nw, w_pad, cos, sin):
    t, d = x.shape
    tm = _tile(t, (256, 128))
    return pl.pallas_call(
        _in_proj_body,
        grid=(t // tm,),
        in_specs=[pl.BlockSpec((tm, d), lambda i: (i, 0)),
                  pl.BlockSpec((1, d), lambda i: (0, 0)),
                  pl.BlockSpec((d, UW), lambda i: (0, 0), pipeline_mode=pl.Buffered(1)),
                  pl.BlockSpec((tm, LANES), lambda i: (i, 0)),
                  pl.BlockSpec((tm, LANES), lambda i: (i, 0))],
        out_specs=pl.BlockSpec((tm, UW), lambda i: (i, 0)),
        out_shape=jax.ShapeDtypeStruct((t, UW), F32),
        compiler_params=_cparams(("arbitrary",), 52),
        name="in_proj",
    )(x, nw, w_pad, cos, sin)


def _lambda(lq1, lk1, lq2, lk2, lam_init):
    s1 = jnp.sum(lq1[...] * lk1[...], axis=-1, keepdims=True)
    s2 = jnp.sum(lq2[...] * lk2[...], axis=-1, keepdims=True)
    return jnp.exp(s1) - jnp.exp(s2) + lam_init


def _attn_prompt_body(lq1, lk1, lq2, lk2, sub_ref, q_ref, k_ref, v_ref, o_ref, ko_ref, vo_ref, acc_sc,
                      *, tq, tk, lam_init):
    i = pl.program_id(2)
    lam = _lambda(lq1, lk1, lq2, lk2, lam_init)

    @pl.when(i == 0)
    def _():
        h = pl.program_id(1)
        rows = pl.ds(h, k_ref.shape[0], stride=ATTN_KV_HEADS)
        ko_ref[rows, :] = k_ref[...]
        vo_ref[rows, :] = v_ref[...]

    q = q_ref[...] * (ATTN_DH ** -0.5 * math.log2(math.e))
    lane = lax.broadcasted_iota(I32, (tq, LANES), 1)
    n4 = 4 * tq
    parts = []
    for g in range(2):
        qh = q[:, g * LANES:(g + 1) * LANES]
        parts.append(jnp.where(lane < ATTN_DH, qh, 0.0))
        parts.append(jnp.where(lane >= ATTN_DH, qh, 0.0))
    q4 = jnp.concatenate(parts, axis=0).astype(BF16)
    acc_sc[...] = jnp.zeros_like(acc_sc)

    def chunk(c, stats, masked):
        m_old, l_old = stats
        start = pl.multiple_of(c * tk, tk)
        k = k_ref[pl.ds(start, tk), :].astype(BF16)
        v = v_ref[pl.ds(start, tk), :].astype(BF16)
        st = _nt(k, q4)
        if masked:
            key = start + lax.broadcasted_iota(I32, (tk, n4), 0)
            t = i * tq + jnp.bitwise_and(lax.broadcasted_iota(I32, (tk, n4), 1), tq - 1)
            st = jnp.where(key <= t, st, -jnp.inf)
        m_new = jnp.maximum(m_old, jnp.max(st, axis=0, keepdims=True))
        alpha = jnp.exp2(m_old - m_new)
        p = jnp.exp2(st - m_new)
        pv = lax.dot_general(v, p.astype(BF16), (((0,), (0,)), ((), ())), preferred_element_type=F32)
        acc_sc[...] = alpha * acc_sc[...] + pv
        return m_new, alpha * l_old + jnp.sum(p, axis=0, keepdims=True)

    last = (i * tq) // tk
    stats = (jnp.full((1, n4), -jnp.inf, F32), jnp.zeros((1, n4), F32))
    stats = lax.fori_loop(0, last, lambda c, s: chunk(c, s, False), stats)
    _, l = chunk(last, stats, True)
    ot = acc_sc[...] / l
    for g in range(2):
        og = ot[:, (2 * g) * tq:(2 * g + 1) * tq] - lam * ot[:, (2 * g + 1) * tq:(2 * g + 2) * tq]
        ms = jnp.mean(og * og, axis=0, keepdims=True)
        nrm = (og * lax.rsqrt(ms + EPS)) * sub_ref[...] * (1.0 - lam_init)
        o_ref[:, g * LANES:(g + 1) * LANES] = nrm.T


def _attn_prompt(u, lams, subln, bsz, seq, lam_init):
    tq = _tile(seq, (512, 256, 128))
    tk = tq
    nq = seq // tq
    vec = pl.BlockSpec((1, ATTN_DH), lambda b, h, i: (0, 0))
    return pl.pallas_call(
        functools.partial(_attn_prompt_body, tq=tq, tk=tk, lam_init=lam_init),
        grid=(bsz, ATTN_KV_HEADS, nq),
        in_specs=[vec, vec, vec, vec,
                  pl.BlockSpec((ATTN_VD, 1), lambda b, h, i: (0, 0)),
                  pl.BlockSpec((tq, 2 * LANES), lambda b, h, i: (b * nq + i, h)),
                  pl.BlockSpec((seq, LANES), lambda b, h, i: (b, K0 // LANES + h)),
                  pl.BlockSpec((seq, LANES), lambda b, h, i: (b, V0 // LANES + h))],
        out_specs=[pl.BlockSpec((tq, 2 * LANES), lambda b, h, i: (b * nq + i, h)),
                   pl.BlockSpec((seq * ATTN_KV_HEADS, ATTN_VD), lambda b, h, i: (b, 0)),
                   pl.BlockSpec((seq * ATTN_KV_HEADS, ATTN_VD), lambda b, h, i: (b, 0))],
        out_shape=[jax.ShapeDtypeStruct((bsz * seq, ATTN_HEADS * ATTN_VD), F32),
                   jax.ShapeDtypeStruct((bsz * seq * ATTN_KV_HEADS, ATTN_VD), F32),
                   jax.ShapeDtypeStruct((bsz * seq * ATTN_KV_HEADS, ATTN_VD), F32)],
        scratch_shapes=[pltpu.VMEM((ATTN_VD, 4 * tq), F32)],
        compiler_params=_cparams(("arbitrary", "arbitrary", "arbitrary"), 56),
        name="attn_prompt",
    )(*lams, subln.reshape(ATTN_VD, 1), u, u, u)


def _attn_sample_body(pt_ref, lq1, lk1, lq2, lk2, sub_ref, q_ref, kn_ref, vn_ref, *rest, n_pg, lam_init, tdec):
    del pt_ref
    k_refs, v_refs = rest[:n_pg], rest[n_pg:2 * n_pg]
    o_ref, m_sc, l_sc, acc_sc, qr_sc = rest[2 * n_pg:]
    j = pl.program_id(1)
    nrow = ATTN_KV_HEADS * 2 * 2 * tdec
    page_tok = k_refs[0].shape[0] // ATTN_KV_HEADS

    @pl.when(j == 0)
    def _():
        q = q_ref[...] * (ATTN_DH ** -0.5)
        lane = lax.broadcasted_iota(I32, (tdec, LANES), 1)
        qr_sc[...] = jnp.zeros_like(qr_sc)
        for kv in range(ATTN_KV_HEADS):
            for c in range(2):
                for g in range(2):
                    hd = kv * 2 + g
                    qh = q[:, hd * LANES:(hd + 1) * LANES]
                    n0 = ((kv * 2 + c) * 2 + g) * tdec
                    qr_sc[n0:n0 + tdec, kv * LANES:(kv + 1) * LANES] = jnp.where(
                        (lane >= c * ATTN_DH) & (lane < (c + 1) * ATTN_DH), qh, 0.0)
        m_sc[...] = jnp.full_like(m_sc, -jnp.inf)
        l_sc[...] = jnp.zeros_like(l_sc)
        acc_sc[...] = jnp.zeros_like(acc_sc)

    qr = qr_sc[...].astype(BF16)

    def page(ref):
        return jnp.concatenate([ref[pl.ds(kv, page_tok, stride=ATTN_KV_HEADS), :] for kv in range(ATTN_KV_HEADS)],
                               axis=1).astype(BF16)

    def update(s_list, v_list):
        m_old = m_sc[...]
        m_new = m_old
        for s in s_list:
            m_new = jnp.maximum(m_new, jnp.max(s, axis=1, keepdims=True))
        alpha = jnp.exp(m_old - m_new)
        l = alpha * l_sc[...]
        pv = None
        for s, v in zip(s_list, v_list):
            p = jnp.exp(s - m_new)
            l = l + jnp.sum(p, axis=1, keepdims=True)
            d = _mm(p.astype(BF16), v)
            pv = d if pv is None else pv + d
        m_sc[...] = m_new
        l_sc[...] = l
        acc_sc[...] = alpha * acc_sc[...] + pv

    for g0 in range(0, n_pg, PAGE_GROUP):
        grp = range(g0, min(g0 + PAGE_GROUP, n_pg))
        update([_nt(qr, page(k_refs[i])) for i in grp], [page(v_refs[i]) for i in grp])

    @pl.when(j == pl.num_programs(1) - 1)
    def _():
        lam = _lambda(lq1, lk1, lq2, lk2, lam_init)
        pad = jnp.zeros((LANES - tdec, kn_ref.shape[1]), F32)
        kn = jnp.concatenate([kn_ref[...], pad], axis=0).astype(BF16)
        vn = jnp.concatenate([vn_ref[...], pad], axis=0).astype(BF16)
        s = _nt(qr, kn)
        row = jnp.bitwise_and(lax.broadcasted_iota(I32, (nrow, LANES), 0), tdec - 1)
        col = lax.broadcasted_iota(I32, (nrow, LANES), 1)
        update([jnp.where(col <= row, s, -jnp.inf)], [vn])
        o = acc_sc[...] / l_sc[...]
        for kv in range(ATTN_KV_HEADS):
            for g in range(2):
                n0 = ((kv * 2 + 0) * 2 + g) * tdec
                n1 = ((kv * 2 + 1) * 2 + g) * tdec
                og = (o[n0:n0 + tdec, kv * LANES:(kv + 1) * LANES]
                      - lam * o[n1:n1 + tdec, kv * LANES:(kv + 1) * LANES])
                hd = kv * 2 + g
                o_ref[:, hd * LANES:(hd + 1) * LANES] = _rms(og, sub_ref[...]) * (1.0 - lam_init)


def _attn_sample(u, cache_k, cache_v, page_flat, layer, n_pool, lams, subln, bsz, tdec, n_pages, lam_init):
    del n_pool
    n_pg = _tile(n_pages, (32, 16, 8, 4, 2, 1))
    nchunks = n_pages // n_pg
    prow = cache_k.shape[2]
    width = ATTN_KV_HEADS * LANES
    nrow = ATTN_KV_HEADS * 2 * 2 * tdec
    assert nrow == LANES and tdec == SUBLANES
    vec = pl.BlockSpec((1, ATTN_DH), lambda b, j, pt: (0, 0))

    def page_spec(i):
        return pl.BlockSpec((None, None, prow, ATTN_VD),
                            lambda b, j, pt: (layer, pt[b * n_pages + j * n_pg + i], 0, 0))

    in_specs = [vec, vec, vec, vec,
                pl.BlockSpec((1, ATTN_VD), lambda b, j, pt: (0, 0)),
                pl.BlockSpec((tdec, ATTN_HEADS * ATTN_VD), lambda b, j, pt: (b, 0)),
                pl.BlockSpec((tdec, width), lambda b, j, pt: (b, K0 // width)),
                pl.BlockSpec((tdec, width), lambda b, j, pt: (b, V0 // width))]
    in_specs += [page_spec(i) for i in range(n_pg)] * 2
    return pl.pallas_call(
        functools.partial(_attn_sample_body, n_pg=n_pg, lam_init=lam_init, tdec=tdec),
        grid_spec=pltpu.PrefetchScalarGridSpec(
            num_scalar_prefetch=1,
            grid=(bsz, nchunks),
            in_specs=in_specs,
            out_specs=pl.BlockSpec((tdec, ATTN_HEADS * ATTN_VD), lambda b, j, pt: (b, 0)),
            scratch_shapes=[pltpu.VMEM((nrow, 1), F32), pltpu.VMEM((nrow, 1), F32),
                            pltpu.VMEM((nrow, width), F32), pltpu.VMEM((nrow, width), F32)]),
        out_shape=jax.ShapeDtypeStruct((bsz * tdec, ATTN_HEADS * ATTN_VD), F32),
        compiler_params=_cparams(("arbitrary", "arbitrary"), 56),
        name="attn_sample",
    )(page_flat, *lams, subln, u, u, u, *([cache_k] * n_pg), *([cache_v] * n_pg))


def _causal_conv(x, prev, cw_ref, cb_ref):
    taps = cw_ref.shape[0]
    rowi = lax.broadcasted_iota(I32, x.shape, 0)
    acc = cb_ref[...] + x * cw_ref[taps - 1:taps, :]
    for s in range(1, taps):
        sh = jnp.where(rowi < s, pltpu.roll(prev, s, 0), pltpu.roll(x, s, 0))
        acc = acc + sh * cw_ref[taps - 1 - s:taps - s, :]
    return acc


def _pad_rows(x, rows):
    if x.shape[0] == rows:
        return x
    return jnp.concatenate([x, jnp.zeros((rows - x.shape[0], x.shape[1]), x.dtype)], axis=0)


def _ssd_body(xbc_ref, z_ref, dt_ref, st_ref, h0_ref, cw_ref, cb_ref, dtb_ref, alog_ref, dvec_ref, nw_ref,
              y_ref, hout_ref, prev_sc, h_sc, *, rows):
    c = pl.program_id(1)
    L = SSD_CHUNK
    width = SSD_HEADS * SSD_HEAD_DIM
    gw = SSD_GROUPS * SSD_STATE

    @pl.when(c == 0)
    def _():
        prev_sc[...] = jnp.concatenate([jnp.zeros((L - SUBLANES, prev_sc.shape[1]), F32), st_ref[...]], axis=0)
        h_sc[...] = h0_ref[...]

    x = _pad_rows(xbc_ref[...], L)
    conv = _causal_conv(x, prev_sc[...], cw_ref, cb_ref)
    prev_sc[...] = x
    xc = conv * jax.nn.sigmoid(conv)
    xs, bm, cm = xc[:, :width], xc[:, width:width + gw], xc[:, width + gw:]

    dt = jax.nn.softplus(_pad_rows(dt_ref[...], L) + dtb_ref[...])
    if rows < L:
        dt = jnp.where(lax.broadcasted_iota(I32, dt.shape, 0) < rows, dt, 0.0)
    da = dt * (-jnp.exp(alog_ref[...]))
    r0 = lax.broadcasted_iota(I32, (L, L), 0)
    c0 = lax.broadcasted_iota(I32, (L, L), 1)
    causal = r0 >= c0
    acum = jnp.dot(causal.astype(F32), da, precision=lax.Precision.HIGHEST, preferred_element_type=F32)
    acum_t, dt_t = acum.T, dt.T
    last = acum[L - 1:L, :]
    wend = jnp.exp(last - acum) * dt
    eac = jnp.exp(acum)
    elast = jnp.exp(last)
    lane = lax.broadcasted_iota(I32, (L, LANES), 1)
    first = lane < SSD_HEAD_DIM
    top = lax.broadcasted_iota(I32, (2 * SSD_HEAD_DIM, SSD_STATE), 0) < SSD_HEAD_DIM

    ys = []
    for pr in range(SSD_HEADS // 2):
        g = (2 * pr) // (SSD_HEADS // SSD_GROUPS)
        bg = bm[:, g * SSD_STATE:(g + 1) * SSD_STATE].astype(BF16)
        cg = cm[:, g * SSD_STATE:(g + 1) * SSD_STATE].astype(BF16)
        gmat = _nt(cg, bg)
        xp = xs[:, pr * LANES:(pr + 1) * LANES]
        xpb = xp.astype(BF16)
        hp = h_sc[pr * LANES:(pr + 1) * LANES, :]
        outs = []
        for hh in (2 * pr, 2 * pr + 1):
            seg = acum[:, hh:hh + 1] - acum_t[hh:hh + 1, :]
            dec = jnp.exp(jnp.where(causal, seg, -jnp.inf))
            w = gmat * dec * dt_t[hh:hh + 1, :]
            outs.append(_mm(w.astype(BF16), xpb))
        h_a, h_b = 2 * pr, 2 * pr + 1
        y_intra = jnp.where(first, outs[0], outs[1])
        e_pair = jnp.where(first, eac[:, h_a:h_a + 1], eac[:, h_b:h_b + 1])
        ys.append(y_intra + _nt(cg, hp.astype(BF16)) * e_pair)
        w_pair = jnp.where(first, wend[:, h_a:h_a + 1], wend[:, h_b:h_b + 1])
        upd = _mm((xp * w_pair).T.astype(BF16), bg)
        keep = jnp.where(top, elast[:, h_a:h_a + 1], elast[:, h_b:h_b + 1])
        h_sc[pr * LANES:(pr + 1) * LANES, :] = hp * keep + upd

    y = jnp.concatenate(ys, axis=1) + dvec_ref[...] * xs
    zz = _pad_rows(z_ref[...], L)
    gated = y * (zz * jax.nn.sigmoid(zz))
    y_ref[...] = _rms(gated, nw_ref[...])[:rows]

    @pl.when(c == pl.num_programs(1) - 1)
    def _():
        hout_ref[...] = h_sc[...]


def _ssd(u, st8, h0, layer, cw, cb, dtb, alog, dvec, nw, bsz, seq):
    rows = SSD_CHUNK if seq % SSD_CHUNK == 0 else seq
    assert rows == SSD_CHUNK or (rows == seq and rows % SUBLANES == 0 and rows <= SSD_CHUNK)
    nch = seq // rows
    cc = cw.shape[1]
    width = SSD_HEADS * SSD_HEAD_DIM
    hp = SSD_HEADS * SSD_HEAD_DIM
    const = lambda shape: pl.BlockSpec(shape, lambda b, c: (0,) * len(shape))
    return pl.pallas_call(
        functools.partial(_ssd_body, rows=rows),
        grid=(bsz, nch),
        in_specs=[pl.BlockSpec((rows, cc), lambda b, c: (b * nch + c, XBC0 // cc)),
                  pl.BlockSpec((rows, width), lambda b, c: (b * nch + c, Z0 // width)),
                  pl.BlockSpec((rows, LANES), lambda b, c: (b * nch + c, DT0 // LANES)),
                  pl.BlockSpec((None, SUBLANES, cc), lambda b, c: (b, 0, 0)),
                  pl.BlockSpec((None, None, hp, SSD_STATE), lambda b, c: (layer, b, 0, 0)),
                  const((SSD_CONV, cc)), const((1, cc)), const((1, LANES)), const((1, LANES)),
                  const((1, width)), const((1, width))],
        out_specs=[pl.BlockSpec((rows, width), lambda b, c: (b * nch + c, 0)),
                   pl.BlockSpec((None, hp, SSD_STATE), lambda b, c: (b, 0, 0))],
        out_shape=[jax.ShapeDtypeStruct((bsz * seq, width), F32),
                   jax.ShapeDtypeStruct((bsz, hp, SSD_STATE), F32)],
        scratch_shapes=[pltpu.VMEM((SSD_CHUNK, cc), F32), pltpu.VMEM((hp, SSD_STATE), F32)],
        compiler_params=_cparams(("arbitrary", "arbitrary"), 32),
        name="ssd",
    )(u, u, u, st8, h0, cw, cb, dtb, alog, dvec, nw)


def _expm1(t):
    u = jnp.exp(t)
    small = (u - 1.0) * t / jnp.log(u)
    return jnp.where(t < -1.0, u - 1.0, jnp.where(u == 1.0, t, small))


def _lru_body(xr_ref, gate_ref, st_ref, h0_ref, cw_ref, cb_ref, wa_ref, ba_ref, wx_ref, bx_ref, lam_ref,
              y_ref, hout_ref, prev_sc, h_sc, *, rows):
    c = pl.program_id(1)

    @pl.when(c == 0)
    def _():
        st = st_ref[...]
        if rows > SUBLANES:
            st = jnp.concatenate([jnp.zeros((rows - SUBLANES, st.shape[1]), F32), st], axis=0)
        prev_sc[...] = st
        h_sc[...] = h0_ref[...]

    x = xr_ref[...]
    xc = _causal_conv(x, prev_sc[...], cw_ref, cb_ref)
    prev_sc[...] = x
    xcb = xc.astype(BF16)
    r = jax.nn.sigmoid(_mm(xcb, wa_ref[...]) + ba_ref[...])
    ig = jax.nn.sigmoid(_mm(xcb, wx_ref[...]) + bx_ref[...])
    log_a = (-LRU_C) * r * jax.nn.softplus(-lam_ref[...])
    a = jnp.exp(log_a)
    b = jnp.sqrt(-_expm1(2.0 * log_a)) * (ig * xc)
    rowi = lax.broadcasted_iota(I32, a.shape, 0)
    d = 1
    while d < rows:
        ok = rowi >= d
        b = jnp.where(ok, a * pltpu.roll(b, d, 0) + b, b)
        a = jnp.where(ok, a * pltpu.roll(a, d, 0), a)
        d *= 2
    h = b + a * h_sc[...]
    h_sc[...] = h[rows - 1:rows, :]
    y_ref[...] = h * jax.nn.gelu(gate_ref[...])

    @pl.when(c == pl.num_programs(1) - 1)
    def _():
        hout_ref[...] = h[rows - 1:rows, :]


def _lru(u, st8, h0, cw, cb, wa, ba, wx, bx, lam, bsz, seq):
    rows = _tile(seq, (128, 64, 32, 16, 8))
    nch = seq // rows
    w = cw.shape[1]
    const = lambda shape: pl.BlockSpec(shape, lambda b, c: (0,) * len(shape))
    return pl.pallas_call(
        functools.partial(_lru_body, rows=rows),
        grid=(bsz, nch),
        in_specs=[pl.BlockSpec((rows, w), lambda b, c: (b * nch + c, XR0 // w)),
                  pl.BlockSpec((rows, w), lambda b, c: (b * nch + c, GATE0 // w)),
                  pl.BlockSpec((None, SUBLANES, w), lambda b, c: (b, 0, 0)),
                  pl.BlockSpec((None, 1, w), lambda b, c: (b, 0, 0)),
                  const((LRU_CONV, w)), const((1, w)), const((w, w)), const((1, w)), const((w, w)), const((1, w)),
                  const((1, w))],
        out_specs=[pl.BlockSpec((rows, w), lambda b, c: (b * nch + c, 0)),
                   pl.BlockSpec((None, 1, w), lambda b, c: (b, 0, 0))],
        out_shape=[jax.ShapeDtypeStruct((bsz * seq, w), F32), jax.ShapeDtypeStruct((bsz, 1, w), F32)],
        scratch_shapes=[pltpu.VMEM((rows, w), F32), pltpu.VMEM((1, w), F32)],
        compiler_params=_cparams(("arbitrary", "arbitrary"), 32),
        name="lru",
    )(u, u, st8, h0, cw, cb, wa, ba, wx, bx, lam)


def _out_proj_body(x_ref, a_ref, s_ref, l_ref, wo_ref, nw_ref, wq_ref, x1_ref, qm_ref):
    wa, ws = a_ref.shape[1], s_ref.shape[1]
    acc = x_ref[...] + _mm(a_ref[...].astype(BF16), wo_ref[0:wa, :])
    acc = acc + _mm(s_ref[...].astype(BF16), wo_ref[wa:wa + ws, :])
    acc = acc + _mm(l_ref[...].astype(BF16), wo_ref[wa + ws:, :])
    x1_ref[...] = acc
    qm_ref[...] = _mm(_rms(acc, nw_ref[...]).astype(BF16), wq_ref[...])


def _out_proj(x, attn, ssd, lru, wo, nw, wq):
    t, d = x.shape
    tm = _tile(t, (256, 128))
    row = lambda w: pl.BlockSpec((tm, w), lambda i: (i, 0))
    const = lambda shape: pl.BlockSpec(shape, lambda i: (0, 0))
    return pl.pallas_call(
        _out_proj_body,
        grid=(t // tm,),
        in_specs=[row(d), row(attn.shape[1]), row(ssd.shape[1]), row(lru.shape[1]),
                  const(wo.shape), const((1, d)), const(wq.shape)],
        out_specs=[row(d), row(wq.shape[1])],
        out_shape=[jax.ShapeDtypeStruct((t, d), F32), jax.ShapeDtypeStruct((t, wq.shape[1]), F32)],
        compiler_params=_cparams(("arbitrary",), 48),
        name="out_proj",
    )(x, attn, ssd, lru, wo, nw, wq)


def _mem_kv_body(m_ref, wk_ref, wv_ref, k_ref, v_ref):
    mb = m_ref[...].astype(BF16)
    k_ref[...] = _mm(mb, wk_ref[...])
    v_ref[...] = _mm(mb, wv_ref[...])


def _mem_kv(mem, wk, wv):
    t, d = mem.shape
    tm = _tile(t, (256, 128))
    w = wk.shape[1]
    return pl.pallas_call(
        _mem_kv_body,
        grid=(t // tm,),
        in_specs=[pl.BlockSpec((tm, d), lambda i: (i, 0)),
                  pl.BlockSpec((d, w), lambda i: (0, 0)), pl.BlockSpec((d, w), lambda i: (0, 0))],
        out_specs=[pl.BlockSpec((tm, w), lambda i: (i, 0)), pl.BlockSpec((tm, w), lambda i: (i, 0))],
        out_shape=[jax.ShapeDtypeStruct((t, w), F32), jax.ShapeDtypeStruct((t, w), F32)],
        compiler_params=_cparams(("arbitrary",), 32),
        name="mem_kv",
    )(mem, wk, wv)


def _xattn_body(q_ref, k_ref, v_ref, o_ref, *, head_rows):
    for h in range(MEM_HEADS):
        sl = slice(h * MEM_DH, (h + 1) * MEM_DH)
        if head_rows:
            rows = pl.ds(h, k_ref.shape[0] // MEM_HEADS, stride=MEM_HEADS)
            k, v = k_ref[rows, :], v_ref[rows, :]
        else:
            k, v = k_ref[:, sl], v_ref[:, sl]
        s = _nt(q_ref[:, sl].astype(BF16), k.astype(BF16)) * (MEM_DH ** -0.5)
        e = jnp.exp(s - jnp.max(s, axis=1, keepdims=True))
        o = _mm(e.astype(BF16), v.astype(BF16))
        o_ref[:, sl] = o / jnp.sum(e, axis=1, keepdims=True)


def _xattn(qm, mem_k, mem_v, bsz, seq, layer=None):
    tq = _tile(seq, (256, 128))
    nq = seq // tq
    w = qm.shape[1]
    if layer is None:
        mem_spec = pl.BlockSpec((None,) + mem_k.shape[1:], lambda b, i: (b, 0, 0))
    else:
        mem_spec = pl.BlockSpec((None, None) + mem_k.shape[2:], lambda b, i: (layer, b, 0, 0))
    return pl.pallas_call(
        functools.partial(_xattn_body, head_rows=layer is not None),
        grid=(bsz, nq),
        in_specs=[pl.BlockSpec((tq, w), lambda b, i: (b * nq + i, 0)), mem_spec, mem_spec],
        out_specs=pl.BlockSpec((tq, w), lambda b, i: (b * nq + i, 0)),
        out_shape=jax.ShapeDtypeStruct(qm.shape, F32),
        compiler_params=_cparams(("arbitrary", "arbitrary"), 32),
        name="xattn",
    )(qm, mem_k, mem_v)


def _route_body(*refs, n_own):
    hf_ref = refs[-3]
    i = pl.program_id(0)

    @pl.when(i < n_own)
    def _():
        _route_tile(*refs)

    @pl.when(i >= n_own)
    def _():
        hf_ref[...] = jnp.zeros_like(hf_ref)


def _route_tile(x1_ref, om_ref, wo_ref, nw_ref, rw_ref, rb_ref, *rest):
    x2_ref, hf_ref, eid_ref, gate_ref = rest[-4:]
    x2 = x1_ref[...] + _mm(om_ref[...].astype(BF16), wo_ref[...])
    x2_ref[...] = x2
    hf = _rms(x2, nw_ref[...])
    hf_ref[...] = hf
    logits = jnp.dot(hf, rw_ref[...], precision=lax.Precision.HIGHEST, preferred_element_type=F32) + rb_ref[...]
    lane = lax.broadcasted_iota(I32, logits.shape, 1).astype(F32)
    ninf = -jnp.inf
    big = float(LANES)

    def first_argmax(v, mx):
        return jnp.min(jnp.where(v == mx, lane, big), axis=1, keepdims=True)

    lg = jnp.where(lane < MOE_GROUPS, logits, ninf)
    mg = jnp.max(lg, axis=1, keepdims=True)
    p_grp = 1.0 / jnp.sum(jnp.exp(lg - mg), axis=1, keepdims=True)
    lo = MOE_GROUPS + MOE_EXPERTS_PER_GROUP * first_argmax(lg, mg)
    le = jnp.where((lane >= lo) & (lane < lo + MOE_EXPERTS_PER_GROUP), logits, ninf)
    v1 = jnp.max(le, axis=1, keepdims=True)
    i1 = first_argmax(le, v1)
    le2 = jnp.where(lane == i1, ninf, le)
    v2 = jnp.max(le2, axis=1, keepdims=True)
    i2 = first_argmax(le2, v2)
    e2 = jnp.exp(v2 - v1)
    den = 1.0 + e2
    g1 = p_grp * (1.0 / den)
    g2 = p_grp * (e2 / den)
    gate_ref[...] = jnp.where(lane == 0.0, g1, jnp.where(lane == 1.0, g2, 0.0))
    eid_ref[...] = jnp.where(lane == 0.0, i1 - MOE_GROUPS, jnp.where(lane == 1.0, i2 - MOE_GROUPS, 0.0)).astype(I32)


def _route(x1, om, wo, nw, rw, rb, hf_rows, hf_row0, hf_shared):
    t, d = x1.shape
    creating = hf_shared is None
    tail = hf_rows - hf_row0 - t if creating else 0
    tm = _tile(math.gcd(t, hf_row0, tail), (256, 128, 64, 32, 16, 8))
    blk0 = hf_row0 // tm
    n_own = t // tm
    n_steps = n_own + tail // tm
    row = lambda w: pl.BlockSpec((tm, w), lambda i: (jnp.minimum(i, n_own - 1), 0))
    const = lambda shape: pl.BlockSpec(shape, lambda i: (0, 0))
    in_specs = [row(d), row(om.shape[1]), const(wo.shape), const((1, d)), const(rw.shape), const((1, LANES))]
    args = [x1, om, wo, nw, rw, rb]
    aliases = {}
    if hf_shared is not None:
        in_specs.append(pl.BlockSpec(memory_space=pl.ANY))
        args.append(hf_shared)
        aliases = {len(args) - 1: 1}
    return pl.pallas_call(
        functools.partial(_route_body, n_own=n_own),
        grid=(n_steps,),
        in_specs=in_specs,
        out_specs=[row(d), pl.BlockSpec((tm, d), lambda i: (blk0 + i, 0)), row(LANES), row(LANES)],
        out_shape=[jax.ShapeDtypeStruct((t, d), F32), jax.ShapeDtypeStruct((hf_rows, d), F32),
                   jax.ShapeDtypeStruct((t, LANES), I32), jax.ShapeDtypeStruct((t, LANES), F32)],
        input_output_aliases=aliases,
        compiler_params=_cparams(("arbitrary",), 40),
        name="route",
    )(*args)


ROW_UNROLL = 8


def _moe_body(grow_ref, srow_ref, be_ref, base_ref, cnt_ref, hf_hbm, wg_ref, wu_ref, wd_ref, o_hbm,
              xbuf, ybuf, wgb, wub, wdb, gsem, ssem, *, rb):
    g = pl.program_id(0)
    ng = pl.num_programs(0)
    slot = lax.rem(g, 2)
    cnt = cnt_ref[g]

    def for_rows(n, fn):
        n_grp = lax.shift_right_logical(n, ROW_UNROLL.bit_length() - 1)

        def grp(q, c):
            for u in range(ROW_UNROLL):
                fn(q * ROW_UNROLL + u)
            return c

        def one(i, c):
            fn(i)
            return c

        lax.fori_loop(0, n_grp, grp, 0)
        lax.fori_loop(n_grp * ROW_UNROLL, n, one, 0)

    def gather_row(sl, i, src):
        return pltpu.make_async_copy(hf_hbm.at[pl.ds(src, 1), :], xbuf.at[sl, pl.ds(i, 1), :], gsem.at[sl])

    def scatter_row(sl, i, dst):
        return pltpu.make_async_copy(ybuf.at[sl, pl.ds(i, 1), :], o_hbm.at[pl.ds(dst, 1), :], ssem.at[sl])

    def start_gathers(blk, sl):
        base = base_ref[blk]
        for_rows(cnt_ref[blk], lambda i: gather_row(sl, i, grow_ref[base + i]).start())

    def start_scatters(blk, sl):
        base = base_ref[blk]
        for_rows(cnt_ref[blk], lambda i: scatter_row(sl, i, srow_ref[base + i]).start())

    def wait_rows(blk, sl, row_copy, block_copy):
        n = cnt_ref[blk]

        @pl.when(n == rb)
        def _():
            block_copy.wait()

        @pl.when(n < rb)
        def _():
            for_rows(n, lambda i: row_copy(sl, i, 0).wait())

    def wait_gathers(blk, sl):
        wait_rows(blk, sl, gather_row,
                  pltpu.make_async_copy(hf_hbm.at[pl.ds(0, rb), :], xbuf.at[sl], gsem.at[sl]))

    def wait_scatters(blk, sl):
        wait_rows(blk, sl, scatter_row,
                  pltpu.make_async_copy(ybuf.at[sl], o_hbm.at[pl.ds(0, rb), :], ssem.at[sl]))

    @pl.when(g == 0)
    def _():
        xbuf[...] = jnp.zeros_like(xbuf)
        start_gathers(0, 0)

    prev_e = be_ref[jnp.maximum(g - 1, 0)]

    @pl.when((cnt > 0) & ((g == 0) | (prev_e != be_ref[g])))
    def _():
        wgb[...] = wg_ref[...].astype(BF16)
        wub[...] = wu_ref[...].astype(BF16)
        wdb[...] = wd_ref[...].astype(BF16)

    wait_gathers(g, slot)

    @pl.when(g + 1 < ng)
    def _():
        start_gathers(g + 1, 1 - slot)

    @pl.when(g >= 2)
    def _():
        wait_scatters(g - 2, slot)

    @pl.when(cnt > 0)
    def _():
        xb = xbuf[slot].astype(BF16)
        hg = _mm(xb, wgb[...])
        act = (hg * jax.nn.sigmoid(hg)) * _mm(xb, wub[...])
        ybuf[slot] = _mm(act.astype(BF16), wdb[...])
        start_scatters(g, slot)

    @pl.when(g == ng - 1)
    def _():
        @pl.when(g >= 1)
        def _():
            wait_scatters(g - 1, 1 - slot)

        wait_scatters(g, slot)


def _moe_ffn(hf, grow, srow, blk_e, blk_base, blk_cnt, wg, wu, wd, layer, rb):
    t, d = hf.shape
    n_asg = grow.shape[0]
    ff = wg.shape[-1]
    n_blk = blk_e.shape[0]
    wspec = lambda shape: pl.BlockSpec((None, None) + shape, lambda g, gr, sr, be, bb, bc: (layer, be[g], 0, 0))
    return pl.pallas_call(
        functools.partial(_moe_body, rb=rb),
        grid_spec=pltpu.PrefetchScalarGridSpec(
            num_scalar_prefetch=5,
            grid=(n_blk,),
            in_specs=[pl.BlockSpec(memory_space=pl.ANY), wspec((d, ff)), wspec((d, ff)), wspec((ff, d))],
            out_specs=pl.BlockSpec(memory_space=pl.ANY),
            scratch_shapes=[pltpu.VMEM((2, rb, d), F32), pltpu.VMEM((2, rb, d), F32),
                            pltpu.VMEM((d, ff), BF16), pltpu.VMEM((d, ff), BF16), pltpu.VMEM((ff, d), BF16),
                            pltpu.SemaphoreType.DMA((2,)), pltpu.SemaphoreType.DMA((2,))]),
        out_shape=jax.ShapeDtypeStruct((n_asg, d), F32),
        compiler_params=_cparams(("arbitrary",), 56),
        name="moe_ffn",
    )(grow, srow, blk_e, blk_base, blk_cnt, hf, wg, wu, wd)


def _moe_plan(eid, n_tok, rb):
    n_asg = n_tok * MOE_TOPK
    flat_e = eid.reshape(-1)
    order = jnp.argsort(flat_e, stable=True).astype(I32)
    grow = order // MOE_TOPK
    srow = (order % MOE_TOPK) * n_tok + grow
    counts = jnp.sum((flat_e[:, None] == jnp.arange(MOE_EXPERTS, dtype=I32)[None, :]).astype(I32), axis=0)
    starts = jnp.cumsum(counts) - counts
    nblk = (counts + rb - 1) // rb
    blk_end = jnp.cumsum(nblk)
    n_blk = -(-n_asg // rb) + MOE_EXPERTS
    gi = jnp.arange(n_blk, dtype=I32)
    be = jnp.minimum(jnp.sum((gi[:, None] >= blk_end[None, :]).astype(I32), axis=1), MOE_EXPERTS - 1)
    r = gi - (blk_end - nblk)[be]
    cnt = jnp.clip(counts[be] - r * rb, 0, rb).astype(I32)
    base = jnp.minimum(starts[be] + r * rb, n_asg - 1).astype(I32)
    return grow.astype(I32), srow.astype(I32), be, base, cnt


def _combine_body(x2_ref, o0_ref, o1_ref, g_ref, fw_ref, x3_ref, *, final):
    g = g_ref[...]
    x3 = x2_ref[...] + (g[:, 0:1] * o0_ref[...] + g[:, 1:2] * o1_ref[...])
    x3_ref[...] = _rms(x3, fw_ref[...]) if final else x3


def _combine(x2, o2, gates, fw, final, row0, n_all):
    t, d = x2.shape
    tm = _tile(math.gcd(t, row0, n_all), (256, 128, 64, 32, 16, 8))
    assert t % tm == 0 and row0 % tm == 0 and n_all % tm == 0
    nt = t // tm
    b0, b1 = row0 // tm, (n_all + row0) // tm
    return pl.pallas_call(
        functools.partial(_combine_body, final=final),
        grid=(nt,),
        in_specs=[pl.BlockSpec((tm, d), lambda i: (i, 0)),
                  pl.BlockSpec((tm, d), lambda i: (b0 + i, 0)),
                  pl.BlockSpec((tm, d), lambda i: (b1 + i, 0)),
                  pl.BlockSpec((tm, LANES), lambda i: (i, 0)),
                  pl.BlockSpec((1, d), lambda i: (0, 0))],
        out_specs=pl.BlockSpec((tm, d), lambda i: (i, 0)),
        out_shape=jax.ShapeDtypeStruct((t, d), F32),
        compiler_params=_cparams(("arbitrary",), 32),
        name="combine",
    )(x2, o2, o2, gates, fw)


def _rope_tables(pos):
    half = ATTN_DH // 2
    inv = ROPE_THETA ** (-jnp.arange(half, dtype=F32) / half)
    ang = pos.astype(F32)[:, None] * inv[None, :]
    cos, sin = jnp.cos(ang), jnp.sin(ang)
    reps = LANES // ATTN_DH
    return jnp.tile(jnp.concatenate([cos, cos], axis=1), (1, reps)), jnp.tile(jnp.concatenate([-sin, sin], axis=1), (1, reps))


def _block_diag(w):
    g, a, b = w.shape
    eye = jnp.eye(g, dtype=w.dtype)
    return (w[:, :, None, :] * eye[:, None, :, None]).reshape(g * a, g * b)


def _pad_lanes(v, n):
    return jnp.pad(v.reshape(1, -1), ((0, 0), (0, n - v.size)))


def _layer_front(x, grp, lw, layer, lam_init, hf_rows, hf_row0, hf_shared):
    bsz, seq = grp["bsz"], grp["seq"]
    u = _in_proj(x, lw["norm_mix"], lw["w_in"], grp["cos"], grp["sin"])
    lams = lw["lams"]
    if grp["paged"]:
        attn = _attn_sample(u, grp["cache_k"], grp["cache_v"], grp["page_flat"], layer, grp["n_pool"], lams,
                            lw["subln"], bsz, seq, grp["n_pages"], lam_init)
        k_new = v_new = None
    else:
        attn, k_new, v_new = _attn_prompt(u, lams, lw["subln"], bsz, seq, lam_init)
    ssd, h_ssd = _ssd(u, grp["ssd_conv"][layer], grp["ssd_state"], layer, lw["ssd_conv_w"], lw["ssd_conv_b"],
                      lw["ssd_dt_bias"], lw["ssd_a_log"], lw["ssd_d"], lw["ssd_norm"], bsz, seq)
    lru, h_lru = _lru(u, grp["lru_conv"][layer], grp["lru_state"][layer], lw["lru_conv_w"], lw["lru_conv_b"],
                      lw["lru_wa"], lw["lru_ba"], lw["lru_wx"], lw["lru_bx"], lw["lru_lambda"], bsz, seq)
    x1, qm = _out_proj(x, attn, ssd, lru, lw["w_out"], lw["norm_mem"], lw["wq_mem"])
    if grp["paged"]:
        om = _xattn(qm, grp["mem_k"], grp["mem_v"], bsz, seq, layer)
    else:
        om = _xattn(qm, grp["mem_k"], grp["mem_v"], bsz, seq)
    routed = _route(x1, om, lw["wo_mem"], lw["norm_ffn"], lw["router_w"], lw["router_b"],
                    hf_rows, hf_row0, hf_shared)
    ur = u.reshape(bsz, seq, UW)
    width_kv = ATTN_KV_HEADS * ATTN_VD
    kv_shape = (bsz, seq, ATTN_KV_HEADS, ATTN_VD)
    outs = dict(
        k=(ur[:, :, K0:K0 + width_kv] if k_new is None else k_new).reshape(kv_shape),
        v=(ur[:, :, V0:V0 + width_kv] if v_new is None else v_new).reshape(kv_shape),
        ssd_conv=ur[:, seq - (SSD_CONV - 1):, XBC0:Z0],
        ssd_state=h_ssd.reshape(bsz, SSD_HEADS, SSD_HEAD_DIM, SSD_STATE),
        lru_conv=ur[:, seq - (LRU_CONV - 1):, XR0:GATE0],
        lru_state=h_lru.reshape(bsz, -1),
    )
    return routed, outs


def _moe_groups(routed, lw, layer, final, final_norm):
    hf = routed[-1][1]
    eid = jnp.concatenate([r[2][:, :MOE_TOPK] for r in routed], axis=0)
    n_tok = hf.shape[0]
    rb = 256 if n_tok * MOE_TOPK >= 256 * MOE_EXPERTS else 128
    plan = _moe_plan(eid, n_tok, rb)
    o2 = _moe_ffn(hf, *plan, lw["moe_w_gate"], lw["moe_w_up"], lw["moe_w_down"], layer, rb)
    outs, row0 = [], 0
    for x2, _, _, gate in routed:
        outs.append(_combine(x2, o2, gate, final_norm, final, row0, n_tok))
        row0 += x2.shape[0]
    return outs


def _conv_state8(st):
    return jnp.pad(st, ((0, 0), (0, 0), (SUBLANES - st.shape[2], 0), (0, 0)))


def kernel(x_prompt, x_sample, cache_k, cache_v, cache_mem_k, cache_mem_v, state_ssd_conv, state_ssd, state_lru_conv, state_lru, page_table, mem_prompt, norm_mix, w_in, attn_lambda_q1, attn_lambda_k1, attn_lambda_q2, attn_lambda_k2, attn_subln, ssd_conv_w, ssd_conv_b, ssd_dt_bias, ssd_a_log, ssd_d, ssd_norm, lru_conv_w, lru_conv_b, lru_wa, lru_ba, lru_wx, lru_bx, lru_lambda, w_out, norm_mem, wq_mem, wk_mem, wv_mem, wo_mem, norm_ffn, router_group_w, router_group_b, router_expert_w, router_expert_b, moe_w_gate, moe_w_up, moe_w_down, final_norm):
    depth = w_in.shape[0]
    bp, tp, d = x_prompt.shape
    bs, ts, _ = x_sample.shape
    n_pool, page = cache_k.shape[1], cache_k.shape[2]
    n_pages = page_table.shape[1]
    past_len = n_pages * page
    n_mem = mem_prompt.shape[1]
    ssd_cc = state_ssd_conv.shape[-1]
    lru_w = state_lru.shape[-1]
    kvw = ATTN_KV_HEADS * ATTN_VD
    memw = MEM_HEADS * MEM_DH

    cos_p, sin_p = _rope_tables(jnp.tile(jnp.arange(tp, dtype=I32), bp))
    cos_s, sin_s = _rope_tables(jnp.tile(past_len + jnp.arange(ts, dtype=I32), bs))

    prompt = dict(bsz=bp, seq=tp, paged=False, cos=cos_p, sin=sin_p,
                  ssd_conv=jnp.zeros((depth, bp, SUBLANES, ssd_cc), F32),
                  ssd_state=jnp.zeros((depth, bp, SSD_HEADS * SSD_HEAD_DIM, SSD_STATE), F32),
                  lru_conv=jnp.zeros((depth, bp, SUBLANES, lru_w), F32),
                  lru_state=jnp.zeros((depth, bp, 1, lru_w), F32))
    sample = dict(bsz=bs, seq=ts, paged=True, cos=cos_s, sin=sin_s,
                  cache_k=cache_k.reshape(depth, n_pool, page * ATTN_KV_HEADS, ATTN_VD),
                  cache_v=cache_v.reshape(depth, n_pool, page * ATTN_KV_HEADS, ATTN_VD),
                  page_flat=page_table.reshape(-1), n_pool=n_pool, n_pages=n_pages,
                  ssd_conv=_conv_state8(state_ssd_conv),
                  ssd_state=state_ssd.reshape(depth, bs, SSD_HEADS * SSD_HEAD_DIM, SSD_STATE),
                  lru_conv=_conv_state8(state_lru_conv),
                  lru_state=state_lru.reshape(depth, bs, 1, lru_w),
                  mem_k=cache_mem_k.reshape(depth, bs, n_mem * MEM_HEADS, MEM_DH),
                  mem_v=cache_mem_v.reshape(depth, bs, n_mem * MEM_HEADS, MEM_DH))

    xp = x_prompt.reshape(bp * tp, d)
    xs = x_sample.reshape(bs * ts, d)
    po, so, mks, mvs = [], [], [], []
    for l in range(depth):
        w = w_in[l]
        zc = V0 + kvw
        xc0 = zc + SSD_HEADS * SSD_HEAD_DIM
        dc = xc0 + ssd_cc
        rc = dc + SSD_HEADS
        w_pad = jnp.concatenate([w[:, :zc], w[:, xc0:dc], w[:, zc:xc0], w[:, rc:rc + 2 * lru_w], w[:, dc:rc],
                                 jnp.zeros((d, UW - DT0 - SSD_HEADS), F32)], axis=1).astype(BF16)
        router_w = jnp.pad(jnp.concatenate([router_group_w[l], router_expert_w[l]], axis=1),
                           ((0, 0), (0, LANES - MOE_GROUPS - MOE_EXPERTS)))
        router_b = _pad_lanes(jnp.concatenate([router_group_b[l], router_expert_b[l]]), LANES)
        lw = dict(
            norm_mix=norm_mix[l].reshape(1, d), w_in=w_pad,
            lams=[v[l].reshape(1, ATTN_DH) for v in (attn_lambda_q1, attn_lambda_k1, attn_lambda_q2, attn_lambda_k2)],
            subln=attn_subln[l].reshape(1, ATTN_VD),
            ssd_conv_w=ssd_conv_w[l], ssd_conv_b=ssd_conv_b[l].reshape(1, -1),
            ssd_dt_bias=_pad_lanes(ssd_dt_bias[l], LANES), ssd_a_log=_pad_lanes(ssd_a_log[l], LANES),
            ssd_d=jnp.repeat(ssd_d[l], SSD_HEAD_DIM).reshape(1, -1), ssd_norm=ssd_norm[l].reshape(1, -1),
            lru_conv_w=lru_conv_w[l], lru_conv_b=lru_conv_b[l].reshape(1, -1),
            lru_wa=_block_diag(lru_wa[l]).astype(BF16), lru_ba=lru_ba[l].reshape(1, -1),
            lru_wx=_block_diag(lru_wx[l]).astype(BF16), lru_bx=lru_bx[l].reshape(1, -1),
            lru_lambda=lru_lambda[l].reshape(1, -1),
            w_out=w_out[l].astype(BF16), norm_mem=norm_mem[l].reshape(1, d), wq_mem=wq_mem[l].astype(BF16),
            wo_mem=wo_mem[l].astype(BF16), norm_ffn=norm_ffn[l].reshape(1, d),
            router_w=router_w, router_b=router_b,
            moe_w_gate=moe_w_gate, moe_w_up=moe_w_up, moe_w_down=moe_w_down,
        )
        lam_init = 0.8 - 0.6 * math.exp(-0.3 * l)
        mk, mv = _mem_kv(mem_prompt.reshape(bp * n_mem, d), wk_mem[l].astype(BF16), wv_mem[l].astype(BF16))
        mks.append(mk.reshape(bp, n_mem, MEM_HEADS, MEM_DH))
        mvs.append(mv.reshape(bp, n_mem, MEM_HEADS, MEM_DH))
        pg = dict(prompt, mem_k=mk.reshape(bp, n_mem, memw), mem_v=mv.reshape(bp, n_mem, memw))
        n_all = xp.shape[0] + xs.shape[0]
        routed_p, o = _layer_front(xp, pg, lw, l, lam_init, n_all, 0, None)
        po.append(o)
        routed_s, o = _layer_front(xs, sample, lw, l, lam_init, n_all, xp.shape[0], routed_p[1])
        so.append(o)
        xp, xs = _moe_groups([routed_p, routed_s], lw, l, l == depth - 1, final_norm.reshape(1, d))

    st = lambda outs, key: jnp.stack([o[key] for o in outs])
    return (xp.reshape(bp, tp, d), xs.reshape(bs, ts, d),
            st(po, "k"), st(po, "v"), jnp.stack(mks), jnp.stack(mvs),
            st(po, "ssd_conv"), st(po, "ssd_state"), st(po, "lru_conv"), st(po, "lru_state"),
            st(so, "k"), st(so, "v"), st(so, "ssd_conv"), st(so, "ssd_state"), st(so, "lru_conv"), st(so, "lru_state"))
```

```python
import functools
import math

import jax
import jax.numpy as jnp
from jax import lax
from jax.experimental import pallas as pl
from jax.experimental.pallas import tpu as pltpu

F32, BF16, I32 = jnp.float32, jnp.bfloat16, jnp.int32
EPS = 1e-6
LANES = 128
SUBLANES = 8
MIB = 1024 * 1024

ATTN_HEADS, ATTN_KV_HEADS, ATTN_DH = 8, 4, 64
ATTN_VD = 2 * ATTN_DH
ROPE_THETA = 10000.0
SSD_HEADS, SSD_HEAD_DIM, SSD_GROUPS, SSD_STATE, SSD_CONV = 8, 64, 2, 128, 4
SSD_CHUNK = 128
LRU_BLOCKS, LRU_CONV, LRU_C = 8, 4, 8.0
MEM_HEADS, MEM_DH = 4, 128
MOE_GROUPS, MOE_EXPERTS_PER_GROUP, MOE_TOPK = 4, 8, 2
MOE_EXPERTS = MOE_GROUPS * MOE_EXPERTS_PER_GROUP

Q0, K0, V0, XBC0, Z0, XR0, GATE0, DT0, UW = 0, 1024, 1536, 2048, 3072, 3584, 4096, 4608, 5120
PROJ_TN = 1024
PAGE_GROUP = 16


def _cparams(sem, vmem_mib):
    return pltpu.CompilerParams(dimension_semantics=sem, vmem_limit_bytes=vmem_mib * MIB)


def _tile(n, prefs):
    for p in prefs:
        if n % p == 0:
            return p
    return n


def _rms(x, w):
    return (x * lax.rsqrt(jnp.mean(x * x, axis=-1, keepdims=True) + EPS)) * w


def _load_token_rows(ref, n, c, tok0=0):
    return jnp.concatenate([ref[pl.ds(tok0 * c + j, n, stride=c), :] for j in range(c)], axis=1)


def _store_token_rows(ref, x, c, tok0=0):
    for j in range(c):
        ref[pl.ds(tok0 * c + j, x.shape[0], stride=c), :] = x[:, j * LANES:(j + 1) * LANES]


def _nt(a, b):
    return lax.dot_general(a, b, (((1,), (1,)), ((), ())), preferred_element_type=F32)


def _mm(a, b):
    return jnp.dot(a, b, preferred_element_type=F32)


def _rope128(yc, cos, sin):
    lane = lax.broadcasted_iota(I32, yc.shape, 1)
    sw = jnp.where((lane % 64) < 32, pltpu.roll(yc, 96, 1), pltpu.roll(yc, 32, 1))
    return yc * cos + sw * sin


def _in_proj_body(x_ref, nw_ref, w_ref, cos_ref, sin_ref, o_ref):
    hb = _rms(x_ref[...], nw_ref[...]).astype(BF16)
    cos, sin = cos_ref[...], sin_ref[...]
    rope_end = V0
    for c0 in range(0, UW, PROJ_TN):
        y = _mm(hb, w_ref[:, c0:c0 + PROJ_TN])
        for c in range(c0, c0 + PROJ_TN, LANES):
            yc = y[:, c - c0:c - c0 + LANES]
            o_ref[:, c:c + LANES] = _rope128(yc, cos, sin) if c < rope_end else yc


def _in_proj(x, nw, w_pad, cos, sin):
    t, d = x.shape
    tm = _tile(t, (256, 128))
    return pl.pallas_call(
        _in_proj_body,
        grid=(t // tm,),
        in_specs=[pl.BlockSpec((tm, d), lambda i: (i, 0)),
                  pl.BlockSpec((1, d), lambda i: (0, 0)),
                  pl.BlockSpec((d, UW), lambda i: (0, 0), pipeline_mode=pl.Buffered(1)),
                  pl.BlockSpec((tm, LANES), lambda i: (i, 0)),
                  pl.BlockSpec((tm, LANES), lambda i: (i, 0))],
        out_specs=pl.BlockSpec((tm, UW), lambda i: (i, 0)),
        out_shape=jax.ShapeDtypeStruct((t, UW), F32),
        compiler_params=_cparams(("arbitrary",), 52),
        name="in_proj",
    )(x, nw, w_pad, cos, sin)


def _lambda(lq1, lk1, lq2, lk2, lam_init):
    s1 = jnp.sum(lq1[...] * lk1[...], axis=-1, keepdims=True)
    s2 = jnp.sum(lq2[...] * lk2[...], axis=-1, keepdims=True)
    return jnp.exp(s1) - jnp.exp(s2) + lam_init


def _attn_prompt_body(lq1, lk1, lq2, lk2, sub_ref, q_ref, k_ref, v_ref, o_ref, ko_ref, vo_ref, acc_sc,
                      *, tq, tk, lam_init):
    i = pl.program_id(2)
    lam = _lambda(lq1, lk1, lq2, lk2, lam_init)

    @pl.when(i == 0)
    def _():
        h = pl.program_id(1)
        rows = pl.ds(h, k_ref.shape[0], stride=ATTN_KV_HEADS)
        ko_ref[rows, :] = k_ref[...]
        vo_ref[rows, :] = v_ref[...]

    q = q_ref[...] * (ATTN_DH ** -0.5 * math.log2(math.e))
    lane = lax.broadcasted_iota(I32, (tq, LANES), 1)
    n4 = 4 * tq
    parts = []
    for g in range(2):
        qh = q[:, g * LANES:(g + 1) * LANES]
        parts.append(jnp.where(lane < ATTN_DH, qh, 0.0))
        parts.append(jnp.where(lane >= ATTN_DH, qh, 0.0))
    q4 = jnp.concatenate(parts, axis=0).astype(BF16)
    acc_sc[...] = jnp.zeros_like(acc_sc)

    def chunk(c, stats, masked):
        m_old, l_old = stats
        start = pl.multiple_of(c * tk, tk)
        k = k_ref[pl.ds(start, tk), :].astype(BF16)
        v = v_ref[pl.ds(start, tk), :].astype(BF16)
        st = _nt(k, q4)
        if masked:
            key = start + lax.broadcasted_iota(I32, (tk, n4), 0)
            t = i * tq + jnp.bitwise_and(lax.broadcasted_iota(I32, (tk, n4), 1), tq - 1)
            st = jnp.where(key <= t, st, -jnp.inf)
        m_new = jnp.maximum(m_old, jnp.max(st, axis=0, keepdims=True))
        alpha = jnp.exp2(m_old - m_new)
        p = jnp.exp2(st - m_new)
        pv = lax.dot_general(v, p.astype(BF16), (((0,), (0,)), ((), ())), preferred_element_type=F32)
        acc_sc[...] = alpha * acc_sc[...] + pv
        return m_new, alpha * l_old + jnp.sum(p, axis=0, keepdims=True)

    last = (i * tq) // tk
    stats = (jnp.full((1, n4), -jnp.inf, F32), jnp.zeros((1, n4), F32))
    stats = lax.fori_loop(0, last, lambda c, s: chunk(c, s, False), stats)
    _, l = chunk(last, stats, True)
    ot = acc_sc[...] / l
    for g in range(2):
        og = ot[:, (2 * g) * tq:(2 * g + 1) * tq] - lam * ot[:, (2 * g + 1) * tq:(2 * g + 2) * tq]
        ms = jnp.mean(og * og, axis=0, keepdims=True)
        nrm = (og * lax.rsqrt(ms + EPS)) * sub_ref[...] * (1.0 - lam_init)
        o_ref[:, g * LANES:(g + 1) * LANES] = nrm.T


def _attn_prompt(u, lams, subln, bsz, seq, lam_init):
    tq = _tile(seq, (512, 256, 128))
    tk = tq
    nq = seq // tq
    vec = pl.BlockSpec((1, ATTN_DH), lambda b, h, i: (0, 0))
    return pl.pallas_call(
        functools.partial(_attn_prompt_body, tq=tq, tk=tk, lam_init=lam_init),
        grid=(bsz, ATTN_KV_HEADS, nq),
        in_specs=[vec, vec, vec, vec,
                  pl.BlockSpec((ATTN_VD, 1), lambda b, h, i: (0, 0)),
                  pl.BlockSpec((tq, 2 * LANES), lambda b, h, i: (b * nq + i, h)),
                  pl.BlockSpec((seq, LANES), lambda b, h, i: (b, K0 // LANES + h)),
                  pl.BlockSpec((seq, LANES), lambda b, h, i: (b, V0 // LANES + h))],
        out_specs=[pl.BlockSpec((tq, 2 * LANES), lambda b, h, i: (b * nq + i, h)),
                   pl.BlockSpec((seq * ATTN_KV_HEADS, ATTN_VD), lambda b, h, i: (b, 0)),
                   pl.BlockSpec((seq * ATTN_KV_HEADS, ATTN_VD), lambda b, h, i: (b, 0))],
        out_shape=[jax.ShapeDtypeStruct((bsz * seq, ATTN_HEADS * ATTN_VD), F32),
                   jax.ShapeDtypeStruct((bsz * seq * ATTN_KV_HEADS, ATTN_VD), F32),
                   jax.ShapeDtypeStruct((bsz * seq * ATTN_KV_HEADS, ATTN_VD), F32)],
        scratch_shapes=[pltpu.VMEM((ATTN_VD, 4 * tq), F32)],
        compiler_params=_cparams(("arbitrary", "arbitrary", "arbitrary"), 56),
        name="attn_prompt",
    )(*lams, subln.reshape(ATTN_VD, 1), u, u, u)


def _attn_sample_body(pt_ref, lq1, lk1, lq2, lk2, sub_ref, q_ref, kn_ref, vn_ref, *rest, n_pg, lam_init, tdec):
    del pt_ref
    k_refs, v_refs = rest[:n_pg], rest[n_pg:2 * n_pg]
    o_ref, m_sc, l_sc, acc_sc, qr_sc = rest[2 * n_pg:]
    j = pl.program_id(1)
    nrow = ATTN_KV_HEADS * 2 * 2 * tdec
    page_tok = k_refs[0].shape[0] // ATTN_KV_HEADS

    @pl.when(j == 0)
    def _():
        q = q_ref[...] * (ATTN_DH ** -0.5)
        lane = lax.broadcasted_iota(I32, (tdec, LANES), 1)
        qr_sc[...] = jnp.zeros_like(qr_sc)
        for kv in range(ATTN_KV_HEADS):
            for c in range(2):
                for g in range(2):
                    hd = kv * 2 + g
                    qh = q[:, hd * LANES:(hd + 1) * LANES]
                    n0 = ((kv * 2 + c) * 2 + g) * tdec
                    qr_sc[n0:n0 + tdec, kv * LANES:(kv + 1) * LANES] = jnp.where(
                        (lane >= c * ATTN_DH) & (lane < (c + 1) * ATTN_DH), qh, 0.0)
        m_sc[...] = jnp.full_like(m_sc, -jnp.inf)
        l_sc[...] = jnp.zeros_like(l_sc)
        acc_sc[...] = jnp.zeros_like(acc_sc)

    qr = qr_sc[...].astype(BF16)

    def page(ref):
        return jnp.concatenate([ref[pl.ds(kv, page_tok, stride=ATTN_KV_HEADS), :] for kv in range(ATTN_KV_HEADS)],
                               axis=1).astype(BF16)

    def update(s_list, v_list):
        m_old = m_sc[...]
        m_new = m_old
        for s in s_list:
            m_new = jnp.maximum(m_new, jnp.max(s, axis=1, keepdims=True))
        alpha = jnp.exp(m_old - m_new)
        l = alpha * l_sc[...]
        pv = None
        for s, v in zip(s_list, v_list):
            p = jnp.exp(s - m_new)
            l = l + jnp.sum(p, axis=1, keepdims=True)
            d = _mm(p.astype(BF16), v)
            pv = d if pv is None else pv + d
        m_sc[...] = m_new
        l_sc[...] = l
        acc_sc[...] = alpha * acc_sc[...] + pv

    for g0 in range(0, n_pg, PAGE_GROUP):
        grp = range(g0, min(g0 + PAGE_GROUP, n_pg))
        update([_nt(qr, page(k_refs[i])) for i in grp], [page(v_refs[i]) for i in grp])

    @pl.when(j == pl.num_programs(1) - 1)
    def _():
        lam = _lambda(lq1, lk1, lq2, lk2, lam_init)
        pad = jnp.zeros((LANES - tdec, kn_ref.shape[1]), F32)
        kn = jnp.concatenate([kn_ref[...], pad], axis=0).astype(BF16)
        vn = jnp.concatenate([vn_ref[...], pad], axis=0).astype(BF16)
        s = _nt(qr, kn)
        row = jnp.bitwise_and(lax.broadcasted_iota(I32, (nrow, LANES), 0), tdec - 1)
        col = lax.broadcasted_iota(I32, (nrow, LANES), 1)
        update([jnp.where(col <= row, s, -jnp.inf)], [vn])
        o = acc_sc[...] / l_sc[...]
        for kv in range(ATTN_KV_HEADS):
            for g in range(2):
                n0 = ((kv * 2 + 0) * 2 + g) * tdec
                n1 = ((kv * 2 + 1) * 2 + g) * tdec
                og = (o[n0:n0 + tdec, kv * LANES:(kv + 1) * LANES]
                      - lam * o[n1:n1 + tdec, kv * LANES:(kv + 1) * LANES])
                hd = kv * 2 + g
                o_ref[:, hd * LANES:(hd + 1) * LANES] = _rms(og, sub_ref[...]) * (1.0 - lam_init)


def _attn_sample(u, cache_k, cache_v, page_flat, layer, n_pool, lams, subln, bsz, tdec, n_pages, lam_init):
    del n_pool
    n_pg = _tile(n_pages, (32, 16, 8, 4, 2, 1))
    nchunks = n_pages // n_pg
    prow = cache_k.shape[2]
    width = ATTN_KV_HEADS * LANES
    nrow = ATTN_KV_HEADS * 2 * 2 * tdec
    assert nrow == LANES and tdec == SUBLANES
    vec = pl.BlockSpec((1, ATTN_DH), lambda b, j, pt: (0, 0))

    def page_spec(i):
        return pl.BlockSpec((None, None, prow, ATTN_VD),
                            lambda b, j, pt: (layer, pt[b * n_pages + j * n_pg + i], 0, 0))

    in_specs = [vec, vec, vec, vec,
                pl.BlockSpec((1, ATTN_VD), lambda b, j, pt: (0, 0)),
                pl.BlockSpec((tdec, ATTN_HEADS * ATTN_VD), lambda b, j, pt: (b, 0)),
                pl.BlockSpec((tdec, width), lambda b, j, pt: (b, K0 // width)),
                pl.BlockSpec((tdec, width), lambda b, j, pt: (b, V0 // width))]
    in_specs += [page_spec(i) for i in range(n_pg)] * 2
    return pl.pallas_call(
        functools.partial(_attn_sample_body, n_pg=n_pg, lam_init=lam_init, tdec=tdec),
        grid_spec=pltpu.PrefetchScalarGridSpec(
            num_scalar_prefetch=1,
            grid=(bsz, nchunks),
            in_specs=in_specs,
            out_specs=pl.BlockSpec((tdec, ATTN_HEADS * ATTN_VD), lambda b, j, pt: (b, 0)),
            scratch_shapes=[pltpu.VMEM((nrow, 1), F32), pltpu.VMEM((nrow, 1), F32),
                            pltpu.VMEM((nrow, width), F32), pltpu.VMEM((nrow, width), F32)]),
        out_shape=jax.ShapeDtypeStruct((bsz * tdec, ATTN_HEADS * ATTN_VD), F32),
        compiler_params=_cparams(("arbitrary", "arbitrary"), 56),
        name="attn_sample",
    )(page_flat, *lams, subln, u, u, u, *([cache_k] * n_pg), *([cache_v] * n_pg))


def _causal_conv(x, prev, cw_ref, cb_ref):
    taps = cw_ref.shape[0]
    rowi = lax.broadcasted_iota(I32, x.shape, 0)
    acc = cb_ref[...] + x * cw_ref[taps - 1:taps, :]
    for s in range(1, taps):
        sh = jnp.where(rowi < s, pltpu.roll(prev, s, 0), pltpu.roll(x, s, 0))
        acc = acc + sh * cw_ref[taps - 1 - s:taps - s, :]
    return acc


def _pad_rows(x, rows):
    if x.shape[0] == rows:
        return x
    return jnp.concatenate([x, jnp.zeros((rows - x.shape[0], x.shape[1]), x.dtype)], axis=0)


def _ssd_body(xbc_ref, z_ref, dt_ref, st_ref, h0_ref, cw_ref, cb_ref, dtb_ref, alog_ref, dvec_ref, nw_ref,
              y_ref, hout_ref, prev_sc, h_sc, *, rows):
    c = pl.program_id(1)
    L = SSD_CHUNK
    width = SSD_HEADS * SSD_HEAD_DIM
    gw = SSD_GROUPS * SSD_STATE

    @pl.when(c == 0)
    def _():
        prev_sc[...] = jnp.concatenate([jnp.zeros((L - SUBLANES, prev_sc.shape[1]), F32), st_ref[...]], axis=0)
        h_sc[...] = h0_ref[...]

    x = _pad_rows(xbc_ref[...], L)
    conv = _causal_conv(x, prev_sc[...], cw_ref, cb_ref)
    prev_sc[...] = x
    xc = conv * jax.nn.sigmoid(conv)
    xs, bm, cm = xc[:, :width], xc[:, width:width + gw], xc[:, width + gw:]

    dt = jax.nn.softplus(_pad_rows(dt_ref[...], L) + dtb_ref[...])
    if rows < L:
        dt = jnp.where(lax.broadcasted_iota(I32, dt.shape, 0) < rows, dt, 0.0)
    da = dt * (-jnp.exp(alog_ref[...]))
    r0 = lax.broadcasted_iota(I32, (L, L), 0)
    c0 = lax.broadcasted_iota(I32, (L, L), 1)
    causal = r0 >= c0
    acum = jnp.dot(causal.astype(F32), da, precision=lax.Precision.HIGHEST, preferred_element_type=F32)
    acum_t, dt_t = acum.T, dt.T
    last = acum[L - 1:L, :]
    wend = jnp.exp(last - acum) * dt
    eac = jnp.exp(acum)
    elast = jnp.exp(last)
    lane = lax.broadcasted_iota(I32, (L, LANES), 1)
    first = lane < SSD_HEAD_DIM
    top = lax.broadcasted_iota(I32, (2 * SSD_HEAD_DIM, SSD_STATE), 0) < SSD_HEAD_DIM

    ys = []
    for pr in range(SSD_HEADS // 2):
        g = (2 * pr) // (SSD_HEADS // SSD_GROUPS)
        bg = bm[:, g * SSD_STATE:(g + 1) * SSD_STATE].astype(BF16)
        cg = cm[:, g * SSD_STATE:(g + 1) * SSD_STATE].astype(BF16)
        gmat = _nt(cg, bg)
        xp = xs[:, pr * LANES:(pr + 1) * LANES]
        xpb = xp.astype(BF16)
        hp = h_sc[pr * LANES:(pr + 1) * LANES, :]
        outs = []
        for hh in (2 * pr, 2 * pr + 1):
            seg = acum[:, hh:hh + 1] - acum_t[hh:hh + 1, :]
            dec = jnp.exp(jnp.where(causal, seg, -jnp.inf))
            w = gmat * dec * dt_t[hh:hh + 1, :]
            outs.append(_mm(w.astype(BF16), xpb))
        h_a, h_b = 2 * pr, 2 * pr + 1
        y_intra = jnp.where(first, outs[0], outs[1])
        e_pair = jnp.where(first, eac[:, h_a:h_a + 1], eac[:, h_b:h_b + 1])
        ys.append(y_intra + _nt(cg, hp.astype(BF16)) * e_pair)
        w_pair = jnp.where(first, wend[:, h_a:h_a + 1], wend[:, h_b:h_b + 1])
        upd = _mm((xp * w_pair).T.astype(BF16), bg)
        keep = jnp.where(top, elast[:, h_a:h_a + 1], elast[:, h_b:h_b + 1])
        h_sc[pr * LANES:(pr + 1) * LANES, :] = hp * keep + upd

    y = jnp.concatenate(ys, axis=1) + dvec_ref[...] * xs
    zz = _pad_rows(z_ref[...], L)
    gated = y * (zz * jax.nn.sigmoid(zz))
    y_ref[...] = _rms(gated, nw_ref[...])[:rows]

    @pl.when(c == pl.num_programs(1) - 1)
    def _():
        hout_ref[...] = h_sc[...]


def _ssd(u, st8, h0, layer, cw, cb, dtb, alog, dvec, nw, bsz, seq):
    rows = SSD_CHUNK if seq % SSD_CHUNK == 0 else seq
    assert rows == SSD_CHUNK or (rows == seq and rows % SUBLANES == 0 and rows <= SSD_CHUNK)
    nch = seq // rows
    cc = cw.shape[1]
    width = SSD_HEADS * SSD_HEAD_DIM
    hp = SSD_HEADS * SSD_HEAD_DIM
    const = lambda shape: pl.BlockSpec(shape, lambda b, c: (0,) * len(shape))
    return pl.pallas_call(
        functools.partial(_ssd_body, rows=rows),
        grid=(bsz, nch),
        in_specs=[pl.BlockSpec((rows, cc), lambda b, c: (b * nch + c, XBC0 // cc)),
                  pl.BlockSpec((rows, width), lambda b, c: (b * nch + c, Z0 // width)),
                  pl.BlockSpec((rows, LANES), lambda b, c: (b * nch + c, DT0 // LANES)),
                  pl.BlockSpec((None, SUBLANES, cc), lambda b, c: (b, 0, 0)),
                  pl.BlockSpec((None, None, hp, SSD_STATE), lambda b, c: (layer, b, 0, 0)),
                  const((SSD_CONV, cc)), const((1, cc)), const((1, LANES)), const((1, LANES)),
                  const((1, width)), const((1, width))],
        out_specs=[pl.BlockSpec((rows, width), lambda b, c: (b * nch + c, 0)),
                   pl.BlockSpec((None, hp, SSD_STATE), lambda b, c: (b, 0, 0))],
        out_shape=[jax.ShapeDtypeStruct((bsz * seq, width), F32),
                   jax.ShapeDtypeStruct((bsz, hp, SSD_STATE), F32)],
        scratch_shapes=[pltpu.VMEM((SSD_CHUNK, cc), F32), pltpu.VMEM((hp, SSD_STATE), F32)],
        compiler_params=_cparams(("arbitrary", "arbitrary"), 32),
        name="ssd",
    )(u, u, u, st8, h0, cw, cb, dtb, alog, dvec, nw)


def _expm1(t):
    u = jnp.exp(t)
    small = (u - 1.0) * t / jnp.log(u)
    return jnp.where(t < -1.0, u - 1.0, jnp.where(u == 1.0, t, small))


def _lru_body(xr_ref, gate_ref, st_ref, h0_ref, cw_ref, cb_ref, wa_ref, ba_ref, wx_ref, bx_ref, lam_ref,
              y_ref, hout_ref, prev_sc, h_sc, *, rows):
    c = pl.program_id(1)

    @pl.when(c == 0)
    def _():
        st = st_ref[...]
        if rows > SUBLANES:
            st = jnp.concatenate([jnp.zeros((rows - SUBLANES, st.shape[1]), F32), st], axis=0)
        prev_sc[...] = st
        h_sc[...] = h0_ref[...]

    x = xr_ref[...]
    xc = _causal_conv(x, prev_sc[...], cw_ref, cb_ref)
    prev_sc[...] = x
    xcb = xc.astype(BF16)
    r = jax.nn.sigmoid(_mm(xcb, wa_ref[...]) + ba_ref[...])
    ig = jax.nn.sigmoid(_mm(xcb, wx_ref[...]) + bx_ref[...])
    log_a = (-LRU_C) * r * jax.nn.softplus(-lam_ref[...])
    a = jnp.exp(log_a)
    b = jnp.sqrt(-_expm1(2.0 * log_a)) * (ig * xc)
    rowi = lax.broadcasted_iota(I32, a.shape, 0)
    d = 1
    while d < rows:
        ok = rowi >= d
        b = jnp.where(ok, a * pltpu.roll(b, d, 0) + b, b)
        a = jnp.where(ok, a * pltpu.roll(a, d, 0), a)
        d *= 2
    h = b + a * h_sc[...]
    h_sc[...] = h[rows - 1:rows, :]
    y_ref[...] = h * jax.nn.gelu(gate_ref[...])

    @pl.when(c == pl.num_programs(1) - 1)
    def _():
        hout_ref[...] = h[rows - 1:rows, :]


def _lru(u, st8, h0, cw, cb, wa, ba, wx, bx, lam, bsz, seq):
    rows = _tile(seq, (128, 64, 32, 16, 8))
    nch = seq // rows
    w = cw.shape[1]
    const = lambda shape: pl.BlockSpec(shape, lambda b, c: (0,) * len(shape))
    return pl.pallas_call(
        functools.partial(_lru_body, rows=rows),
        grid=(bsz, nch),
        in_specs=[pl.BlockSpec((rows, w), lambda b, c: (b * nch + c, XR0 // w)),
                  pl.BlockSpec((rows, w), lambda b, c: (b * nch + c, GATE0 // w)),
                  pl.BlockSpec((None, SUBLANES, w), lambda b, c: (b, 0, 0)),
                  pl.BlockSpec((None, 1, w), lambda b, c: (b, 0, 0)),
                  const((LRU_CONV, w)), const((1, w)), const((w, w)), const((1, w)), const((w, w)), const((1, w)),
                  const((1, w))],
        out_specs=[pl.BlockSpec((rows, w), lambda b, c: (b * nch + c, 0)),
                   pl.BlockSpec((None, 1, w), lambda b, c: (b, 0, 0))],
        out_shape=[jax.ShapeDtypeStruct((bsz * seq, w), F32), jax.ShapeDtypeStruct((bsz, 1, w), F32)],
        scratch_shapes=[pltpu.VMEM((rows, w), F32), pltpu.VMEM((1, w), F32)],
        compiler_params=_cparams(("arbitrary", "arbitrary"), 32),
        name="lru",
    )(u, u, st8, h0, cw, cb, wa, ba, wx, bx, lam)


def _out_proj_body(x_ref, a_ref, s_ref, l_ref, wo_ref, nw_ref, wq_ref, x1_ref, qm_ref):
    wa, ws = a_ref.shape[1], s_ref.shape[1]
    acc = x_ref[...] + _mm(a_ref[...].astype(BF16), wo_ref[0:wa, :])
    acc = acc + _mm(s_ref[...].astype(BF16), wo_ref[wa:wa + ws, :])
    acc = acc + _mm(l_ref[...].astype(BF16), wo_ref[wa + ws:, :])
    x1_ref[...] = acc
    qm_ref[...] = _mm(_rms(acc, nw_ref[...]).astype(BF16), wq_ref[...])


def _out_proj(x, attn, ssd, lru, wo, nw, wq):
    t, d = x.shape
    tm = _tile(t, (256, 128))
    row = lambda w: pl.BlockSpec((tm, w), lambda i: (i, 0))
    const = lambda shape: pl.BlockSpec(shape, lambda i: (0, 0))
    return pl.pallas_call(
        _out_proj_body,
        grid=(t // tm,),
        in_specs=[row(d), row(attn.shape[1]), row(ssd.shape[1]), row(lru.shape[1]),
                  const(wo.shape), const((1, d)), const(wq.shape)],
        out_specs=[row(d), row(wq.shape[1])],
        out_shape=[jax.ShapeDtypeStruct((t, d), F32), jax.ShapeDtypeStruct((t, wq.shape[1]), F32)],
        compiler_params=_cparams(("arbitrary",), 48),
        name="out_proj",
    )(x, attn, ssd, lru, wo, nw, wq)


def _mem_kv_body(m_ref, wk_ref, wv_ref, k_ref, v_ref):
    mb = m_ref[...].astype(BF16)
    k_ref[...] = _mm(mb, wk_ref[...])
    v_ref[...] = _mm(mb, wv_ref[...])


def _mem_kv(mem, wk, wv):
    t, d = mem.shape
    tm = _tile(t, (256, 128))
    w = wk.shape[1]
    return pl.pallas_call(
        _mem_kv_body,
        grid=(t // tm,),
        in_specs=[pl.BlockSpec((tm, d), lambda i: (i, 0)),
                  pl.BlockSpec((d, w), lambda i: (0, 0)), pl.BlockSpec((d, w), lambda i: (0, 0))],
        out_specs=[pl.BlockSpec((tm, w), lambda i: (i, 0)), pl.BlockSpec((tm, w), lambda i: (i, 0))],
        out_shape=[jax.ShapeDtypeStruct((t, w), F32), jax.ShapeDtypeStruct((t, w), F32)],
        compiler_params=_cparams(("arbitrary",), 32),
        name="mem_kv",
    )(mem, wk, wv)


def _xattn_body(q_ref, k_ref, v_ref, o_ref, *, head_rows):
    for h in range(MEM_HEADS):
        sl = slice(h * MEM_DH, (h + 1) * MEM_DH)
        if head_rows:
            rows = pl.ds(h, k_ref.shape[0] // MEM_HEADS, stride=MEM_HEADS)
            k, v = k_ref[rows, :], v_ref[rows, :]
        else:
            k, v = k_ref[:, sl], v_ref[:, sl]
        s = _nt(q_ref[:, sl].astype(BF16), k.astype(BF16)) * (MEM_DH ** -0.5)
        e = jnp.exp(s - jnp.max(s, axis=1, keepdims=True))
        o = _mm(e.astype(BF16), v.astype(BF16))
        o_ref[:, sl] = o / jnp.sum(e, axis=1, keepdims=True)


def _xattn(qm, mem_k, mem_v, bsz, seq, layer=None):
    tq = _tile(seq, (256, 128))
    nq = seq // tq
    w = qm.shape[1]
    if layer is None:
        mem_spec = pl.BlockSpec((None,) + mem_k.shape[1:], lambda b, i: (b, 0, 0))
    else:
        mem_spec = pl.BlockSpec((None, None) + mem_k.shape[2:], lambda b, i: (layer, b, 0, 0))
    return pl.pallas_call(
        functools.partial(_xattn_body, head_rows=layer is not None),
        grid=(bsz, nq),
        in_specs=[pl.BlockSpec((tq, w), lambda b, i: (b * nq + i, 0)), mem_spec, mem_spec],
        out_specs=pl.BlockSpec((tq, w), lambda b, i: (b * nq + i, 0)),
        out_shape=jax.ShapeDtypeStruct(qm.shape, F32),
        compiler_params=_cparams(("arbitrary", "arbitrary"), 32),
        name="xattn",
    )(qm, mem_k, mem_v)


def _route_body(*refs, n_own):
    hf_ref = refs[-3]
    i = pl.program_id(0)

    @pl.when(i < n_own)
    def _():
        _route_tile(*refs)

    @pl.when(i >= n_own)
    def _():
        hf_ref[...] = jnp.zeros_like(hf_ref)


def _route_tile(x1_ref, om_ref, wo_ref, nw_ref, rw_ref, rb_ref, *rest):
    x2_ref, hf_ref, eid_ref, gate_ref = rest[-4:]
    x2 = x1_ref[...] + _mm(om_ref[...].astype(BF16), wo_ref[...])
    x2_ref[...] = x2
    hf = _rms(x2, nw_ref[...])
    _store_token_rows(hf_ref, hf, hf.shape[1] // LANES)
    logits = jnp.dot(hf, rw_ref[...], precision=lax.Precision.HIGHEST, preferred_element_type=F32) + rb_ref[...]
    lane = lax.broadcasted_iota(I32, logits.shape, 1).astype(F32)
    ninf = -jnp.inf
    big = float(LANES)

    def first_argmax(v, mx):
        return jnp.min(jnp.where(v == mx, lane, big), axis=1, keepdims=True)

    lg = jnp.where(lane < MOE_GROUPS, logits, ninf)
    mg = jnp.max(lg, axis=1, keepdims=True)
    p_grp = 1.0 / jnp.sum(jnp.exp(lg - mg), axis=1, keepdims=True)
    lo = MOE_GROUPS + MOE_EXPERTS_PER_GROUP * first_argmax(lg, mg)
    le = jnp.where((lane >= lo) & (lane < lo + MOE_EXPERTS_PER_GROUP), logits, ninf)
    v1 = jnp.max(le, axis=1, keepdims=True)
    i1 = first_argmax(le, v1)
    le2 = jnp.where(lane == i1, ninf, le)
    v2 = jnp.max(le2, axis=1, keepdims=True)
    i2 = first_argmax(le2, v2)
    e2 = jnp.exp(v2 - v1)
    den = 1.0 + e2
    g1 = p_grp * (1.0 / den)
    g2 = p_grp * (e2 / den)
    gate_ref[...] = jnp.where(lane == 0.0, g1, jnp.where(lane == 1.0, g2, 0.0))
    eid_ref[...] = jnp.where(lane == 0.0, i1 - MOE_GROUPS, jnp.where(lane == 1.0, i2 - MOE_GROUPS, 0.0)).astype(I32)


def _route(x1, om, wo, nw, rw, rb, hf_rows, hf_row0, hf_shared):
    t, d = x1.shape
    creating = hf_shared is None
    tail = hf_rows - hf_row0 - t if creating else 0
    tm = _tile(math.gcd(t, hf_row0, tail), (256, 128, 64, 32, 16, 8))
    blk0 = hf_row0 // tm
    n_own = t // tm
    n_steps = n_own + tail // tm
    row = lambda w: pl.BlockSpec((tm, w), lambda i: (jnp.minimum(i, n_own - 1), 0))
    const = lambda shape: pl.BlockSpec(shape, lambda i: (0, 0))
    in_specs = [row(d), row(om.shape[1]), const(wo.shape), const((1, d)), const(rw.shape), const((1, LANES))]
    args = [x1, om, wo, nw, rw, rb]
    aliases = {}
    if hf_shared is not None:
        in_specs.append(pl.BlockSpec(memory_space=pl.ANY))
        args.append(hf_shared)
        aliases = {len(args) - 1: 1}
    return pl.pallas_call(
        functools.partial(_route_body, n_own=n_own),
        grid=(n_steps,),
        in_specs=in_specs,
        out_specs=[row(d), pl.BlockSpec((tm * (d // LANES), LANES), lambda i: (blk0 + i, 0)), row(LANES), row(LANES)],
        out_shape=[jax.ShapeDtypeStruct((t, d), F32), jax.ShapeDtypeStruct((hf_rows * (d // LANES), LANES), F32),
                   jax.ShapeDtypeStruct((t, LANES), I32), jax.ShapeDtypeStruct((t, LANES), F32)],
        input_output_aliases=aliases,
        compiler_params=_cparams(("arbitrary",), 40),
        name="route",
    )(*args)


ROW_UNROLL = 8


def _moe_body(grow_ref, srow_ref, be_ref, base_ref, cnt_ref, hf_hbm, wg_ref, wu_ref, wd_ref, o_hbm,
              xbuf, ybuf, wgb, wub, wdb, gsem, ssem, *, rb):
    g = pl.program_id(0)
    ng = pl.num_programs(0)
    slot = lax.rem(g, 2)
    cnt = cnt_ref[g]

    def for_rows(n, fn):
        n_grp = lax.shift_right_logical(n, ROW_UNROLL.bit_length() - 1)

        def grp(q, c):
            for u in range(ROW_UNROLL):
                fn(q * ROW_UNROLL + u)
            return c

        def one(i, c):
            fn(i)
            return c

        lax.fori_loop(0, n_grp, grp, 0)
        lax.fori_loop(n_grp * ROW_UNROLL, n, one, 0)

    tr = xbuf.shape[0] // (2 * rb)

    def tok(ref, i, n=1):
        return ref.at[pl.ds(i * tr, n * tr), :]

    def gather_row(sl, i, src):
        return pltpu.make_async_copy(tok(hf_hbm, src), tok(xbuf, sl * rb + i), gsem.at[sl])

    def scatter_row(sl, i, dst):
        return pltpu.make_async_copy(tok(ybuf, sl * rb + i), tok(o_hbm, dst), ssem.at[sl])

    def start_gathers(blk, sl):
        base = base_ref[blk]
        for_rows(cnt_ref[blk], lambda i: gather_row(sl, i, grow_ref[base + i]).start())

    def start_scatters(blk, sl):
        base = base_ref[blk]
        for_rows(cnt_ref[blk], lambda i: scatter_row(sl, i, srow_ref[base + i]).start())

    def wait_rows(blk, sl, row_copy, block_copy):
        n = cnt_ref[blk]

        @pl.when(n == rb)
        def _():
            block_copy.wait()

        @pl.when(n < rb)
        def _():
            for_rows(n, lambda i: row_copy(sl, i, 0).wait())

    def wait_gathers(blk, sl):
        wait_rows(blk, sl, gather_row,
                  pltpu.make_async_copy(tok(hf_hbm, 0, rb), tok(xbuf, sl * rb, rb), gsem.at[sl]))

    def wait_scatters(blk, sl):
        wait_rows(blk, sl, scatter_row,
                  pltpu.make_async_copy(tok(ybuf, sl * rb, rb), tok(o_hbm, 0, rb), ssem.at[sl]))

    @pl.when(g == 0)
    def _():
        xbuf[...] = jnp.zeros_like(xbuf)
        start_gathers(0, 0)

    prev_e = be_ref[jnp.maximum(g - 1, 0)]

    @pl.when((cnt > 0) & ((g == 0) | (prev_e != be_ref[g])))
    def _():
        wgb[...] = wg_ref[...].astype(BF16)
        wub[...] = wu_ref[...].astype(BF16)
        wdb[...] = wd_ref[...].astype(BF16)

    wait_gathers(g, slot)

    @pl.when(g + 1 < ng)
    def _():
        start_gathers(g + 1, 1 - slot)

    @pl.when(g >= 2)
    def _():
        wait_scatters(g - 2, slot)

    @pl.when(cnt > 0)
    def _():
        xb = _load_token_rows(xbuf, rb, tr, slot * rb).astype(BF16)
        hg = _mm(xb, wgb[...])
        act = (hg * jax.nn.sigmoid(hg)) * _mm(xb, wub[...])
        _store_token_rows(ybuf, _mm(act.astype(BF16), wdb[...]), tr, slot * rb)
        start_scatters(g, slot)

    @pl.when(g == ng - 1)
    def _():
        @pl.when(g >= 1)
        def _():
            wait_scatters(g - 1, 1 - slot)

        wait_scatters(g, slot)


def _moe_ffn(hf, grow, srow, blk_e, blk_base, blk_cnt, wg, wu, wd, layer, rb):
    d, ff = wg.shape[-2:]
    tr = d // LANES
    n_asg = grow.shape[0]
    n_blk = blk_e.shape[0]
    wspec = lambda shape: pl.BlockSpec((None, None) + shape, lambda g, gr, sr, be, bb, bc: (layer, be[g], 0, 0))
    return pl.pallas_call(
        functools.partial(_moe_body, rb=rb),
        grid_spec=pltpu.PrefetchScalarGridSpec(
            num_scalar_prefetch=5,
            grid=(n_blk,),
            in_specs=[pl.BlockSpec(memory_space=pl.ANY), wspec((d, ff)), wspec((d, ff)), wspec((ff, d))],
            out_specs=pl.BlockSpec(memory_space=pl.ANY),
            scratch_shapes=[pltpu.VMEM((2 * rb * tr, LANES), F32), pltpu.VMEM((2 * rb * tr, LANES), F32),
                            pltpu.VMEM((d, ff), BF16), pltpu.VMEM((d, ff), BF16), pltpu.VMEM((ff, d), BF16),
                            pltpu.SemaphoreType.DMA((2,)), pltpu.SemaphoreType.DMA((2,))]),
        out_shape=jax.ShapeDtypeStruct((n_asg * tr, LANES), F32),
        compiler_params=_cparams(("arbitrary",), 56),
        name="moe_ffn",
    )(grow, srow, blk_e, blk_base, blk_cnt, hf, wg, wu, wd)


def _moe_plan(eid, n_tok, rb):
    n_asg = n_tok * MOE_TOPK
    flat_e = eid.reshape(-1)
    order = jnp.argsort(flat_e, stable=True).astype(I32)
    grow = order // MOE_TOPK
    srow = (order % MOE_TOPK) * n_tok + grow
    counts = jnp.sum((flat_e[:, None] == jnp.arange(MOE_EXPERTS, dtype=I32)[None, :]).astype(I32), axis=0)
    starts = jnp.cumsum(counts) - counts
    nblk = (counts + rb - 1) // rb
    blk_end = jnp.cumsum(nblk)
    n_blk = -(-n_asg // rb) + MOE_EXPERTS
    gi = jnp.arange(n_blk, dtype=I32)
    be = jnp.minimum(jnp.sum((gi[:, None] >= blk_end[None, :]).astype(I32), axis=1), MOE_EXPERTS - 1)
    r = gi - (blk_end - nblk)[be]
    cnt = jnp.clip(counts[be] - r * rb, 0, rb).astype(I32)
    base = jnp.minimum(starts[be] + r * rb, n_asg - 1).astype(I32)
    return grow.astype(I32), srow.astype(I32), be, base, cnt


def _combine_body(x2_ref, o0_ref, o1_ref, g_ref, fw_ref, x3_ref, *, final):
    g = g_ref[...]
    n, c = x2_ref.shape[0], x2_ref.shape[1] // LANES
    x3 = x2_ref[...] + (g[:, 0:1] * _load_token_rows(o0_ref, n, c) + g[:, 1:2] * _load_token_rows(o1_ref, n, c))
    x3_ref[...] = _rms(x3, fw_ref[...]) if final else x3


def _combine(x2, o2, gates, fw, final, row0, n_all):
    t, d = x2.shape
    tm = _tile(math.gcd(t, row0, n_all), (256, 128, 64, 32, 16, 8))
    assert t % tm == 0 and row0 % tm == 0 and n_all % tm == 0
    nt = t // tm
    b0, b1 = row0 // tm, (n_all + row0) // tm
    return pl.pallas_call(
        functools.partial(_combine_body, final=final),
        grid=(nt,),
        in_specs=[pl.BlockSpec((tm, d), lambda i: (i, 0)),
                  pl.BlockSpec((tm * (d // LANES), LANES), lambda i: (b0 + i, 0)),
                  pl.BlockSpec((tm * (d // LANES), LANES), lambda i: (b1 + i, 0)),
                  pl.BlockSpec((tm, LANES), lambda i: (i, 0)),
                  pl.BlockSpec((1, d), lambda i: (0, 0))],
        out_specs=pl.BlockSpec((tm, d), lambda i: (i, 0)),
        out_shape=jax.ShapeDtypeStruct((t, d), F32),
        compiler_params=_cparams(("arbitrary",), 32),
        name="combine",
    )(x2, o2, o2, gates, fw)


def _rope_tables(pos):
    half = ATTN_DH // 2
    inv = ROPE_THETA ** (-jnp.arange(half, dtype=F32) / half)
    ang = pos.astype(F32)[:, None] * inv[None, :]
    cos, sin = jnp.cos(ang), jnp.sin(ang)
    reps = LANES // ATTN_DH
    return jnp.tile(jnp.concatenate([cos, cos], axis=1), (1, reps)), jnp.tile(jnp.concatenate([-sin, sin], axis=1), (1, reps))


def _block_diag(w):
    g, a, b = w.shape
    eye = jnp.eye(g, dtype=w.dtype)
    return (w[:, :, None, :] * eye[:, None, :, None]).reshape(g * a, g * b)


def _pad_lanes(v, n):
    return jnp.pad(v.reshape(1, -1), ((0, 0), (0, n - v.size)))


def _layer_front(x, grp, lw, layer, lam_init, hf_rows, hf_row0, hf_shared):
    bsz, seq = grp["bsz"], grp["seq"]
    u = _in_proj(x, lw["norm_mix"], lw["w_in"], grp["cos"], grp["sin"])
    lams = lw["lams"]
    if grp["paged"]:
        attn = _attn_sample(u, grp["cache_k"], grp["cache_v"], grp["page_flat"], layer, grp["n_pool"], lams,
                            lw["subln"], bsz, seq, grp["n_pages"], lam_init)
        k_new = v_new = None
    else:
        attn, k_new, v_new = _attn_prompt(u, lams, lw["subln"], bsz, seq, lam_init)
    ssd, h_ssd = _ssd(u, grp["ssd_conv"][layer], grp["ssd_state"], layer, lw["ssd_conv_w"], lw["ssd_conv_b"],
                      lw["ssd_dt_bias"], lw["ssd_a_log"], lw["ssd_d"], lw["ssd_norm"], bsz, seq)
    lru, h_lru = _lru(u, grp["lru_conv"][layer], grp["lru_state"][layer], lw["lru_conv_w"], lw["lru_conv_b"],
                      lw["lru_wa"], lw["lru_ba"], lw["lru_wx"], lw["lru_bx"], lw["lru_lambda"], bsz, seq)
    x1, qm = _out_proj(x, attn, ssd, lru, lw["w_out"], lw["norm_mem"], lw["wq_mem"])
    if grp["paged"]:
        om = _xattn(qm, grp["mem_k"], grp["mem_v"], bsz, seq, layer)
    else:
        om = _xattn(qm, grp["mem_k"], grp["mem_v"], bsz, seq)
    routed = _route(x1, om, lw["wo_mem"], lw["norm_ffn"], lw["router_w"], lw["router_b"],
                    hf_rows, hf_row0, hf_shared)
    ur = u.reshape(bsz, seq, UW)
    width_kv = ATTN_KV_HEADS * ATTN_VD
    kv_shape = (bsz, seq, ATTN_KV_HEADS, ATTN_VD)
    outs = dict(
        k=(ur[:, :, K0:K0 + width_kv] if k_new is None else k_new).reshape(kv_shape),
        v=(ur[:, :, V0:V0 + width_kv] if v_new is None else v_new).reshape(kv_shape),
        ssd_conv=ur[:, seq - (SSD_CONV - 1):, XBC0:Z0],
        ssd_state=h_ssd.reshape(bsz, SSD_HEADS, SSD_HEAD_DIM, SSD_STATE),
        lru_conv=ur[:, seq - (LRU_CONV - 1):, XR0:GATE0],
        lru_state=h_lru.reshape(bsz, -1),
    )
    return routed, outs


def _moe_groups(routed, lw, layer, final, final_norm):
    hf = routed[-1][1]
    eid = jnp.concatenate([r[2][:, :MOE_TOPK] for r in routed], axis=0)
    n_tok = eid.shape[0]
    rb = 256 if n_tok * MOE_TOPK >= 256 * MOE_EXPERTS else 128
    plan = _moe_plan(eid, n_tok, rb)
    o2 = _moe_ffn(hf, *plan, lw["moe_w_gate"], lw["moe_w_up"], lw["moe_w_down"], layer, rb)
    outs, row0 = [], 0
    for x2, _, _, gate in routed:
        outs.append(_combine(x2, o2, gate, final_norm, final, row0, n_tok))
        row0 += x2.shape[0]
    return outs


def _conv_state8(st):
    return jnp.pad(st, ((0, 0), (0, 0), (SUBLANES - st.shape[2], 0), (0, 0)))


def kernel(x_prompt, x_sample, cache_k, cache_v, cache_mem_k, cache_mem_v, state_ssd_conv, state_ssd, state_lru_conv, state_lru, page_table, mem_prompt, norm_mix, w_in, attn_lambda_q1, attn_lambda_k1, attn_lambda_q2, attn_lambda_k2, attn_subln, ssd_conv_w, ssd_conv_b, ssd_dt_bias, ssd_a_log, ssd_d, ssd_norm, lru_conv_w, lru_conv_b, lru_wa, lru_ba, lru_wx, lru_bx, lru_lambda, w_out, norm_mem, wq_mem, wk_mem, wv_mem, wo_mem, norm_ffn, router_group_w, router_group_b, router_expert_w, router_expert_b, moe_w_gate, moe_w_up, moe_w_down, final_norm):
    depth = w_in.shape[0]
    bp, tp, d = x_prompt.shape
    bs, ts, _ = x_sample.shape
    n_pool, page = cache_k.shape[1], cache_k.shape[2]
    n_pages = page_table.shape[1]
    past_len = n_pages * page
    n_mem = mem_prompt.shape[1]
    ssd_cc = state_ssd_conv.shape[-1]
    lru_w = state_lru.shape[-1]
    kvw = ATTN_KV_HEADS * ATTN_VD
    memw = MEM_HEADS * MEM_DH

    cos_p, sin_p = _rope_tables(jnp.tile(jnp.arange(tp, dtype=I32), bp))
    cos_s, sin_s = _rope_tables(jnp.tile(past_len + jnp.arange(ts, dtype=I32), bs))

    prompt = dict(bsz=bp, seq=tp, paged=False, cos=cos_p, sin=sin_p,
                  ssd_conv=jnp.zeros((depth, bp, SUBLANES, ssd_cc), F32),
                  ssd_state=jnp.zeros((depth, bp, SSD_HEADS * SSD_HEAD_DIM, SSD_STATE), F32),
                  lru_conv=jnp.zeros((depth, bp, SUBLANES, lru_w), F32),
                  lru_state=jnp.zeros((depth, bp, 1, lru_w), F32))
    sample = dict(bsz=bs, seq=ts, paged=True, cos=cos_s, sin=sin_s,
                  cache_k=cache_k.reshape(depth, n_pool, page * ATTN_KV_HEADS, ATTN_VD),
                  cache_v=cache_v.reshape(depth, n_pool, page * ATTN_KV_HEADS, ATTN_VD),
                  page_flat=page_table.reshape(-1), n_pool=n_pool, n_pages=n_pages,
                  ssd_conv=_conv_state8(state_ssd_conv),
                  ssd_state=state_ssd.reshape(depth, bs, SSD_HEADS * SSD_HEAD_DIM, SSD_STATE),
                  lru_conv=_conv_state8(state_lru_conv),
                  lru_state=state_lru.reshape(depth, bs, 1, lru_w),
                  mem_k=cache_mem_k.reshape(depth, bs, n_mem * MEM_HEADS, MEM_DH),
                  mem_v=cache_mem_v.reshape(depth, bs, n_mem * MEM_HEADS, MEM_DH))

    xp = x_prompt.reshape(bp * tp, d)
    xs = x_sample.reshape(bs * ts, d)
    po, so, mks, mvs = [], [], [], []
    for l in range(depth):
        w = w_in[l]
        zc = V0 + kvw
        xc0 = zc + SSD_HEADS * SSD_HEAD_DIM
        dc = xc0 + ssd_cc
        rc = dc + SSD_HEADS
        w_pad = jnp.concatenate([w[:, :zc], w[:, xc0:dc], w[:, zc:xc0], w[:, rc:rc + 2 * lru_w], w[:, dc:rc],
                                 jnp.zeros((d, UW - DT0 - SSD_HEADS), F32)], axis=1).astype(BF16)
        router_w = jnp.pad(jnp.concatenate([router_group_w[l], router_expert_w[l]], axis=1),
                           ((0, 0), (0, LANES - MOE_GROUPS - MOE_EXPERTS)))
        router_b = _pad_lanes(jnp.concatenate([router_group_b[l], router_expert_b[l]]), LANES)
        lw = dict(
            norm_mix=norm_mix[l].reshape(1, d), w_in=w_pad,
            lams=[v[l].reshape(1, ATTN_DH) for v in (attn_lambda_q1, attn_lambda_k1, attn_lambda_q2, attn_lambda_k2)],
            subln=attn_subln[l].reshape(1, ATTN_VD),
            ssd_conv_w=ssd_conv_w[l], ssd_conv_b=ssd_conv_b[l].reshape(1, -1),
            ssd_dt_bias=_pad_lanes(ssd_dt_bias[l], LANES), ssd_a_log=_pad_lanes(ssd_a_log[l], LANES),
            ssd_d=jnp.repeat(ssd_d[l], SSD_HEAD_DIM).reshape(1, -1), ssd_norm=ssd_norm[l].reshape(1, -1),
            lru_conv_w=lru_conv_w[l], lru_conv_b=lru_conv_b[l].reshape(1, -1),
            lru_wa=_block_diag(lru_wa[l]).astype(BF16), lru_ba=lru_ba[l].reshape(1, -1),
            lru_wx=_block_diag(lru_wx[l]).astype(BF16), lru_bx=lru_bx[l].reshape(1, -1),
            lru_lambda=lru_lambda[l].reshape(1, -1),
            w_out=w_out[l].astype(BF16), norm_mem=norm_mem[l].reshape(1, d), wq_mem=wq_mem[l].astype(BF16),
            wo_mem=wo_mem[l].astype(BF16), norm_ffn=norm_ffn[l].reshape(1, d),
            router_w=router_w, router_b=router_b,
            moe_w_gate=moe_w_gate, moe_w_up=moe_w_up, moe_w_down=moe_w_down,
        )
        lam_init = 0.8 - 0.6 * math.exp(-0.3 * l)
        mk, mv = _mem_kv(mem_prompt.reshape(bp * n_mem, d), wk_mem[l].astype(BF16), wv_mem[l].astype(BF16))
        mks.append(mk.reshape(bp, n_mem, MEM_HEADS, MEM_DH))
        mvs.append(mv.reshape(bp, n_mem, MEM_HEADS, MEM_DH))
        pg = dict(prompt, mem_k=mk.reshape(bp, n_mem, memw), mem_v=mv.reshape(bp, n_mem, memw))
        n_all = xp.shape[0] + xs.shape[0]
        routed_p, o = _layer_front(xp, pg, lw, l, lam_init, n_all, 0, None)
        po.append(o)
        routed_s, o = _layer_front(xs, sample, lw, l, lam_init, n_all, xp.shape[0], routed_p[1])
        so.append(o)
        xp, xs = _moe_groups([routed_p, routed_s], lw, l, l == depth - 1, final_norm.reshape(1, d))

    st = lambda outs, key: jnp.stack([o[key] for o in outs])
    return (xp.reshape(bp, tp, d), xs.reshape(bs, ts, d),
            st(po, "k"), st(po, "v"), jnp.stack(mks), jnp.stack(mvs),
            st(po, "ssd_conv"), st(po, "ssd_state"), st(po, "lru_conv"), st(po, "lru_state"),
            st(so, "k"), st(so, "v"), st(so, "ssd_conv"), st(so, "ssd_state"), st(so, "lru_conv"), st(so, "lru_state"))
```

```python
import functools
import math

import jax
import jax.numpy as jnp
from jax import lax
from jax.experimental import pallas as pl
from jax.experimental.pallas import tpu as pltpu

F32, BF16, I32 = jnp.float32, jnp.bfloat16, jnp.int32
EPS = 1e-6
LANES = 128
SUBLANES = 8
MIB = 1024 * 1024

ATTN_HEADS, ATTN_KV_HEADS, ATTN_DH = 8, 4, 64
ATTN_VD = 2 * ATTN_DH
ROPE_THETA = 10000.0
SSD_HEADS, SSD_HEAD_DIM, SSD_GROUPS, SSD_STATE, SSD_CONV = 8, 64, 2, 128, 4
SSD_CHUNK = 128
LRU_BLOCKS, LRU_CONV, LRU_C = 8, 4, 8.0
MEM_HEADS, MEM_DH = 4, 128
MOE_GROUPS, MOE_EXPERTS_PER_GROUP, MOE_TOPK = 4, 8, 2
MOE_EXPERTS = MOE_GROUPS * MOE_EXPERTS_PER_GROUP

Q0, K0, V0, XBC0, Z0, XR0, GATE0, DT0, UW = 0, 1024, 1536, 2048, 3072, 3584, 4096, 4608, 5120
PROJ_TN = 1024
PAGE_GROUP = 16


def _cparams(sem, vmem_mib):
    return pltpu.CompilerParams(dimension_semantics=sem, vmem_limit_bytes=vmem_mib * MIB)


def _tile(n, prefs):
    for p in prefs:
        if n % p == 0:
            return p
    return n


def _rms(x, w):
    return (x * lax.rsqrt(jnp.mean(x * x, axis=-1, keepdims=True) + EPS)) * w


def _load_token_rows(ref, n, c, tok0=0):
    return jnp.concatenate([ref[pl.ds(tok0 * c + j, n, stride=c), :] for j in range(c)], axis=1)


def _store_token_rows(ref, x, c, tok0=0):
    for j in range(c):
        ref[pl.ds(tok0 * c + j, x.shape[0], stride=c), :] = x[:, j * LANES:(j + 1) * LANES]


def _nt(a, b):
    return lax.dot_general(a, b, (((1,), (1,)), ((), ())), preferred_element_type=F32)


def _mm(a, b):
    return jnp.dot(a, b, preferred_element_type=F32)


def _rope128(yc, cos, sin):
    lane = lax.broadcasted_iota(I32, yc.shape, 1)
    sw = jnp.where((lane % 64) < 32, pltpu.roll(yc, 96, 1), pltpu.roll(yc, 32, 1))
    return yc * cos + sw * sin


def _in_proj_body(x_ref, nw_ref, w_ref, cos_ref, sin_ref, o_ref):
    hb = _rms(x_ref[...], nw_ref[...]).astype(BF16)
    cos, sin = cos_ref[...], sin_ref[...]
    rope_end = V0
    for c0 in range(0, UW, PROJ_TN):
        y = _mm(hb, w_ref[:, c0:c0 + PROJ_TN])
        for c in range(c0, c0 + PROJ_TN, LANES):
            yc = y[:, c - c0:c - c0 + LANES]
            o_ref[:, c:c + LANES] = _rope128(yc, cos, sin) if c < rope_end else yc


def _in_proj(x, nw, w_pad, cos, sin):
    t, d = x.shape
    tm = _tile(t, (256, 128))
    return pl.pallas_call(
        _in_proj_body,
        grid=(t // tm,),
        in_specs=[pl.BlockSpec((tm, d), lambda i: (i, 0)),
                  pl.BlockSpec((1, d), lambda i: (0, 0)),
                  pl.BlockSpec((d, UW), lambda i: (0, 0), pipeline_mode=pl.Buffered(1)),
                  pl.BlockSpec((tm, LANES), lambda i: (i, 0)),
                  pl.BlockSpec((tm, LANES), lambda i: (i, 0))],
        out_specs=pl.BlockSpec((tm, UW), lambda i: (i, 0)),
        out_shape=jax.ShapeDtypeStruct((t, UW), F32),
        compiler_params=_cparams(("arbitrary",), 52),
        name="in_proj",
    )(x, nw, w_pad, cos, sin)


def _lambda(lq1, lk1, lq2, lk2, lam_init):
    s1 = jnp.sum(lq1[...] * lk1[...], axis=-1, keepdims=True)
    s2 = jnp.sum(lq2[...] * lk2[...], axis=-1, keepdims=True)
    return jnp.exp(s1) - jnp.exp(s2) + lam_init


def _attn_prompt_body(lq1, lk1, lq2, lk2, sub_ref, q_ref, k_ref, v_ref, o_ref, ko_ref, vo_ref, acc_sc,
                      *, tq, tk, lam_init):
    i = pl.program_id(2)
    lam = _lambda(lq1, lk1, lq2, lk2, lam_init)

    @pl.when(i == 0)
    def _():
        h = pl.program_id(1)
        rows = pl.ds(h, k_ref.shape[0], stride=ATTN_KV_HEADS)
        ko_ref[rows, :] = k_ref[...]
        vo_ref[rows, :] = v_ref[...]

    q = q_ref[...] * (ATTN_DH ** -0.5 * math.log2(math.e))
    lane = lax.broadcasted_iota(I32, (tq, LANES), 1)
    n4 = 4 * tq
    parts = []
    for g in range(2):
        qh = q[:, g * LANES:(g + 1) * LANES]
        parts.append(jnp.where(lane < ATTN_DH, qh, 0.0))
        parts.append(jnp.where(lane >= ATTN_DH, qh, 0.0))
    q4 = jnp.concatenate(parts, axis=0).astype(BF16)
    acc_sc[...] = jnp.zeros_like(acc_sc)

    def chunk(c, stats, masked):
        m_old, l_old = stats
        start = pl.multiple_of(c * tk, tk)
        k = k_ref[pl.ds(start, tk), :].astype(BF16)
        v = v_ref[pl.ds(start, tk), :].astype(BF16)
        st = _nt(k, q4)
        if masked:
            key = start + lax.broadcasted_iota(I32, (tk, n4), 0)
            t = i * tq + jnp.bitwise_and(lax.broadcasted_iota(I32, (tk, n4), 1), tq - 1)
            st = jnp.where(key <= t, st, -jnp.inf)
        m_new = jnp.maximum(m_old, jnp.max(st, axis=0, keepdims=True))
        alpha = jnp.exp2(m_old - m_new)
        p = jnp.exp2(st - m_new)
        pv = lax.dot_general(v, p.astype(BF16), (((0,), (0,)), ((), ())), preferred_element_type=F32)
        acc_sc[...] = alpha * acc_sc[...] + pv
        return m_new, alpha * l_old + jnp.sum(p, axis=0, keepdims=True)

    last = (i * tq) // tk
    stats = (jnp.full((1, n4), -jnp.inf, F32), jnp.zeros((1, n4), F32))
    stats = lax.fori_loop(0, last, lambda c, s: chunk(c, s, False), stats)
    _, l = chunk(last, stats, True)
    ot = acc_sc[...] / l
    for g in range(2):
        og = ot[:, (2 * g) * tq:(2 * g + 1) * tq] - lam * ot[:, (2 * g + 1) * tq:(2 * g + 2) * tq]
        ms = jnp.mean(og * og, axis=0, keepdims=True)
        nrm = (og * lax.rsqrt(ms + EPS)) * sub_ref[...] * (1.0 - lam_init)
        o_ref[:, g * LANES:(g + 1) * LANES] = nrm.T


def _attn_prompt(u, lams, subln, bsz, seq, lam_init):
    tq = _tile(seq, (512, 256, 128))
    tk = tq
    nq = seq // tq
    vec = pl.BlockSpec((1, ATTN_DH), lambda b, h, i: (0, 0))
    return pl.pallas_call(
        functools.partial(_attn_prompt_body, tq=tq, tk=tk, lam_init=lam_init),
        grid=(bsz, ATTN_KV_HEADS, nq),
        in_specs=[vec, vec, vec, vec,
                  pl.BlockSpec((ATTN_VD, 1), lambda b, h, i: (0, 0)),
                  pl.BlockSpec((tq, 2 * LANES), lambda b, h, i: (b * nq + i, h)),
                  pl.BlockSpec((seq, LANES), lambda b, h, i: (b, K0 // LANES + h)),
                  pl.BlockSpec((seq, LANES), lambda b, h, i: (b, V0 // LANES + h))],
        out_specs=[pl.BlockSpec((tq, 2 * LANES), lambda b, h, i: (b * nq + i, h)),
                   pl.BlockSpec((seq * ATTN_KV_HEADS, ATTN_VD), lambda b, h, i: (b, 0)),
                   pl.BlockSpec((seq * ATTN_KV_HEADS, ATTN_VD), lambda b, h, i: (b, 0))],
        out_shape=[jax.ShapeDtypeStruct((bsz * seq, ATTN_HEADS * ATTN_VD), F32),
                   jax.ShapeDtypeStruct((bsz * seq * ATTN_KV_HEADS, ATTN_VD), F32),
                   jax.ShapeDtypeStruct((bsz * seq * ATTN_KV_HEADS, ATTN_VD), F32)],
        scratch_shapes=[pltpu.VMEM((ATTN_VD, 4 * tq), F32)],
        compiler_params=_cparams(("arbitrary", "arbitrary", "arbitrary"), 56),
        name="attn_prompt",
    )(*lams, subln.reshape(ATTN_VD, 1), u, u, u)


def _attn_sample_body(pt_ref, lq1, lk1, lq2, lk2, sub_ref, q_ref, kn_ref, vn_ref, *rest, n_pg, lam_init, tdec):
    del pt_ref
    k_refs, v_refs = rest[:n_pg], rest[n_pg:2 * n_pg]
    o_ref, m_sc, l_sc, acc_sc, qr_sc = rest[2 * n_pg:]
    j = pl.program_id(1)
    nrow = ATTN_KV_HEADS * 2 * 2 * tdec
    page_tok = k_refs[0].shape[0] // ATTN_KV_HEADS

    @pl.when(j == 0)
    def _():
        q = q_ref[...] * (ATTN_DH ** -0.5)
        lane = lax.broadcasted_iota(I32, (tdec, LANES), 1)
        qr_sc[...] = jnp.zeros_like(qr_sc)
        for kv in range(ATTN_KV_HEADS):
            for c in range(2):
                for g in range(2):
                    hd = kv * 2 + g
                    qh = q[:, hd * LANES:(hd + 1) * LANES]
                    n0 = ((kv * 2 + c) * 2 + g) * tdec
                    qr_sc[n0:n0 + tdec, kv * LANES:(kv + 1) * LANES] = jnp.where(
                        (lane >= c * ATTN_DH) & (lane < (c + 1) * ATTN_DH), qh, 0.0)
        m_sc[...] = jnp.full_like(m_sc, -jnp.inf)
        l_sc[...] = jnp.zeros_like(l_sc)
        acc_sc[...] = jnp.zeros_like(acc_sc)

    qr = qr_sc[...].astype(BF16)

    def page(ref):
        return jnp.concatenate([ref[pl.ds(kv, page_tok, stride=ATTN_KV_HEADS), :] for kv in range(ATTN_KV_HEADS)],
                               axis=1).astype(BF16)

    def update(s_list, v_list):
        m_old = m_sc[...]
        m_new = m_old
        for s in s_list:
            m_new = jnp.maximum(m_new, jnp.max(s, axis=1, keepdims=True))
        alpha = jnp.exp(m_old - m_new)
        l = alpha * l_sc[...]
        pv = None
        for s, v in zip(s_list, v_list):
            p = jnp.exp(s - m_new)
            l = l + jnp.sum(p, axis=1, keepdims=True)
            d = _mm(p.astype(BF16), v)
            pv = d if pv is None else pv + d
        m_sc[...] = m_new
        l_sc[...] = l
        acc_sc[...] = alpha * acc_sc[...] + pv

    for g0 in range(0, n_pg, PAGE_GROUP):
        grp = range(g0, min(g0 + PAGE_GROUP, n_pg))
        update([_nt(qr, page(k_refs[i])) for i in grp], [page(v_refs[i]) for i in grp])

    @pl.when(j == pl.num_programs(1) - 1)
    def _():
        lam = _lambda(lq1, lk1, lq2, lk2, lam_init)
        pad = jnp.zeros((LANES - tdec, kn_ref.shape[1]), F32)
        kn = jnp.concatenate([kn_ref[...], pad], axis=0).astype(BF16)
        vn = jnp.concatenate([vn_ref[...], pad], axis=0).astype(BF16)
        s = _nt(qr, kn)
        row = jnp.bitwise_and(lax.broadcasted_iota(I32, (nrow, LANES), 0), tdec - 1)
        col = lax.broadcasted_iota(I32, (nrow, LANES), 1)
        update([jnp.where(col <= row, s, -jnp.inf)], [vn])
        o = acc_sc[...] / l_sc[...]
        for kv in range(ATTN_KV_HEADS):
            for g in range(2):
                n0 = ((kv * 2 + 0) * 2 + g) * tdec
                n1 = ((kv * 2 + 1) * 2 + g) * tdec
                og = (o[n0:n0 + tdec, kv * LANES:(kv + 1) * LANES]
                      - lam * o[n1:n1 + tdec, kv * LANES:(kv + 1) * LANES])
                hd = kv * 2 + g
                o_ref[:, hd * LANES:(hd + 1) * LANES] = _rms(og, sub_ref[...]) * (1.0 - lam_init)


def _attn_sample(u, cache_k, cache_v, page_flat, layer, n_pool, lams, subln, bsz, tdec, n_pages, lam_init):
    del n_pool
    n_pg = _tile(n_pages, (32, 16, 8, 4, 2, 1))
    nchunks = n_pages // n_pg
    prow = cache_k.shape[2]
    width = ATTN_KV_HEADS * LANES
    nrow = ATTN_KV_HEADS * 2 * 2 * tdec
    assert nrow == LANES and tdec == SUBLANES
    vec = pl.BlockSpec((1, ATTN_DH), lambda b, j, pt: (0, 0))

    def page_spec(i):
        return pl.BlockSpec((None, None, prow, ATTN_VD),
                            lambda b, j, pt: (layer, pt[b * n_pages + j * n_pg + i], 0, 0))

    in_specs = [vec, vec, vec, vec,
                pl.BlockSpec((1, ATTN_VD), lambda b, j, pt: (0, 0)),
                pl.BlockSpec((tdec, ATTN_HEADS * ATTN_VD), lambda b, j, pt: (b, 0)),
                pl.BlockSpec((tdec, width), lambda b, j, pt: (b, K0 // width)),
                pl.BlockSpec((tdec, width), lambda b, j, pt: (b, V0 // width))]
    in_specs += [page_spec(i) for i in range(n_pg)] * 2
    return pl.pallas_call(
        functools.partial(_attn_sample_body, n_pg=n_pg, lam_init=lam_init, tdec=tdec),
        grid_spec=pltpu.PrefetchScalarGridSpec(
            num_scalar_prefetch=1,
            grid=(bsz, nchunks),
            in_specs=in_specs,
            out_specs=pl.BlockSpec((tdec, ATTN_HEADS * ATTN_VD), lambda b, j, pt: (b, 0)),
            scratch_shapes=[pltpu.VMEM((nrow, 1), F32), pltpu.VMEM((nrow, 1), F32),
                            pltpu.VMEM((nrow, width), F32), pltpu.VMEM((nrow, width), F32)]),
        out_shape=jax.ShapeDtypeStruct((bsz * tdec, ATTN_HEADS * ATTN_VD), F32),
        compiler_params=_cparams(("arbitrary", "arbitrary"), 56),
        name="attn_sample",
    )(page_flat, *lams, subln, u, u, u, *([cache_k] * n_pg), *([cache_v] * n_pg))


def _causal_conv(x, prev, cw_ref, cb_ref):
    taps = cw_ref.shape[0]
    rowi = lax.broadcasted_iota(I32, x.shape, 0)
    acc = cb_ref[...] + x * cw_ref[taps - 1:taps, :]
    for s in range(1, taps):
        sh = jnp.where(rowi < s, pltpu.roll(prev, s, 0), pltpu.roll(x, s, 0))
        acc = acc + sh * cw_ref[taps - 1 - s:taps - s, :]
    return acc


def _pad_rows(x, rows):
    if x.shape[0] == rows:
        return x
    return jnp.concatenate([x, jnp.zeros((rows - x.shape[0], x.shape[1]), x.dtype)], axis=0)


def _ssd_body(xbc_ref, z_ref, dt_ref, st_ref, h0_ref, cw_ref, cb_ref, dtb_ref, alog_ref, dvec_ref, nw_ref,
              y_ref, hout_ref, prev_sc, h_sc, *, rows):
    c = pl.program_id(1)
    L = SSD_CHUNK
    width = SSD_HEADS * SSD_HEAD_DIM
    gw = SSD_GROUPS * SSD_STATE

    @pl.when(c == 0)
    def _():
        prev_sc[...] = jnp.concatenate([jnp.zeros((L - SUBLANES, prev_sc.shape[1]), F32), st_ref[...]], axis=0)
        h_sc[...] = h0_ref[...]

    x = _pad_rows(xbc_ref[...], L)
    conv = _causal_conv(x, prev_sc[...], cw_ref, cb_ref)
    prev_sc[...] = x
    xc = conv * jax.nn.sigmoid(conv)
    xs, bm, cm = xc[:, :width], xc[:, width:width + gw], xc[:, width + gw:]

    dt = jax.nn.softplus(_pad_rows(dt_ref[...], L) + dtb_ref[...])
    if rows < L:
        dt = jnp.where(lax.broadcasted_iota(I32, dt.shape, 0) < rows, dt, 0.0)
    da = dt * (-jnp.exp(alog_ref[...]))
    r0 = lax.broadcasted_iota(I32, (L, L), 0)
    c0 = lax.broadcasted_iota(I32, (L, L), 1)
    causal = r0 >= c0
    acum = jnp.dot(causal.astype(F32), da, precision=lax.Precision.HIGHEST, preferred_element_type=F32)
    acum_t, dt_t = acum.T, dt.T
    last = acum[L - 1:L, :]
    wend = jnp.exp(last - acum) * dt
    eac = jnp.exp(acum)
    elast = jnp.exp(last)
    lane = lax.broadcasted_iota(I32, (L, LANES), 1)
    first = lane < SSD_HEAD_DIM
    top = lax.broadcasted_iota(I32, (2 * SSD_HEAD_DIM, SSD_STATE), 0) < SSD_HEAD_DIM

    ys = []
    for pr in range(SSD_HEADS // 2):
        g = (2 * pr) // (SSD_HEADS // SSD_GROUPS)
        bg = bm[:, g * SSD_STATE:(g + 1) * SSD_STATE].astype(BF16)
        cg = cm[:, g * SSD_STATE:(g + 1) * SSD_STATE].astype(BF16)
        gmat = _nt(cg, bg)
        xp = xs[:, pr * LANES:(pr + 1) * LANES]
        xpb = xp.astype(BF16)
        hp = h_sc[pr * LANES:(pr + 1) * LANES, :]
        outs = []
        for hh in (2 * pr, 2 * pr + 1):
            seg = acum[:, hh:hh + 1] - acum_t[hh:hh + 1, :]
            dec = jnp.exp(jnp.where(causal, seg, -jnp.inf))
            w = gmat * dec * dt_t[hh:hh + 1, :]
            outs.append(_mm(w.astype(BF16), xpb))
        h_a, h_b = 2 * pr, 2 * pr + 1
        y_intra = jnp.where(first, outs[0], outs[1])
        e_pair = jnp.where(first, eac[:, h_a:h_a + 1], eac[:, h_b:h_b + 1])
        ys.append(y_intra + _nt(cg, hp.astype(BF16)) * e_pair)
        w_pair = jnp.where(first, wend[:, h_a:h_a + 1], wend[:, h_b:h_b + 1])
        upd = _mm((xp * w_pair).T.astype(BF16), bg)
        keep = jnp.where(top, elast[:, h_a:h_a + 1], elast[:, h_b:h_b + 1])
        h_sc[pr * LANES:(pr + 1) * LANES, :] = hp * keep + upd

    y = jnp.concatenate(ys, axis=1) + dvec_ref[...] * xs
    zz = _pad_rows(z_ref[...], L)
    gated = y * (zz * jax.nn.sigmoid(zz))
    y_ref[...] = _rms(gated, nw_ref[...])[:rows]

    @pl.when(c == pl.num_programs(1) - 1)
    def _():
        hout_ref[...] = h_sc[...]


def _ssd(u, st8, h0, layer, cw, cb, dtb, alog, dvec, nw, bsz, seq):
    rows = SSD_CHUNK if seq % SSD_CHUNK == 0 else seq
    assert rows == SSD_CHUNK or (rows == seq and rows % SUBLANES == 0 and rows <= SSD_CHUNK)
    nch = seq // rows
    cc = cw.shape[1]
    width = SSD_HEADS * SSD_HEAD_DIM
    hp = SSD_HEADS * SSD_HEAD_DIM
    const = lambda shape: pl.BlockSpec(shape, lambda b, c: (0,) * len(shape))
    return pl.pallas_call(
        functools.partial(_ssd_body, rows=rows),
        grid=(bsz, nch),
        in_specs=[pl.BlockSpec((rows, cc), lambda b, c: (b * nch + c, XBC0 // cc)),
                  pl.BlockSpec((rows, width), lambda b, c: (b * nch + c, Z0 // width)),
                  pl.BlockSpec((rows, LANES), lambda b, c: (b * nch + c, DT0 // LANES)),
                  pl.BlockSpec((None, SUBLANES, cc), lambda b, c: (b, 0, 0)),
                  pl.BlockSpec((None, None, hp, SSD_STATE), lambda b, c: (layer, b, 0, 0)),
                  const((SSD_CONV, cc)), const((1, cc)), const((1, LANES)), const((1, LANES)),
                  const((1, width)), const((1, width))],
        out_specs=[pl.BlockSpec((rows, width), lambda b, c: (b * nch + c, 0)),
                   pl.BlockSpec((None, hp, SSD_STATE), lambda b, c: (b, 0, 0))],
        out_shape=[jax.ShapeDtypeStruct((bsz * seq, width), F32),
                   jax.ShapeDtypeStruct((bsz, hp, SSD_STATE), F32)],
        scratch_shapes=[pltpu.VMEM((SSD_CHUNK, cc), F32), pltpu.VMEM((hp, SSD_STATE), F32)],
        compiler_params=_cparams(("arbitrary", "arbitrary"), 32),
        name="ssd",
    )(u, u, u, st8, h0, cw, cb, dtb, alog, dvec, nw)


def _expm1(t):
    u = jnp.exp(t)
    small = (u - 1.0) * t / jnp.log(u)
    return jnp.where(t < -1.0, u - 1.0, jnp.where(u == 1.0, t, small))


def _lru_body(xr_ref, gate_ref, st_ref, h0_ref, cw_ref, cb_ref, wa_ref, ba_ref, wx_ref, bx_ref, lam_ref,
              y_ref, hout_ref, prev_sc, h_sc, *, rows):
    c = pl.program_id(1)

    @pl.when(c == 0)
    def _():
        st = st_ref[...]
        if rows > SUBLANES:
            st = jnp.concatenate([jnp.zeros((rows - SUBLANES, st.shape[1]), F32), st], axis=0)
        prev_sc[...] = st
        h_sc[...] = h0_ref[...]

    x = xr_ref[...]
    xc = _causal_conv(x, prev_sc[...], cw_ref, cb_ref)
    prev_sc[...] = x
    xcb = xc.astype(BF16)
    r = jax.nn.sigmoid(_mm(xcb, wa_ref[...]) + ba_ref[...])
    ig = jax.nn.sigmoid(_mm(xcb, wx_ref[...]) + bx_ref[...])
    log_a = (-LRU_C) * r * jax.nn.softplus(-lam_ref[...])
    a = jnp.exp(log_a)
    b = jnp.sqrt(-_expm1(2.0 * log_a)) * (ig * xc)
    rowi = lax.broadcasted_iota(I32, a.shape, 0)
    d = 1
    while d < rows:
        ok = rowi >= d
        b = jnp.where(ok, a * pltpu.roll(b, d, 0) + b, b)
        a = jnp.where(ok, a * pltpu.roll(a, d, 0), a)
        d *= 2
    h = b + a * h_sc[...]
    h_sc[...] = h[rows - 1:rows, :]
    y_ref[...] = h * jax.nn.gelu(gate_ref[...])

    @pl.when(c == pl.num_programs(1) - 1)
    def _():
        hout_ref[...] = h[rows - 1:rows, :]


def _lru(u, st8, h0, cw, cb, wa, ba, wx, bx, lam, bsz, seq):
    rows = _tile(seq, (128, 64, 32, 16, 8))
    nch = seq // rows
    w = cw.shape[1]
    const = lambda shape: pl.BlockSpec(shape, lambda b, c: (0,) * len(shape))
    return pl.pallas_call(
        functools.partial(_lru_body, rows=rows),
        grid=(bsz, nch),
        in_specs=[pl.BlockSpec((rows, w), lambda b, c: (b * nch + c, XR0 // w)),
                  pl.BlockSpec((rows, w), lambda b, c: (b * nch + c, GATE0 // w)),
                  pl.BlockSpec((None, SUBLANES, w), lambda b, c: (b, 0, 0)),
                  pl.BlockSpec((None, 1, w), lambda b, c: (b, 0, 0)),
                  const((LRU_CONV, w)), const((1, w)), const((w, w)), const((1, w)), const((w, w)), const((1, w)),
                  const((1, w))],
        out_specs=[pl.BlockSpec((rows, w), lambda b, c: (b * nch + c, 0)),
                   pl.BlockSpec((None, 1, w), lambda b, c: (b, 0, 0))],
        out_shape=[jax.ShapeDtypeStruct((bsz * seq, w), F32), jax.ShapeDtypeStruct((bsz, 1, w), F32)],
        scratch_shapes=[pltpu.VMEM((rows, w), F32), pltpu.VMEM((1, w), F32)],
        compiler_params=_cparams(("arbitrary", "arbitrary"), 32),
        name="lru",
    )(u, u, st8, h0, cw, cb, wa, ba, wx, bx, lam)


def _out_proj_body(x_ref, a_ref, s_ref, l_ref, wo_ref, nw_ref, wq_ref, x1_ref, qm_ref):
    wa, ws = a_ref.shape[1], s_ref.shape[1]
    acc = x_ref[...] + _mm(a_ref[...].astype(BF16), wo_ref[0:wa, :])
    acc = acc + _mm(s_ref[...].astype(BF16), wo_ref[wa:wa + ws, :])
    acc = acc + _mm(l_ref[...].astype(BF16), wo_ref[wa + ws:, :])
    x1_ref[...] = acc
    qm_ref[...] = _mm(_rms(acc, nw_ref[...]).astype(BF16), wq_ref[...])


def _out_proj(x, attn, ssd, lru, wo, nw, wq):
    t, d = x.shape
    tm = _tile(t, (256, 128))
    row = lambda w: pl.BlockSpec((tm, w), lambda i: (i, 0))
    const = lambda shape: pl.BlockSpec(shape, lambda i: (0, 0))
    return pl.pallas_call(
        _out_proj_body,
        grid=(t // tm,),
        in_specs=[row(d), row(attn.shape[1]), row(ssd.shape[1]), row(lru.shape[1]),
                  const(wo.shape), const((1, d)), const(wq.shape)],
        out_specs=[row(d), row(wq.shape[1])],
        out_shape=[jax.ShapeDtypeStruct((t, d), F32), jax.ShapeDtypeStruct((t, wq.shape[1]), F32)],
        compiler_params=_cparams(("arbitrary",), 48),
        name="out_proj",
    )(x, attn, ssd, lru, wo, nw, wq)


def _mem_kv_body(m_ref, wk_ref, wv_ref, k_ref, v_ref):
    mb = m_ref[...].astype(BF16)
    k_ref[...] = _mm(mb, wk_ref[...])
    v_ref[...] = _mm(mb, wv_ref[...])


def _mem_kv(mem, wk, wv):
    t, d = mem.shape
    tm = _tile(t, (256, 128))
    w = wk.shape[1]
    return pl.pallas_call(
        _mem_kv_body,
        grid=(t // tm,),
        in_specs=[pl.BlockSpec((tm, d), lambda i: (i, 0)),
                  pl.BlockSpec((d, w), lambda i: (0, 0)), pl.BlockSpec((d, w), lambda i: (0, 0))],
        out_specs=[pl.BlockSpec((tm, w), lambda i: (i, 0)), pl.BlockSpec((tm, w), lambda i: (i, 0))],
        out_shape=[jax.ShapeDtypeStruct((t, w), F32), jax.ShapeDtypeStruct((t, w), F32)],
        compiler_params=_cparams(("arbitrary",), 32),
        name="mem_kv",
    )(mem, wk, wv)


def _xattn_body(q_ref, k_ref, v_ref, o_ref, *, head_rows):
    for h in range(MEM_HEADS):
        sl = slice(h * MEM_DH, (h + 1) * MEM_DH)
        if head_rows:
            rows = pl.ds(h, k_ref.shape[0] // MEM_HEADS, stride=MEM_HEADS)
            k, v = k_ref[rows, :], v_ref[rows, :]
        else:
            k, v = k_ref[:, sl], v_ref[:, sl]
        s = _nt(q_ref[:, sl].astype(BF16), k.astype(BF16)) * (MEM_DH ** -0.5)
        e = jnp.exp(s - jnp.max(s, axis=1, keepdims=True))
        o = _mm(e.astype(BF16), v.astype(BF16))
        o_ref[:, sl] = o / jnp.sum(e, axis=1, keepdims=True)


def _xattn(qm, mem_k, mem_v, bsz, seq, layer=None):
    tq = _tile(seq, (256, 128))
    nq = seq // tq
    w = qm.shape[1]
    if layer is None:
        mem_spec = pl.BlockSpec((None,) + mem_k.shape[1:], lambda b, i: (b, 0, 0))
    else:
        mem_spec = pl.BlockSpec((None, None) + mem_k.shape[2:], lambda b, i: (layer, b, 0, 0))
    return pl.pallas_call(
        functools.partial(_xattn_body, head_rows=layer is not None),
        grid=(bsz, nq),
        in_specs=[pl.BlockSpec((tq, w), lambda b, i: (b * nq + i, 0)), mem_spec, mem_spec],
        out_specs=pl.BlockSpec((tq, w), lambda b, i: (b * nq + i, 0)),
        out_shape=jax.ShapeDtypeStruct(qm.shape, F32),
        compiler_params=_cparams(("arbitrary", "arbitrary"), 32),
        name="xattn",
    )(qm, mem_k, mem_v)


def _route_body(*refs, n_own):
    hf_ref = refs[-3]
    i = pl.program_id(0)

    @pl.when(i < n_own)
    def _():
        _route_tile(*refs)

    @pl.when(i >= n_own)
    def _():
        hf_ref[...] = jnp.zeros_like(hf_ref)


def _route_tile(x1_ref, om_ref, wo_ref, nw_ref, rw_ref, rb_ref, *rest):
    x2_ref, hf_ref, eid_ref, gate_ref = rest[-4:]
    x2 = x1_ref[...] + _mm(om_ref[...].astype(BF16), wo_ref[...])
    x2_ref[...] = x2
    hf = _rms(x2, nw_ref[...])
    _store_token_rows(hf_ref, hf, hf.shape[1] // LANES)
    rw = rw_ref[...]
    hf_hi, rw_hi = hf.astype(BF16), rw.astype(BF16)
    hf_lo, rw_lo = (hf - hf_hi.astype(F32)).astype(BF16), (rw - rw_hi.astype(F32)).astype(BF16)
    logits = (_mm(hf_hi, rw_hi) + (_mm(hf_hi, rw_lo) + _mm(hf_lo, rw_hi))) + rb_ref[...]
    lane = lax.broadcasted_iota(I32, logits.shape, 1).astype(F32)
    ninf = -jnp.inf
    big = float(LANES)

    def first_argmax(v, mx):
        return jnp.min(jnp.where(v == mx, lane, big), axis=1, keepdims=True)

    lg = jnp.where(lane < MOE_GROUPS, logits, ninf)
    mg = jnp.max(lg, axis=1, keepdims=True)
    p_grp = 1.0 / jnp.sum(jnp.exp(lg - mg), axis=1, keepdims=True)
    lo = MOE_GROUPS + MOE_EXPERTS_PER_GROUP * first_argmax(lg, mg)
    le = jnp.where((lane >= lo) & (lane < lo + MOE_EXPERTS_PER_GROUP), logits, ninf)
    v1 = jnp.max(le, axis=1, keepdims=True)
    i1 = first_argmax(le, v1)
    le2 = jnp.where(lane == i1, ninf, le)
    v2 = jnp.max(le2, axis=1, keepdims=True)
    i2 = first_argmax(le2, v2)
    e2 = jnp.exp(v2 - v1)
    den = 1.0 + e2
    g1 = p_grp * (1.0 / den)
    g2 = p_grp * (e2 / den)
    gate_ref[...] = jnp.where(lane == 0.0, g1, jnp.where(lane == 1.0, g2, 0.0))
    eid_ref[...] = jnp.where(lane == 0.0, i1 - MOE_GROUPS, jnp.where(lane == 1.0, i2 - MOE_GROUPS, 0.0)).astype(I32)


def _route(x1, om, wo, nw, rw, rb, hf_rows, hf_row0, hf_shared):
    t, d = x1.shape
    creating = hf_shared is None
    tail = hf_rows - hf_row0 - t if creating else 0
    tm = _tile(math.gcd(t, hf_row0, tail), (256, 128, 64, 32, 16, 8))
    blk0 = hf_row0 // tm
    n_own = t // tm
    n_steps = n_own + tail // tm
    row = lambda w: pl.BlockSpec((tm, w), lambda i: (jnp.minimum(i, n_own - 1), 0))
    const = lambda shape: pl.BlockSpec(shape, lambda i: (0, 0))
    in_specs = [row(d), row(om.shape[1]), const(wo.shape), const((1, d)), const(rw.shape), const((1, LANES))]
    args = [x1, om, wo, nw, rw, rb]
    aliases = {}
    if hf_shared is not None:
        in_specs.append(pl.BlockSpec(memory_space=pl.ANY))
        args.append(hf_shared)
        aliases = {len(args) - 1: 1}
    return pl.pallas_call(
        functools.partial(_route_body, n_own=n_own),
        grid=(n_steps,),
        in_specs=in_specs,
        out_specs=[row(d), pl.BlockSpec((tm * (d // LANES), LANES), lambda i: (blk0 + i, 0)), row(LANES), row(LANES)],
        out_shape=[jax.ShapeDtypeStruct((t, d), F32), jax.ShapeDtypeStruct((hf_rows * (d // LANES), LANES), F32),
                   jax.ShapeDtypeStruct((t, LANES), I32), jax.ShapeDtypeStruct((t, LANES), F32)],
        input_output_aliases=aliases,
        compiler_params=_cparams(("arbitrary",), 40),
        name="route",
    )(*args)


ROW_UNROLL = 8


def _moe_body(grow_ref, srow_ref, be_ref, base_ref, cnt_ref, hf_hbm, wg_ref, wu_ref, wd_ref, o_hbm,
              xbuf, ybuf, wgb, wub, wdb, gsem, ssem, *, rb):
    g = pl.program_id(0)
    ng = pl.num_programs(0)
    slot = lax.rem(g, 2)
    cnt = cnt_ref[g]

    def for_rows(n, fn):
        n_grp = lax.shift_right_logical(n, ROW_UNROLL.bit_length() - 1)

        def grp(q, c):
            for u in range(ROW_UNROLL):
                fn(q * ROW_UNROLL + u)
            return c

        def one(i, c):
            fn(i)
            return c

        lax.fori_loop(0, n_grp, grp, 0)
        lax.fori_loop(n_grp * ROW_UNROLL, n, one, 0)

    tr = xbuf.shape[0] // (2 * rb)

    def tok(ref, i, n=1):
        return ref.at[pl.ds(i * tr, n * tr), :]

    def gather_row(sl, i, src):
        return pltpu.make_async_copy(tok(hf_hbm, src), tok(xbuf, sl * rb + i), gsem.at[sl])

    def scatter_row(sl, i, dst):
        return pltpu.make_async_copy(tok(ybuf, sl * rb + i), tok(o_hbm, dst), ssem.at[sl])

    def start_gathers(blk, sl):
        base = base_ref[blk]
        for_rows(cnt_ref[blk], lambda i: gather_row(sl, i, grow_ref[base + i]).start())

    def start_scatters(blk, sl):
        base = base_ref[blk]
        for_rows(cnt_ref[blk], lambda i: scatter_row(sl, i, srow_ref[base + i]).start())

    def wait_rows(blk, sl, row_copy, block_copy):
        n = cnt_ref[blk]

        @pl.when(n == rb)
        def _():
            block_copy.wait()

        @pl.when(n < rb)
        def _():
            for_rows(n, lambda i: row_copy(sl, i, 0).wait())

    def wait_gathers(blk, sl):
        wait_rows(blk, sl, gather_row,
                  pltpu.make_async_copy(tok(hf_hbm, 0, rb), tok(xbuf, sl * rb, rb), gsem.at[sl]))

    def wait_scatters(blk, sl):
        wait_rows(blk, sl, scatter_row,
                  pltpu.make_async_copy(tok(ybuf, sl * rb, rb), tok(o_hbm, 0, rb), ssem.at[sl]))

    @pl.when(g == 0)
    def _():
        xbuf[...] = jnp.zeros_like(xbuf)
        start_gathers(0, 0)

    prev_e = be_ref[jnp.maximum(g - 1, 0)]

    @pl.when((cnt > 0) & ((g == 0) | (prev_e != be_ref[g])))
    def _():
        wgb[...] = wg_ref[...].astype(BF16)
        wub[...] = wu_ref[...].astype(BF16)
        wdb[...] = wd_ref[...].astype(BF16)

    wait_gathers(g, slot)

    @pl.when(g + 1 < ng)
    def _():
        start_gathers(g + 1, 1 - slot)

    @pl.when(g >= 2)
    def _():
        wait_scatters(g - 2, slot)

    @pl.when(cnt > 0)
    def _():
        xb = _load_token_rows(xbuf, rb, tr, slot * rb).astype(BF16)
        hg = _mm(xb, wgb[...])
        act = (hg * jax.nn.sigmoid(hg)) * _mm(xb, wub[...])
        _store_token_rows(ybuf, _mm(act.astype(BF16), wdb[...]), tr, slot * rb)
        start_scatters(g, slot)

    @pl.when(g == ng - 1)
    def _():
        @pl.when(g >= 1)
        def _():
            wait_scatters(g - 1, 1 - slot)

        wait_scatters(g, slot)


def _moe_ffn(hf, grow, srow, blk_e, blk_base, blk_cnt, wg, wu, wd, layer, rb):
    d, ff = wg.shape[-2:]
    tr = d // LANES
    n_asg = grow.shape[0]
    n_blk = blk_e.shape[0]
    wspec = lambda shape: pl.BlockSpec((None, None) + shape, lambda g, gr, sr, be, bb, bc: (layer, be[g], 0, 0))
    return pl.pallas_call(
        functools.partial(_moe_body, rb=rb),
        grid_spec=pltpu.PrefetchScalarGridSpec(
            num_scalar_prefetch=5,
            grid=(n_blk,),
            in_specs=[pl.BlockSpec(memory_space=pl.ANY), wspec((d, ff)), wspec((d, ff)), wspec((ff, d))],
            out_specs=pl.BlockSpec(memory_space=pl.ANY),
            scratch_shapes=[pltpu.VMEM((2 * rb * tr, LANES), F32), pltpu.VMEM((2 * rb * tr, LANES), F32),
                            pltpu.VMEM((d, ff), BF16), pltpu.VMEM((d, ff), BF16), pltpu.VMEM((ff, d), BF16),
                            pltpu.SemaphoreType.DMA((2,)), pltpu.SemaphoreType.DMA((2,))]),
        out_shape=jax.ShapeDtypeStruct((n_asg * tr, LANES), F32),
        compiler_params=_cparams(("arbitrary",), 56),
        name="moe_ffn",
    )(grow, srow, blk_e, blk_base, blk_cnt, hf, wg, wu, wd)


def _moe_plan(eid, n_tok, rb):
    n_asg = n_tok * MOE_TOPK
    flat_e = eid.reshape(-1)
    order = jnp.argsort(flat_e, stable=True).astype(I32)
    grow = order // MOE_TOPK
    srow = (order % MOE_TOPK) * n_tok + grow
    counts = jnp.sum((flat_e[:, None] == jnp.arange(MOE_EXPERTS, dtype=I32)[None, :]).astype(I32), axis=0)
    starts = jnp.cumsum(counts) - counts
    nblk = (counts + rb - 1) // rb
    blk_end = jnp.cumsum(nblk)
    n_blk = -(-n_asg // rb) + MOE_EXPERTS
    gi = jnp.arange(n_blk, dtype=I32)
    be = jnp.minimum(jnp.sum((gi[:, None] >= blk_end[None, :]).astype(I32), axis=1), MOE_EXPERTS - 1)
    r = gi - (blk_end - nblk)[be]
    cnt = jnp.clip(counts[be] - r * rb, 0, rb).astype(I32)
    base = jnp.minimum(starts[be] + r * rb, n_asg - 1).astype(I32)
    return grow.astype(I32), srow.astype(I32), be, base, cnt


def _combine_body(x2_ref, o0_ref, o1_ref, g_ref, fw_ref, x3_ref, *, final):
    g = g_ref[...]
    n, c = x2_ref.shape[0], x2_ref.shape[1] // LANES
    x3 = x2_ref[...] + (g[:, 0:1] * _load_token_rows(o0_ref, n, c) + g[:, 1:2] * _load_token_rows(o1_ref, n, c))
    x3_ref[...] = _rms(x3, fw_ref[...]) if final else x3


def _combine(x2, o2, gates, fw, final, row0, n_all):
    t, d = x2.shape
    tm = _tile(math.gcd(t, row0, n_all), (256, 128, 64, 32, 16, 8))
    assert t % tm == 0 and row0 % tm == 0 and n_all % tm == 0
    nt = t // tm
    b0, b1 = row0 // tm, (n_all + row0) // tm
    return pl.pallas_call(
        functools.partial(_combine_body, final=final),
        grid=(nt,),
        in_specs=[pl.BlockSpec((tm, d), lambda i: (i, 0)),
                  pl.BlockSpec((tm * (d // LANES), LANES), lambda i: (b0 + i, 0)),
                  pl.BlockSpec((tm * (d // LANES), LANES), lambda i: (b1 + i, 0)),
                  pl.BlockSpec((tm, LANES), lambda i: (i, 0)),
                  pl.BlockSpec((1, d), lambda i: (0, 0))],
        out_specs=pl.BlockSpec((tm, d), lambda i: (i, 0)),
        out_shape=jax.ShapeDtypeStruct((t, d), F32),
        compiler_params=_cparams(("arbitrary",), 32),
        name="combine",
    )(x2, o2, o2, gates, fw)


def _rope_tables(pos):
    half = ATTN_DH // 2
    inv = ROPE_THETA ** (-jnp.arange(half, dtype=F32) / half)
    ang = pos.astype(F32)[:, None] * inv[None, :]
    cos, sin = jnp.cos(ang), jnp.sin(ang)
    reps = LANES // ATTN_DH
    return jnp.tile(jnp.concatenate([cos, cos], axis=1), (1, reps)), jnp.tile(jnp.concatenate([-sin, sin], axis=1), (1, reps))


def _block_diag(w):
    g, a, b = w.shape
    eye = jnp.eye(g, dtype=w.dtype)
    return (w[:, :, None, :] * eye[:, None, :, None]).reshape(g * a, g * b)


def _pad_lanes(v, n):
    return jnp.pad(v.reshape(1, -1), ((0, 0), (0, n - v.size)))


def _layer_front(x, grp, lw, layer, lam_init, hf_rows, hf_row0, hf_shared):
    bsz, seq = grp["bsz"], grp["seq"]
    u = _in_proj(x, lw["norm_mix"], lw["w_in"], grp["cos"], grp["sin"])
    lams = lw["lams"]
    if grp["paged"]:
        attn = _attn_sample(u, grp["cache_k"], grp["cache_v"], grp["page_flat"], layer, grp["n_pool"], lams,
                            lw["subln"], bsz, seq, grp["n_pages"], lam_init)
        k_new = v_new = None
    else:
        attn, k_new, v_new = _attn_prompt(u, lams, lw["subln"], bsz, seq, lam_init)
    ssd, h_ssd = _ssd(u, grp["ssd_conv"][layer], grp["ssd_state"], layer, lw["ssd_conv_w"], lw["ssd_conv_b"],
                      lw["ssd_dt_bias"], lw["ssd_a_log"], lw["ssd_d"], lw["ssd_norm"], bsz, seq)
    lru, h_lru = _lru(u, grp["lru_conv"][layer], grp["lru_state"][layer], lw["lru_conv_w"], lw["lru_conv_b"],
                      lw["lru_wa"], lw["lru_ba"], lw["lru_wx"], lw["lru_bx"], lw["lru_lambda"], bsz, seq)
    x1, qm = _out_proj(x, attn, ssd, lru, lw["w_out"], lw["norm_mem"], lw["wq_mem"])
    if grp["paged"]:
        om = _xattn(qm, grp["mem_k"], grp["mem_v"], bsz, seq, layer)
    else:
        om = _xattn(qm, grp["mem_k"], grp["mem_v"], bsz, seq)
    routed = _route(x1, om, lw["wo_mem"], lw["norm_ffn"], lw["router_w"], lw["router_b"],
                    hf_rows, hf_row0, hf_shared)
    ur = u.reshape(bsz, seq, UW)
    width_kv = ATTN_KV_HEADS * ATTN_VD
    kv_shape = (bsz, seq, ATTN_KV_HEADS, ATTN_VD)
    outs = dict(
        k=(ur[:, :, K0:K0 + width_kv] if k_new is None else k_new).reshape(kv_shape),
        v=(ur[:, :, V0:V0 + width_kv] if v_new is None else v_new).reshape(kv_shape),
        ssd_conv=ur[:, seq - (SSD_CONV - 1):, XBC0:Z0],
        ssd_state=h_ssd.reshape(bsz, SSD_HEADS, SSD_HEAD_DIM, SSD_STATE),
        lru_conv=ur[:, seq - (LRU_CONV - 1):, XR0:GATE0],
        lru_state=h_lru.reshape(bsz, -1),
    )
    return routed, outs


def _moe_groups(routed, lw, layer, final, final_norm):
    hf = routed[-1][1]
    eid = jnp.concatenate([r[2][:, :MOE_TOPK] for r in routed], axis=0)
    n_tok = eid.shape[0]
    rb = 256 if n_tok * MOE_TOPK >= 256 * MOE_EXPERTS else 128
    plan = _moe_plan(eid, n_tok, rb)
    o2 = _moe_ffn(hf, *plan, lw["moe_w_gate"], lw["moe_w_up"], lw["moe_w_down"], layer, rb)
    outs, row0 = [], 0
    for x2, _, _, gate in routed:
        outs.append(_combine(x2, o2, gate, final_norm, final, row0, n_tok))
        row0 += x2.shape[0]
    return outs


def _conv_state8(st):
    return jnp.pad(st, ((0, 0), (0, 0), (SUBLANES - st.shape[2], 0), (0, 0)))


def kernel(x_prompt, x_sample, cache_k, cache_v, cache_mem_k, cache_mem_v, state_ssd_conv, state_ssd, state_lru_conv, state_lru, page_table, mem_prompt, norm_mix, w_in, attn_lambda_q1, attn_lambda_k1, attn_lambda_q2, attn_lambda_k2, attn_subln, ssd_conv_w, ssd_conv_b, ssd_dt_bias, ssd_a_log, ssd_d, ssd_norm, lru_conv_w, lru_conv_b, lru_wa, lru_ba, lru_wx, lru_bx, lru_lambda, w_out, norm_mem, wq_mem, wk_mem, wv_mem, wo_mem, norm_ffn, router_group_w, router_group_b, router_expert_w, router_expert_b, moe_w_gate, moe_w_up, moe_w_down, final_norm):
    depth = w_in.shape[0]
    bp, tp, d = x_prompt.shape
    bs, ts, _ = x_sample.shape
    n_pool, page = cache_k.shape[1], cache_k.shape[2]
    n_pages = page_table.shape[1]
    past_len = n_pages * page
    n_mem = mem_prompt.shape[1]
    ssd_cc = state_ssd_conv.shape[-1]
    lru_w = state_lru.shape[-1]
    kvw = ATTN_KV_HEADS * ATTN_VD
    memw = MEM_HEADS * MEM_DH

    cos_p, sin_p = _rope_tables(jnp.tile(jnp.arange(tp, dtype=I32), bp))
    cos_s, sin_s = _rope_tables(jnp.tile(past_len + jnp.arange(ts, dtype=I32), bs))

    prompt = dict(bsz=bp, seq=tp, paged=False, cos=cos_p, sin=sin_p,
                  ssd_conv=jnp.zeros((depth, bp, SUBLANES, ssd_cc), F32),
                  ssd_state=jnp.zeros((depth, bp, SSD_HEADS * SSD_HEAD_DIM, SSD_STATE), F32),
                  lru_conv=jnp.zeros((depth, bp, SUBLANES, lru_w), F32),
                  lru_state=jnp.zeros((depth, bp, 1, lru_w), F32))
    sample = dict(bsz=bs, seq=ts, paged=True, cos=cos_s, sin=sin_s,
                  cache_k=cache_k.reshape(depth, n_pool, page * ATTN_KV_HEADS, ATTN_VD),
                  cache_v=cache_v.reshape(depth, n_pool, page * ATTN_KV_HEADS, ATTN_VD),
                  page_flat=page_table.reshape(-1), n_pool=n_pool, n_pages=n_pages,
                  ssd_conv=_conv_state8(state_ssd_conv),
                  ssd_state=state_ssd.reshape(depth, bs, SSD_HEADS * SSD_HEAD_DIM, SSD_STATE),
                  lru_conv=_conv_state8(state_lru_conv),
                  lru_state=state_lru.reshape(depth, bs, 1, lru_w),
                  mem_k=cache_mem_k.reshape(depth, bs, n_mem * MEM_HEADS, MEM_DH),
                  mem_v=cache_mem_v.reshape(depth, bs, n_mem * MEM_HEADS, MEM_DH))

    xp = x_prompt.reshape(bp * tp, d)
    xs = x_sample.reshape(bs * ts, d)
    po, so, mks, mvs = [], [], [], []
    for l in range(depth):
        w = w_in[l]
        zc = V0 + kvw
        xc0 = zc + SSD_HEADS * SSD_HEAD_DIM
        dc = xc0 + ssd_cc
        rc = dc + SSD_HEADS
        w_pad = jnp.concatenate([w[:, :zc], w[:, xc0:dc], w[:, zc:xc0], w[:, rc:rc + 2 * lru_w], w[:, dc:rc],
                                 jnp.zeros((d, UW - DT0 - SSD_HEADS), F32)], axis=1).astype(BF16)
        router_w = jnp.pad(jnp.concatenate([router_group_w[l], router_expert_w[l]], axis=1),
                           ((0, 0), (0, LANES - MOE_GROUPS - MOE_EXPERTS)))
        router_b = _pad_lanes(jnp.concatenate([router_group_b[l], router_expert_b[l]]), LANES)
        lw = dict(
            norm_mix=norm_mix[l].reshape(1, d), w_in=w_pad,
            lams=[v[l].reshape(1, ATTN_DH) for v in (attn_lambda_q1, attn_lambda_k1, attn_lambda_q2, attn_lambda_k2)],
            subln=attn_subln[l].reshape(1, ATTN_VD),
            ssd_conv_w=ssd_conv_w[l], ssd_conv_b=ssd_conv_b[l].reshape(1, -1),
            ssd_dt_bias=_pad_lanes(ssd_dt_bias[l], LANES), ssd_a_log=_pad_lanes(ssd_a_log[l], LANES),
            ssd_d=jnp.repeat(ssd_d[l], SSD_HEAD_DIM).reshape(1, -1), ssd_norm=ssd_norm[l].reshape(1, -1),
            lru_conv_w=lru_conv_w[l], lru_conv_b=lru_conv_b[l].reshape(1, -1),
            lru_wa=_block_diag(lru_wa[l]).astype(BF16), lru_ba=lru_ba[l].reshape(1, -1),
            lru_wx=_block_diag(lru_wx[l]).astype(BF16), lru_bx=lru_bx[l].reshape(1, -1),
            lru_lambda=lru_lambda[l].reshape(1, -1),
            w_out=w_out[l].astype(BF16), norm_mem=norm_mem[l].reshape(1, d), wq_mem=wq_mem[l].astype(BF16),
            wo_mem=wo_mem[l].astype(BF16), norm_ffn=norm_ffn[l].reshape(1, d),
            router_w=router_w, router_b=router_b,
            moe_w_gate=moe_w_gate, moe_w_up=moe_w_up, moe_w_down=moe_w_down,
        )
        lam_init = 0.8 - 0.6 * math.exp(-0.3 * l)
        mk, mv = _mem_kv(mem_prompt.reshape(bp * n_mem, d), wk_mem[l].astype(BF16), wv_mem[l].astype(BF16))
        mks.append(mk.reshape(bp, n_mem, MEM_HEADS, MEM_DH))
        mvs.append(mv.reshape(bp, n_mem, MEM_HEADS, MEM_DH))
        pg = dict(prompt, mem_k=mk.reshape(bp, n_mem, memw), mem_v=mv.reshape(bp, n_mem, memw))
        n_all = xp.shape[0] + xs.shape[0]
        routed_p, o = _layer_front(xp, pg, lw, l, lam_init, n_all, 0, None)
        po.append(o)
        routed_s, o = _layer_front(xs, sample, lw, l, lam_init, n_all, xp.shape[0], routed_p[1])
        so.append(o)
        xp, xs = _moe_groups([routed_p, routed_s], lw, l, l == depth - 1, final_norm.reshape(1, d))

    st = lambda outs, key: jnp.stack([o[key] for o in outs])
    return (xp.reshape(bp, tp, d), xs.reshape(bs, ts, d),
            st(po, "k"), st(po, "v"), jnp.stack(mks), jnp.stack(mvs),
            st(po, "ssd_conv"), st(po, "ssd_state"), st(po, "lru_conv"), st(po, "lru_state"),
            st(so, "k"), st(so, "v"), st(so, "ssd_conv"), st(so, "ssd_state"), st(so, "lru_conv"), st(so, "lru_state"))
```
